```python
import math
import jax, jax.numpy as jnp
from jax import lax
import numpy as np

D_MODEL = 2048
BATCH = 2
SEQ = 4096
DEPTH = 2

HEAD_DIM = 128
N_GROUPS = 4
GROUP_HEADS = (D_MODEL // HEAD_DIM) // N_GROUPS
GROUP_WIDTH = GROUP_HEADS * HEAD_DIM
MIX_WIDTH = N_GROUPS * GROUP_WIDTH
DIFF_HEADS = GROUP_HEADS
DIFF_QK_DIM = HEAD_DIM // 2
CONV_CH = GROUP_WIDTH
CONV_WIDTH = 31
GQA_Q_HEADS = GROUP_HEADS
GQA_KV_HEADS = GROUP_HEADS // 2
NA_HEADS = GROUP_HEADS
NA_KH_MAX = 8
NA_KW = 16
GRID_W = 64
Q_BLOCK = 128
ROPE_THETA = 10000.0
FFN_HIDDEN = ((8 * D_MODEL + 3 * 256 - 1) // (3 * 256)) * 256
EPS = 1e-6

SPLIT_SIZES = (
    DIFF_HEADS * 2 * DIFF_QK_DIM, DIFF_HEADS * 2 * DIFF_QK_DIM, DIFF_HEADS * HEAD_DIM,
    2 * CONV_CH,
    GQA_Q_HEADS * HEAD_DIM, GQA_KV_HEADS * HEAD_DIM, GQA_KV_HEADS * HEAD_DIM,
    NA_HEADS * HEAD_DIM, NA_HEADS * HEAD_DIM, NA_HEADS * HEAD_DIM,
)
IN_COLS = sum(SPLIT_SIZES)

kernel_name = "hybrid_parallel_heads_encoder"


def rmsnorm(x, g):
    xf = x.astype(jnp.float32)
    y = xf * lax.rsqrt(jnp.mean(xf * xf, axis=-1, keepdims=True) + EPS)
    return (y * g.astype(jnp.float32)).astype(x.dtype)


def layernorm(x, g, b):
    xf = x.astype(jnp.float32)
    mu = jnp.mean(xf, axis=-1, keepdims=True)
    var = jnp.mean(jnp.square(xf - mu), axis=-1, keepdims=True)
    y = (xf - mu) * lax.rsqrt(var + EPS)
    return (y * g.astype(jnp.float32) + b.astype(jnp.float32)).astype(x.dtype)


def rope(x, pos):
    d = x.shape[-1]
    half = d // 2
    inv = jnp.power(ROPE_THETA, -jnp.arange(0, d, 2, dtype=jnp.float32) / d)
    ang = pos[:, None] * inv[None, :]
    cos, sin = jnp.cos(ang), jnp.sin(ang)
    xf = x.astype(jnp.float32)
    x1, x2 = xf[..., :half], xf[..., half:]
    return jnp.concatenate([x1 * cos - x2 * sin, x1 * sin + x2 * cos], axis=-1).astype(x.dtype)


def axial_rope(x, row, col):
    half = x.shape[-1] // 2
    return jnp.concatenate([rope(x[..., :half], row), rope(x[..., half:], col)], axis=-1)


def diff_attention(q, k, v, lam_params, subln_g, layer_idx):
    B, S, _ = q.shape
    H, DK = DIFF_HEADS, DIFF_QK_DIM
    pos = jnp.arange(S, dtype=jnp.float32)
    q = rope(q.reshape(B, S, H, 2, DK).transpose(0, 2, 3, 1, 4), pos)
    k = rope(k.reshape(B, S, H, 2, DK).transpose(0, 2, 3, 1, 4), pos)
    v = v.reshape(B, S, H, HEAD_DIM).transpose(0, 2, 1, 3)
    lam_init = 0.8 - 0.6 * math.exp(-0.3 * layer_idx)
    lp = lam_params.astype(jnp.float32)
    lam = jnp.exp(jnp.sum(lp[0] * lp[1])) - jnp.exp(jnp.sum(lp[2] * lp[3])) + lam_init
    scale = DK ** -0.5
    nb = S // Q_BLOCK
    qb = q.reshape(B, H, 2, nb, Q_BLOCK, DK).transpose(3, 0, 1, 2, 4, 5)

    def block(qi):
        s = jnp.einsum('bhmqd,bhmkd->bhmqk', qi, k).astype(jnp.float32) * scale
        p = jax.nn.softmax(s, axis=-1)
        w = (p[:, :, 0] - lam * p[:, :, 1]).astype(v.dtype)
        return jnp.einsum('bhqk,bhkd->bhqd', w, v)

    o = lax.map(block, qb)
    o = rmsnorm(o, subln_g) * (1.0 - lam_init)
    return o.transpose(1, 0, 3, 2, 4).reshape(B, S, H * HEAD_DIM)


def conformer_conv(h, dw, dw_b, ln_g, ln_b, pw, pw_b):
    a, g = jnp.split(h, 2, axis=-1)
    u = a * jax.nn.sigmoid(g)
    pad = CONV_WIDTH // 2
    u = lax.conv_general_dilated(u, dw[:, None, :], window_strides=(1,), padding=[(pad, pad)],
                                 dimension_numbers=('NWC', 'WIO', 'NWC'),
                                 feature_group_count=CONV_CH) + dw_b
    u = jax.nn.silu(layernorm(u, ln_g, ln_b))
    return u @ pw + pw_b


def gqa_axial_attention(q, k, v, q_norm, k_norm):
    B, S, _ = q.shape
    HQ, HKV, R = GQA_Q_HEADS, GQA_KV_HEADS, GQA_Q_HEADS // GQA_KV_HEADS
    t = jnp.arange(S)
    row = (t // GRID_W).astype(jnp.float32)
    col = (t % GRID_W).astype(jnp.float32)
    q = axial_rope(rmsnorm(q.reshape(B, S, HQ, HEAD_DIM), q_norm).transpose(0, 2, 1, 3), row, col)
    k = axial_rope(rmsnorm(k.reshape(B, S, HKV, HEAD_DIM), k_norm).transpose(0, 2, 1, 3), row, col)
    v = v.reshape(B, S, HKV, HEAD_DIM).transpose(0, 2, 1, 3)
    scale = HEAD_DIM ** -0.5
    nb = S // Q_BLOCK
    qb = q.reshape(B, HKV, R, nb, Q_BLOCK, HEAD_DIM).transpose(3, 0, 1, 2, 4, 5)

    def block(qi):
        s = jnp.einsum('bgrqd,bgkd->bgrqk', qi, k).astype(jnp.float32) * scale
        p = jax.nn.softmax(s, axis=-1).astype(v.dtype)
        return jnp.einsum('bgrqk,bgkd->bgrqd', p, v)

    o = lax.map(block, qb)
    return o.transpose(1, 0, 4, 2, 3, 5).reshape(B, S, HQ * HEAD_DIM)


def neighbourhood_attention(q, k, v, rpb):
    B, S, _ = q.shape
    H = NA_HEADS
    rows = S // GRID_W
    kh = min(NA_KH_MAX, rows)
    kw = min(NA_KW, GRID_W)
    qr_blk = Q_BLOCK // GRID_W
    nb = S // Q_BLOCK
    band = min(kh + qr_blk - 1, rows)
    q = q.reshape(B, S, H, HEAD_DIM).transpose(0, 2, 1, 3).reshape(B, H, nb, Q_BLOCK, HEAD_DIM)
    k = k.reshape(B, S, H, HEAD_DIM).transpose(0, 2, 1, 3).reshape(B, H, rows, GRID_W, HEAD_DIM)
    v = v.reshape(B, S, H, HEAD_DIM).transpose(0, 2, 1, 3).reshape(B, H, rows, GRID_W, HEAD_DIM)
    blk = jnp.arange(nb)
    band_start = jnp.clip(blk * qr_blk - kh // 2, 0, rows - band)
    key_rows = band_start[:, None] + jnp.arange(band)
    kb = k[:, :, key_rows].reshape(B, H, nb, band * GRID_W, HEAD_DIM)
    vb = v[:, :, key_rows].reshape(B, H, nb, band * GRID_W, HEAD_DIM)
    q_local = jnp.arange(Q_BLOCK)
    q_row = blk[:, None] * qr_blk + (q_local // GRID_W)[None, :]
    q_col = jnp.broadcast_to((q_local % GRID_W)[None, :], (nb, Q_BLOCK))
    k_row = jnp.repeat(key_rows, GRID_W, axis=1)[:, None, :]
    k_col = jnp.tile(jnp.arange(GRID_W), band)[None, None, :]
    win_r = jnp.clip(q_row - kh // 2, 0, rows - kh)[:, :, None]
    win_c = jnp.clip(q_col - kw // 2, 0, GRID_W - kw)[:, :, None]
    mask = (k_row >= win_r) & (k_row < win_r + kh) & (k_col >= win_c) & (k_col < win_c + kw)
    ir = jnp.clip(k_row - q_row[:, :, None] + NA_KH_MAX - 1, 0, 2 * NA_KH_MAX - 2)
    ic = jnp.clip(k_col - q_col[:, :, None] + NA_KW - 1, 0, 2 * NA_KW - 2)
    bias = rpb[:, ir, ic].astype(jnp.float32)
    s = jnp.einsum('bhnqd,bhnkd->bhnqk', q, kb).astype(jnp.float32) * (HEAD_DIM ** -0.5) + bias[None]
    s = jnp.where(mask[None, None], s, -1e30)
    p = jax.nn.softmax(s, axis=-1).astype(vb.dtype)
    o = jnp.einsum('bhnqk,bhnkd->bhnqd', p, vb)
    return o.transpose(0, 2, 3, 1, 4).reshape(B, S, H * HEAD_DIM)


def setup_inputs(seed: int = 0) -> dict:
    key = jax.random.key(seed)
    ks = jax.random.split(key, 24)
    L, D, F = DEPTH, D_MODEL, FFN_HIDDEN
    f32 = jnp.float32

    def nrm(k, shape, scale):
        return jax.random.normal(k, shape, f32) * scale

    def gain(k, shape):
        return 1.0 + 0.05 * jax.random.normal(k, shape, f32)

    return {
        "x": nrm(ks[0], (BATCH, SEQ, D), 1.0),
        "norm_mix_pre": gain(ks[1], (L, D)),
        "norm_mix_post": gain(ks[2], (L, D)),
        "norm_ffn_pre": gain(ks[3], (L, D)),
        "norm_ffn_post": gain(ks[4], (L, D)),
        "w_in": nrm(ks[5], (L, D, IN_COLS), D ** -0.5),
        "w_out": nrm(ks[6], (L, MIX_WIDTH, D), MIX_WIDTH ** -0.5),
        "diff_lambda": nrm(ks[7], (L, 4, DIFF_QK_DIM), 0.1),
        "diff_subln": gain(ks[8], (L, HEAD_DIM)),
        "conv_dw": nrm(ks[9], (L, CONV_WIDTH, CONV_CH), CONV_WIDTH ** -0.5),
        "conv_dw_b": nrm(ks[10], (L, CONV_CH), 0.02),
        "conv_ln_g": gain(ks[11], (L, CONV_CH)),
        "conv_ln_b": nrm(ks[12], (L, CONV_CH), 0.02),
        "conv_pw": nrm(ks[13], (L, CONV_CH, CONV_CH), CONV_CH ** -0.5),
        "conv_pw_b": nrm(ks[14], (L, CONV_CH), 0.02),
        "gqa_q_norm": gain(ks[15], (L, HEAD_DIM)),
        "gqa_k_norm": gain(ks[16], (L, HEAD_DIM)),
        "na_rpb": nrm(ks[17], (L, NA_HEADS, 2 * NA_KH_MAX - 1, 2 * NA_KW - 1), 0.1),
        "ffn_gate": nrm(ks[18], (L, D, F), D ** -0.5),
        "ffn_up": nrm(ks[19], (L, D, F), D ** -0.5),
        "ffn_down": nrm(ks[20], (L, F, D), F ** -0.5),
    }


def reference(x, norm_mix_pre, norm_mix_post, norm_ffn_pre, norm_ffn_post, w_in, w_out,
              diff_lambda, diff_subln, conv_dw, conv_dw_b, conv_ln_g, conv_ln_b, conv_pw,
              conv_pw_b, gqa_q_norm, gqa_k_norm, na_rpb, ffn_gate, ffn_up, ffn_down):
    split_at = np.cumsum(SPLIT_SIZES)[:-1].tolist()
    for l in range(DEPTH):
        h = rmsnorm(x, norm_mix_pre[l])
        proj = h @ w_in[l]
        a_q, a_k, a_v, b_glu, c_q, c_k, c_v, d_q, d_k, d_v = jnp.split(proj, split_at, axis=-1)
        out_a = diff_attention(a_q, a_k, a_v, diff_lambda[l], diff_subln[l], l)
        out_b = conformer_conv(b_glu, conv_dw[l], conv_dw_b[l], conv_ln_g[l], conv_ln_b[l],
                               conv_pw[l], conv_pw_b[l])
        out_c = gqa_axial_attention(c_q, c_k, c_v, gqa_q_norm[l], gqa_k_norm[l])
        out_d = neighbourhood_attention(d_q, d_k, d_v, na_rpb[l])
        mixed = jnp.concatenate([out_a, out_b, out_c, out_d], axis=-1) @ w_out[l]
        x = x + rmsnorm(mixed, norm_mix_post[l])
        h = rmsnorm(x, norm_ffn_pre[l])
        f = (jax.nn.silu(h @ ffn_gate[l]) * (h @ ffn_up[l])) @ ffn_down[l]
        x = x + rmsnorm(f, norm_ffn_post[l])
    return x
```

```python
import functools
import math

import numpy as np
import jax
import jax.numpy as jnp
from jax import lax
from jax.experimental import pallas as pl
from jax.experimental.pallas import tpu as pltpu

D_MODEL = 2048
HEAD_DIM = 128
GROUP_HEADS = 4
GROUP_WIDTH = GROUP_HEADS * HEAD_DIM
DIFF_QK_DIM = HEAD_DIM // 2
CONV_WIDTH = 31
GQA_KV_HEADS = 2
NA_KH = 8
NA_KW = 16
GRID_W = 64
Q_BLOCK = 128
ROPE_THETA = 10000.0
EPS = 1e-6

COL_A_Q, COL_A_K, COL_A_V = 0, 512, 1024
COL_B_A, COL_B_G = 1536, 2048
COL_C_Q, COL_C_K, COL_C_V = 2560, 3072, 3328
COL_D_Q, COL_D_K, COL_D_V = 3584, 4096, 4608
IN_COLS = 5120

LANES = 128
VMEM_LIMIT = 56 * 1024 * 1024
ROW_CHUNK = 128

NA_BAND_ROWS = 10
NA_BAND = NA_BAND_ROWS * GRID_W
NA_CLASSES = 5
NA_T2 = 16
CONV_HALO = 16

F32 = jnp.float32
BF16 = jnp.bfloat16


def _cparams(semantics):
    return pltpu.CompilerParams(dimension_semantics=semantics,
                                vmem_limit_bytes=VMEM_LIMIT)


def _rmsnorm_rows_to(x_ref, g_ref, h_ref):
    n_rows = x_ref.shape[0]

    def body(c, _):
        r0 = pl.multiple_of(c * ROW_CHUNK, ROW_CHUNK)
        x = x_ref[pl.ds(r0, ROW_CHUNK), :]
        ms = jnp.mean(x * x, axis=-1, keepdims=True)
        h_ref[pl.ds(r0, ROW_CHUNK), :] = (x * lax.rsqrt(ms + EPS) * g_ref[...]).astype(h_ref.dtype)
        return 0

    lax.fori_loop(0, n_rows // ROW_CHUNK, body, 0)


def _norm_residual_rows(acc_ref, g_ref, x_ref, o_ref):
    n_rows = acc_ref.shape[0]

    def body(c, _):
        r0 = pl.multiple_of(c * ROW_CHUNK, ROW_CHUNK)
        f = acc_ref[pl.ds(r0, ROW_CHUNK), :]
        ms = jnp.mean(f * f, axis=-1, keepdims=True)
        o_ref[pl.ds(r0, ROW_CHUNK), :] = x_ref[pl.ds(r0, ROW_CHUNK), :] + f * lax.rsqrt(ms + EPS) * g_ref[...]
        return 0

    lax.fori_loop(0, n_rows // ROW_CHUNK, body, 0)


def _in_proj_kernel(x_ref, g_ref, w_ref, o_ref, h_ref):
    @pl.when(pl.program_id(1) == 0)
    def _():
        _rmsnorm_rows_to(x_ref, g_ref, h_ref)

    o_ref[...] = jnp.dot(h_ref[...], w_ref[...], preferred_element_type=F32).astype(o_ref.dtype)


def _in_proj(x, g, w, *, tm, tn):
    m, d = x.shape
    n = w.shape[1]
    return pl.pallas_call(
        _in_proj_kernel,
        grid=(m // tm, n // tn),
        in_specs=[pl.BlockSpec((tm, d), lambda i, j: (i, 0)),
                  pl.BlockSpec((1, d), lambda i, j: (0, 0)),
                  pl.BlockSpec((d, tn), lambda i, j: (0, j))],
        out_specs=pl.BlockSpec((tm, tn), lambda i, j: (i, j)),
        out_shape=jax.ShapeDtypeStruct((m, n), BF16),
        scratch_shapes=[pltpu.VMEM((tm, d), BF16)],
        compiler_params=_cparams(("parallel", "arbitrary")),
        name="in_proj",
    )(x, g, w)


def _ffn_up_kernel(x_ref, g_ref, wg_ref, wu_ref, o_ref, h_ref):
    @pl.when(pl.program_id(1) == 0)
    def _():
        _rmsnorm_rows_to(x_ref, g_ref, h_ref)

    h = h_ref[...]
    gate = jnp.dot(h, wg_ref[...], preferred_element_type=F32)
    up = jnp.dot(h, wu_ref[...], preferred_element_type=F32)
    o_ref[...] = (gate * jax.nn.sigmoid(gate) * up).astype(o_ref.dtype)


def _ffn_up(x, g, wg, wu, *, tm, tn):
    m, d = x.shape
    n = wg.shape[1]
    return pl.pallas_call(
        _ffn_up_kernel,
        grid=(m // tm, n // tn),
        in_specs=[pl.BlockSpec((tm, d), lambda i, j: (i, 0)),
                  pl.BlockSpec((1, d), lambda i, j: (0, 0)),
                  pl.BlockSpec((d, tn), lambda i, j: (0, j)),
                  pl.BlockSpec((d, tn), lambda i, j: (0, j))],
        out_specs=pl.BlockSpec((tm, tn), lambda i, j: (i, j)),
        out_shape=jax.ShapeDtypeStruct((m, n), BF16),
        scratch_shapes=[pltpu.VMEM((tm, d), BF16)],
        compiler_params=_cparams(("parallel", "arbitrary")),
        name="ffn_up",
    )(x, g, wg, wu)


def _ffn_down_kernel(a_ref, w_ref, g_ref, x_ref, o_ref, acc_ref):
    k = pl.program_id(1)

    @pl.when(k == 0)
    def _():
        acc_ref[...] = jnp.zeros_like(acc_ref)

    acc_ref[...] += jnp.dot(a_ref[...], w_ref[...], preferred_element_type=F32)

    @pl.when(k == pl.num_programs(1) - 1)
    def _():
        _norm_residual_rows(acc_ref, g_ref, x_ref, o_ref)


def _ffn_down(a, w, g, x, *, tm, tk):
    m, kdim = a.shape
    d = w.shape[1]
    return pl.pallas_call(
        _ffn_down_kernel,
        grid=(m // tm, kdim // tk),
        in_specs=[pl.BlockSpec((tm, tk), lambda i, k: (i, k)),
                  pl.BlockSpec((tk, d), lambda i, k: (k, 0)),
                  pl.BlockSpec((1, d), lambda i, k: (0, 0)),
                  pl.BlockSpec((tm, d), lambda i, k: (i, 0))],
        out_specs=pl.BlockSpec((tm, d), lambda i, k: (i, 0)),
        out_shape=jax.ShapeDtypeStruct((m, d), F32),
        scratch_shapes=[pltpu.VMEM((tm, d), F32)],
        compiler_params=_cparams(("parallel", "arbitrary")),
        name="ffn_down",
    )(a, w, g, x)


def _out_proj_kernel(a_ref, b_ref, c_ref, d_ref, w_ref, g_ref, x_ref, o_ref, lhs_ref, acc_ref):
    for i, part in enumerate((a_ref, b_ref, c_ref, d_ref)):
        lhs_ref[:, i * GROUP_WIDTH:(i + 1) * GROUP_WIDTH] = part[...]
    acc_ref[...] = jnp.dot(lhs_ref[...], w_ref[...], preferred_element_type=F32)
    _norm_residual_rows(acc_ref, g_ref, x_ref, o_ref)


def _out_proj(parts, w, g, x, *, tm):
    m, d = x.shape
    part_spec = pl.BlockSpec((tm, GROUP_WIDTH), lambda i: (i, 0))
    return pl.pallas_call(
        _out_proj_kernel,
        grid=(m // tm,),
        in_specs=[part_spec, part_spec, part_spec, part_spec,
                  pl.BlockSpec((4 * GROUP_WIDTH, d), lambda i: (0, 0)),
                  pl.BlockSpec((1, d), lambda i: (0, 0)),
                  pl.BlockSpec((tm, d), lambda i: (i, 0))],
        out_specs=pl.BlockSpec((tm, d), lambda i: (i, 0)),
        out_shape=jax.ShapeDtypeStruct((m, d), F32),
        scratch_shapes=[pltpu.VMEM((tm, 4 * GROUP_WIDTH), BF16),
                        pltpu.VMEM((tm, d), F32)],
        compiler_params=_cparams(("parallel",)),
        name="out_proj",
    )(*parts, w, g, x)


def _rope_tables(pos_lo, pos_hi):
    half = DIFF_QK_DIM // 2
    inv = ROPE_THETA ** (-np.arange(half, dtype=np.float64) / half)
    lane = np.arange(LANES)
    pos = np.where(lane[None, :] < DIFF_QK_DIM, pos_lo[:, None], pos_hi[:, None]).astype(np.float64)
    ang = pos * inv[lane % half][None, :]
    sign = np.where((lane % DIFF_QK_DIM) < half, -1.0, 1.0)
    return (jnp.asarray(np.cos(ang), dtype=F32),
            jnp.asarray(np.sin(ang) * sign[None, :], dtype=F32))


def _rope(x, cos, sin_signed):
    lane = lax.broadcasted_iota(jnp.int32, x.shape, 1)
    first_half = (lane & (DIFF_QK_DIM // 2)) == 0
    partner = jnp.where(first_half, pltpu.roll(x, LANES - DIFF_QK_DIM // 2, 1),
                        pltpu.roll(x, DIFF_QK_DIM // 2, 1))
    return x * cos + partner * sin_signed


def _rms(x):
    return x * lax.rsqrt(jnp.mean(x * x, axis=-1, keepdims=True) + EPS)


def _attend(qs_ref, krot_ref, v_ref, m_ref, l_ref, acc_ref, *, tk):
    seq = krot_ref.shape[0]
    m_ref[...] = jnp.full(m_ref.shape, -jnp.inf, F32)
    l_ref[...] = jnp.zeros_like(l_ref)
    acc_ref[...] = jnp.zeros_like(acc_ref)

    def body(j, _):
        off = pl.multiple_of(j * tk, tk)
        k = krot_ref[pl.ds(off, tk), :]
        v = v_ref[pl.ds(off, tk), :]
        s = lax.dot_general(qs_ref[...], k, (((1,), (1,)), ((), ())), preferred_element_type=F32)
        m_prev = m_ref[...]
        m_next = jnp.maximum(m_prev, jnp.max(s, axis=1, keepdims=True))
        p = jnp.exp(s - jnp.concatenate([m_next] * (tk // LANES), axis=1))
        alpha = jnp.exp(m_prev - m_next)
        l_ref[...] = alpha * l_ref[...] + jnp.sum(p, axis=1, keepdims=True)
        acc_ref[...] = alpha * acc_ref[...] + jnp.dot(p.astype(BF16), v, preferred_element_type=F32)
        m_ref[...] = m_next
        return 0

    lax.fori_loop(0, seq // tk, body, 0)


def _rotate_keys(k_ref, cos_ref, sin_ref, krot_ref, gain):
    seq = k_ref.shape[0]

    def body(c, _):
        r0 = pl.multiple_of(c * ROW_CHUNK, ROW_CHUNK)
        k = k_ref[pl.ds(r0, ROW_CHUNK), :].astype(F32)
        if gain is not None:
            k = _rms(k) * gain
        krot_ref[pl.ds(r0, ROW_CHUNK), :] = _rope(
            k, cos_ref[pl.ds(r0, ROW_CHUNK), :], sin_ref[pl.ds(r0, ROW_CHUNK), :]).astype(BF16)
        return 0

    lax.fori_loop(0, seq // ROW_CHUNK, body, 0)


def _diff_attn_kernel(q_ref, k_ref, v_ref, cq_ref, sq_ref, ck_ref, sk_ref, lam_ref, g_ref, o_ref,
                      krot_ref, qs_ref, m_ref, l_ref, acc_ref, *, tq, tk, lam_init):
    @pl.when(pl.program_id(1) == 0)
    def _():
        _rotate_keys(k_ref, ck_ref, sk_ref, krot_ref, None)

    q = _rope(q_ref[...].astype(F32), cq_ref[...], sq_ref[...]) * (DIFF_QK_DIM ** -0.5)
    lane = lax.broadcasted_iota(jnp.int32, q.shape, 1)
    qs_ref[0:tq, :] = jnp.where(lane < DIFF_QK_DIM, q, 0.0).astype(BF16)
    qs_ref[tq:2 * tq, :] = jnp.where(lane >= DIFF_QK_DIM, q, 0.0).astype(BF16)

    _attend(qs_ref, krot_ref, v_ref, m_ref, l_ref, acc_ref, tk=tk)

    lp = lam_ref[...]
    lam = (jnp.exp(jnp.sum(lp[0:1] * lp[1:2], keepdims=True))
           - jnp.exp(jnp.sum(lp[2:3] * lp[3:4], keepdims=True)) + lam_init)
    o = (acc_ref[0:tq, :] / l_ref[0:tq, :]
         - lam * (acc_ref[tq:2 * tq, :] / l_ref[tq:2 * tq, :]))
    o_ref[...] = (_rms(o) * g_ref[...] * (1.0 - lam_init)).astype(o_ref.dtype)


def _gqa_attn_kernel(q_ref, k_ref, v_ref, cq_ref, sq_ref, ck_ref, sk_ref, qn_ref, kn_ref, o_ref,
                     krot_ref, qs_ref, m_ref, l_ref, acc_ref, *, tq, tk):
    @pl.when(pl.program_id(1) == 0)
    def _():
        _rotate_keys(k_ref, ck_ref, sk_ref, krot_ref, kn_ref[...])

    for r in range(2):
        q = q_ref[:, r * HEAD_DIM:(r + 1) * HEAD_DIM].astype(F32)
        q = _rope(_rms(q) * qn_ref[...], cq_ref[...], sq_ref[...]) * (HEAD_DIM ** -0.5)
        qs_ref[r * tq:(r + 1) * tq, :] = q.astype(BF16)

    _attend(qs_ref, krot_ref, v_ref, m_ref, l_ref, acc_ref, tk=tk)

    for r in range(2):
        o = acc_ref[r * tq:(r + 1) * tq, :] / l_ref[r * tq:(r + 1) * tq, :]
        o_ref[:, r * HEAD_DIM:(r + 1) * HEAD_DIM] = o.astype(o_ref.dtype)


def _flash_call(kernel, proj3, tables, params, *, n_groups, q_width, q_col, k_col, v_col,
                out_width, tq, name):
    batch, seq, _ = proj3.shape
    cos_t, sin_t = tables
    qb, kb, vb = q_col // q_width, k_col // HEAD_DIM, v_col // HEAD_DIM

    def grp(i):
        return i // n_groups, i % n_groups

    in_specs = [
        pl.BlockSpec((None, tq, q_width), lambda i, j: (grp(i)[0], j, qb + grp(i)[1])),
        pl.BlockSpec((None, seq, HEAD_DIM), lambda i, j: (grp(i)[0], 0, kb + grp(i)[1])),
        pl.BlockSpec((None, seq, HEAD_DIM), lambda i, j: (grp(i)[0], 0, vb + grp(i)[1])),
        pl.BlockSpec((tq, LANES), lambda i, j: (j, 0)),
        pl.BlockSpec((tq, LANES), lambda i, j: (j, 0)),
        pl.BlockSpec((seq, LANES), lambda i, j: (0, 0)),
        pl.BlockSpec((seq, LANES), lambda i, j: (0, 0)),
    ] + [pl.BlockSpec(p.shape, lambda i, j: (0, 0)) for p in params]
    return pl.pallas_call(
        kernel,
        grid=(batch * n_groups, seq // tq),
        in_specs=in_specs,
        out_specs=pl.BlockSpec((None, tq, out_width), lambda i, j: (grp(i)[0], j, grp(i)[1])),
        out_shape=jax.ShapeDtypeStruct((batch, seq, GROUP_WIDTH), BF16),
        scratch_shapes=[pltpu.VMEM((seq, HEAD_DIM), BF16),
                        pltpu.VMEM((2 * tq, HEAD_DIM), BF16),
                        pltpu.VMEM((2 * tq, LANES), F32),
                        pltpu.VMEM((2 * tq, LANES), F32),
                        pltpu.VMEM((2 * tq, HEAD_DIM), F32)],
        compiler_params=_cparams(("parallel", "arbitrary")),
        name=name,
    )(proj3, proj3, proj3, cos_t, sin_t, cos_t, sin_t, *params)


def _conv_kernel(a_ref, g_ref, dw_ref, dwb_ref, lng_ref, lnb_ref, pw_ref, pwb_ref, o_ref, u_ref, *, tm):
    seq = a_ref.shape[0]
    t = pl.program_id(1)

    @pl.when(t == 0)
    def _():
        zeros = jnp.zeros((CONV_HALO, u_ref.shape[1]), F32)
        u_ref[0:CONV_HALO, :] = zeros
        u_ref[CONV_HALO + seq:2 * CONV_HALO + seq, :] = zeros

        def body(c, _):
            r0 = pl.multiple_of(c * ROW_CHUNK, ROW_CHUNK)
            a = a_ref[pl.ds(r0, ROW_CHUNK), :].astype(F32)
            g = g_ref[pl.ds(r0, ROW_CHUNK), :].astype(F32)
            u_ref[pl.ds(CONV_HALO + r0, ROW_CHUNK), :] = a * jax.nn.sigmoid(g)
            return 0

        lax.fori_loop(0, seq // ROW_CHUNK, body, 0)

    pad = CONV_WIDTH // 2
    t0 = pl.multiple_of(t * tm, tm)
    win = u_ref[pl.ds(t0, tm + 2 * CONV_HALO), :]
    acc = jnp.zeros((tm, u_ref.shape[1]), F32) + dwb_ref[...]
    for k in range(CONV_WIDTH):
        s = CONV_HALO - pad + k
        acc = acc + win[s:s + tm, :] * dw_ref[k:k + 1, :]
    mu = jnp.mean(acc, axis=-1, keepdims=True)
    cen = acc - mu
    var = jnp.mean(cen * cen, axis=-1, keepdims=True)
    y = cen * lax.rsqrt(var + EPS) * lng_ref[...] + lnb_ref[...]
    y = y * jax.nn.sigmoid(y)
    o_ref[...] = (jnp.dot(y.astype(BF16), pw_ref[...], preferred_element_type=F32)
                  + pwb_ref[...]).astype(o_ref.dtype)


def _conv_mixer(proj3, dw, dwb, lng, lnb, pw, pwb, *, tm):
    batch, seq, _ = proj3.shape
    ch = GROUP_WIDTH
    vec = pl.BlockSpec((1, ch), lambda b, t: (0, 0))
    return pl.pallas_call(
        functools.partial(_conv_kernel, tm=tm),
        grid=(batch, seq // tm),
        in_specs=[pl.BlockSpec((None, seq, ch), lambda b, t: (b, 0, COL_B_A // ch)),
                  pl.BlockSpec((None, seq, ch), lambda b, t: (b, 0, COL_B_G // ch)),
                  pl.BlockSpec((CONV_WIDTH, ch), lambda b, t: (0, 0)),
                  vec, vec, vec,
                  pl.BlockSpec((ch, ch), lambda b, t: (0, 0)),
                  vec],
        out_specs=pl.BlockSpec((None, tm, ch), lambda b, t: (b, t, 0)),
        out_shape=jax.ShapeDtypeStruct((batch, seq, ch), BF16),
        scratch_shapes=[pltpu.VMEM((seq + 2 * CONV_HALO, ch), F32)],
        compiler_params=_cparams(("parallel", "arbitrary")),
        name="conv_mixer",
    )(proj3, proj3, dw, dwb, lng, lnb, pw, pwb)


def _na_geometry(seq):
    rows = seq // GRID_W
    qr_blk = Q_BLOCK // GRID_W
    n_blk = seq // Q_BLOCK
    reps = (0, 1, 2, n_blk - 2, n_blk - 1)
    mask_add = np.zeros((NA_CLASSES, Q_BLOCK, NA_BAND), np.float32)
    q_local = np.arange(Q_BLOCK)
    p = np.arange(NA_BAND)
    for c, blk in enumerate(reps):
        bs = int(np.clip(blk * qr_blk - NA_KH // 2, 0, rows - NA_BAND_ROWS))
        q_row = (blk * qr_blk + q_local // GRID_W)[:, None]
        q_col = (q_local % GRID_W)[:, None]
        k_row = (bs + p // GRID_W)[None, :]
        k_col = (p % GRID_W)[None, :]
        win_r = np.clip(q_row - NA_KH // 2, 0, rows - NA_KH)
        win_c = np.clip(q_col - NA_KW // 2, 0, GRID_W - NA_KW)
        ok = (k_row >= win_r) & (k_row < win_r + NA_KH) & (k_col >= win_c) & (k_col < win_c + NA_KW)
        mask_add[c] = np.where(ok, 0.0, -1e30)
    return jnp.asarray(mask_add)


def _rpb_tiles_kernel(rpb_ref, o_ref):
    h = pl.program_id(0)
    n_c = 2 * NA_KW - 1
    n_r = 2 * NA_KH - 1
    shape = (GRID_W, LANES)
    lane = lax.broadcasted_iota(jnp.int32, shape, 1)
    q_col = lax.broadcasted_iota(jnp.int32, shape, 0)
    ic = jnp.clip((lane & (GRID_W - 1)) - q_col + (NA_KW - 1), 0, n_c - 1)
    left = lane < GRID_W
    base = h * (n_r * n_c)
    for i in range(NA_T2):
        i_l = min(max(i - 1, 0), n_r - 1)
        i_r = min(max(i, 0), n_r - 1)

        def body(j, acc, i_l=i_l, i_r=i_r):
            coef = jnp.where(left, rpb_ref[base + i_l * n_c + j], rpb_ref[base + i_r * n_c + j])
            return acc + jnp.where(ic == j, coef, 0.0)

        o_ref[i] = lax.fori_loop(0, n_c, body, jnp.zeros(shape, F32))


def _rpb_tiles(rpb):
    heads = rpb.shape[0]
    return pl.pallas_call(
        _rpb_tiles_kernel,
        grid=(heads,),
        in_specs=[pl.BlockSpec(memory_space=pltpu.SMEM)],
        out_specs=pl.BlockSpec((None, NA_T2, GRID_W, LANES), lambda h: (h, 0, 0, 0)),
        out_shape=jax.ShapeDtypeStruct((heads, NA_T2, GRID_W, LANES), F32),
        compiler_params=_cparams(("parallel",)),
        name="na_rpb_tiles",
    )(rpb.reshape(-1))


def _na_kernel(q_ref, k_ref, v_ref, t2_ref, mask_ref, o_ref):
    seq = q_ref.shape[0]
    rows = seq // GRID_W
    qr_blk = Q_BLOCK // GRID_W
    n_blk = seq // Q_BLOCK

    def body(blk, _):
        bs = jnp.clip(blk * qr_blk - NA_KH // 2, 0, rows - NA_BAND_ROWS)
        cls = jnp.where(blk < 2, blk, jnp.where(blk >= n_blk - 2, blk - (n_blk - NA_CLASSES), 2))
        q0 = pl.multiple_of(blk * Q_BLOCK, Q_BLOCK)
        k0 = pl.multiple_of(bs * GRID_W, GRID_W)
        q = q_ref[pl.ds(q0, Q_BLOCK), :]
        k = k_ref[pl.ds(k0, NA_BAND), :]
        v = v_ref[pl.ds(k0, NA_BAND), :]
        s = lax.dot_general(q, k, (((1,), (1,)), ((), ())), preferred_element_type=F32)
        bias_rows = []
        for a in range(qr_blk):
            tiles = []
            for r2 in range(NA_BAND_ROWS // 2):
                i1 = bs + 2 * r2 - (blk * qr_blk + a) + (NA_KH - 1)
                tiles.append(t2_ref[jnp.clip(i1, -1, NA_T2 - 2) + 1])
            bias_rows.append(jnp.concatenate(tiles, axis=1))
        s = s * (HEAD_DIM ** -0.5) + jnp.concatenate(bias_rows, axis=0) + mask_ref[cls]
        m = jnp.max(s, axis=1, keepdims=True)
        p = jnp.exp(s - m)
        l = jnp.sum(p, axis=1, keepdims=True)
        o = jnp.dot(p.astype(BF16), v, preferred_element_type=F32) / l
        o_ref[pl.ds(q0, Q_BLOCK), :] = o.astype(o_ref.dtype)
        return 0

    lax.fori_loop(0, n_blk, body, 0)


def _na_mixer(proj3, t2, mask_add):
    batch, seq, _ = proj3.shape
    heads = GROUP_HEADS

    def head_spec(col):
        return pl.BlockSpec((None, seq, HEAD_DIM), lambda b, h: (b, 0, col // HEAD_DIM + h))

    return pl.pallas_call(
        _na_kernel,
        grid=(batch, heads),
        in_specs=[head_spec(COL_D_Q), head_spec(COL_D_K), head_spec(COL_D_V),
                  pl.BlockSpec((None, NA_T2, GRID_W, LANES), lambda b, h: (h, 0, 0, 0)),
                  pl.BlockSpec(mask_add.shape, lambda b, h: (0, 0, 0))],
        out_specs=pl.BlockSpec((None, seq, HEAD_DIM), lambda b, h: (b, 0, h)),
        out_shape=jax.ShapeDtypeStruct((batch, seq, GROUP_WIDTH), BF16),
        compiler_params=_cparams(("parallel", "parallel")),
        name="na_mixer",
    )(proj3, proj3, proj3, t2, mask_add)


def kernel(x, norm_mix_pre, norm_mix_post, norm_ffn_pre, norm_ffn_post, w_in, w_out, diff_lambda, diff_subln, conv_dw, conv_dw_b, conv_ln_g, conv_ln_b, conv_pw, conv_pw_b, gqa_q_norm, gqa_k_norm, na_rpb, ffn_gate, ffn_up, ffn_down):
    batch, seq, d = x.shape
    depth = w_in.shape[0]
    tokens = batch * seq

    t = np.arange(seq)
    diff_tables = _rope_tables(t, t)
    axial_tables = _rope_tables(t // GRID_W, t % GRID_W)
    mask_add = _na_geometry(seq)

    def row(v):
        return v.reshape(1, -1)

    xf = x.reshape(tokens, d)
    for l in range(depth):
        lam_init = 0.8 - 0.6 * math.exp(-0.3 * l)
        proj = _in_proj(xf, row(norm_mix_pre[l]), w_in[l].astype(BF16), tm=1024, tn=1024)
        proj3 = proj.reshape(batch, seq, IN_COLS)

        out_a = _flash_call(
            functools.partial(_diff_attn_kernel, tq=256, tk=512, lam_init=lam_init),
            proj3, diff_tables, (diff_lambda[l], row(diff_subln[l])),
            n_groups=GROUP_HEADS, q_width=HEAD_DIM, q_col=COL_A_Q, k_col=COL_A_K, v_col=COL_A_V,
            out_width=HEAD_DIM, tq=256, name="diff_attn")
        out_b = _conv_mixer(proj3, conv_dw[l], row(conv_dw_b[l]), row(conv_ln_g[l]), row(conv_ln_b[l]),
                            conv_pw[l].astype(BF16), row(conv_pw_b[l]), tm=256)
        out_c = _flash_call(
            functools.partial(_gqa_attn_kernel, tq=256, tk=512),
            proj3, axial_tables, (row(gqa_q_norm[l]), row(gqa_k_norm[l])),
            n_groups=GQA_KV_HEADS, q_width=2 * HEAD_DIM, q_col=COL_C_Q, k_col=COL_C_K, v_col=COL_C_V,
            out_width=2 * HEAD_DIM, tq=256, name="gqa_attn")
        out_d = _na_mixer(proj3, _rpb_tiles(na_rpb[l]), mask_add)

        parts = [o.reshape(tokens, GROUP_WIDTH) for o in (out_a, out_b, out_c, out_d)]
        xf = _out_proj(parts, w_out[l].astype(BF16), row(norm_mix_post[l]), xf, tm=512)

        act = _ffn_up(xf, row(norm_ffn_pre[l]), ffn_gate[l].astype(BF16), ffn_up[l].astype(BF16),
                      tm=512, tn=1408)
        xf = _ffn_down(act, ffn_down[l].astype(BF16), row(norm_ffn_post[l]), xf, tm=1024, tk=512)
    return xf.reshape(batch, seq, d)
```

```python
import functools
import math

import numpy as np
import jax
import jax.numpy as jnp
from jax import lax
from jax.experimental import pallas as pl
from jax.experimental.pallas import tpu as pltpu

D_MODEL = 2048
HEAD_DIM = 128
GROUP_HEADS = 4
GROUP_WIDTH = GROUP_HEADS * HEAD_DIM
DIFF_QK_DIM = HEAD_DIM // 2
CONV_WIDTH = 31
GQA_KV_HEADS = 2
NA_KH = 8
NA_KW = 16
GRID_W = 64
Q_BLOCK = 128
ROPE_THETA = 10000.0
EPS = 1e-6

COL_A_Q, COL_A_K, COL_A_V = 0, 512, 1024
COL_B_A, COL_B_G = 1536, 2048
COL_C_Q, COL_C_K, COL_C_V = 2560, 3072, 3328
COL_D_Q, COL_D_K, COL_D_V = 3584, 4096, 4608
IN_COLS = 5120

LANES = 128
VMEM_LIMIT = 56 * 1024 * 1024
ROW_CHUNK = 128

NA_BAND_ROWS = 10
NA_BAND = NA_BAND_ROWS * GRID_W
NA_CLASSES = 5
NA_T2 = 16
CONV_HALO = 16

F32 = jnp.float32
BF16 = jnp.bfloat16


def _cparams(semantics):
    return pltpu.CompilerParams(dimension_semantics=semantics,
                                vmem_limit_bytes=VMEM_LIMIT)


def _rmsnorm_rows_to(x_ref, g_ref, h_ref):
    n_rows = x_ref.shape[0]

    def body(c, _):
        r0 = pl.multiple_of(c * ROW_CHUNK, ROW_CHUNK)
        x = x_ref[pl.ds(r0, ROW_CHUNK), :]
        ms = jnp.mean(x * x, axis=-1, keepdims=True)
        h_ref[pl.ds(r0, ROW_CHUNK), :] = (x * lax.rsqrt(ms + EPS) * g_ref[...]).astype(h_ref.dtype)
        return 0

    lax.fori_loop(0, n_rows // ROW_CHUNK, body, 0)


def _norm_residual_rows(acc_ref, g_ref, x_ref, o_ref):
    n_rows = acc_ref.shape[0]

    def body(c, _):
        r0 = pl.multiple_of(c * ROW_CHUNK, ROW_CHUNK)
        f = acc_ref[pl.ds(r0, ROW_CHUNK), :]
        ms = jnp.mean(f * f, axis=-1, keepdims=True)
        o_ref[pl.ds(r0, ROW_CHUNK), :] = x_ref[pl.ds(r0, ROW_CHUNK), :] + f * lax.rsqrt(ms + EPS) * g_ref[...]
        return 0

    lax.fori_loop(0, n_rows // ROW_CHUNK, body, 0)


def _in_proj_kernel(x_ref, g_ref, w_ref, o_ref, h_ref):
    @pl.when(pl.program_id(1) == 0)
    def _():
        _rmsnorm_rows_to(x_ref, g_ref, h_ref)

    o_ref[...] = jnp.dot(h_ref[...], w_ref[...], preferred_element_type=F32).astype(o_ref.dtype)


def _in_proj(x, g, w, *, tm, tn):
    m, d = x.shape
    n = w.shape[1]
    return pl.pallas_call(
        _in_proj_kernel,
        grid=(m // tm, n // tn),
        in_specs=[pl.BlockSpec((tm, d), lambda i, j: (i, 0)),
                  pl.BlockSpec((1, d), lambda i, j: (0, 0)),
                  pl.BlockSpec((d, tn), lambda i, j: (0, j))],
        out_specs=pl.BlockSpec((tm, tn), lambda i, j: (i, j)),
        out_shape=jax.ShapeDtypeStruct((m, n), BF16),
        scratch_shapes=[pltpu.VMEM((tm, d), BF16)],
        compiler_params=_cparams(("parallel", "arbitrary")),
        name="in_proj",
    )(x, g, w)


def _ffn_up_kernel(x_ref, g_ref, wg_ref, wu_ref, o_ref, h_ref):
    @pl.when(pl.program_id(1) == 0)
    def _():
        _rmsnorm_rows_to(x_ref, g_ref, h_ref)

    h = h_ref[...]
    gate = jnp.dot(h, wg_ref[...], preferred_element_type=F32)
    up = jnp.dot(h, wu_ref[...], preferred_element_type=F32)
    o_ref[...] = (gate * jax.nn.sigmoid(gate) * up).astype(o_ref.dtype)


def _ffn_up(x, g, wg, wu, *, tm, tn):
    m, d = x.shape
    n = wg.shape[1]
    return pl.pallas_call(
        _ffn_up_kernel,
        grid=(m // tm, n // tn),
        in_specs=[pl.BlockSpec((tm, d), lambda i, j: (i, 0)),
                  pl.BlockSpec((1, d), lambda i, j: (0, 0)),
                  pl.BlockSpec((d, tn), lambda i, j: (0, j)),
                  pl.BlockSpec((d, tn), lambda i, j: (0, j))],
        out_specs=pl.BlockSpec((tm, tn), lambda i, j: (i, j)),
        out_shape=jax.ShapeDtypeStruct((m, n), BF16),
        scratch_shapes=[pltpu.VMEM((tm, d), BF16)],
        compiler_params=_cparams(("parallel", "arbitrary")),
        name="ffn_up",
    )(x, g, wg, wu)


def _ffn_down_kernel(a_ref, w_ref, g_ref, x_ref, o_ref, acc_ref):
    k = pl.program_id(1)

    @pl.when(k == 0)
    def _():
        acc_ref[...] = jnp.zeros_like(acc_ref)

    acc_ref[...] += jnp.dot(a_ref[...], w_ref[...], preferred_element_type=F32)

    @pl.when(k == pl.num_programs(1) - 1)
    def _():
        _norm_residual_rows(acc_ref, g_ref, x_ref, o_ref)


def _ffn_down(a, w, g, x, *, tm, tk):
    m, kdim = a.shape
    d = w.shape[1]
    return pl.pallas_call(
        _ffn_down_kernel,
        grid=(m // tm, kdim // tk),
        in_specs=[pl.BlockSpec((tm, tk), lambda i, k: (i, k)),
                  pl.BlockSpec((tk, d), lambda i, k: (k, 0)),
                  pl.BlockSpec((1, d), lambda i, k: (0, 0)),
                  pl.BlockSpec((tm, d), lambda i, k: (i, 0))],
        out_specs=pl.BlockSpec((tm, d), lambda i, k: (i, 0)),
        out_shape=jax.ShapeDtypeStruct((m, d), F32),
        scratch_shapes=[pltpu.VMEM((tm, d), F32)],
        compiler_params=_cparams(("parallel", "arbitrary")),
        name="ffn_down",
    )(a, w, g, x)


def _out_proj_kernel(a_ref, b_ref, c_ref, d_ref, w_ref, g_ref, x_ref, o_ref, lhs_ref, acc_ref):
    for i, part in enumerate((a_ref, b_ref, c_ref, d_ref)):
        lhs_ref[:, i * GROUP_WIDTH:(i + 1) * GROUP_WIDTH] = part[...]
    acc_ref[...] = jnp.dot(lhs_ref[...], w_ref[...], preferred_element_type=F32)
    _norm_residual_rows(acc_ref, g_ref, x_ref, o_ref)


def _out_proj(parts, w, g, x, *, tm):
    m, d = x.shape
    part_spec = pl.BlockSpec((tm, GROUP_WIDTH), lambda i: (i, 0))
    return pl.pallas_call(
        _out_proj_kernel,
        grid=(m // tm,),
        in_specs=[part_spec, part_spec, part_spec, part_spec,
                  pl.BlockSpec((4 * GROUP_WIDTH, d), lambda i: (0, 0)),
                  pl.BlockSpec((1, d), lambda i: (0, 0)),
                  pl.BlockSpec((tm, d), lambda i: (i, 0))],
        out_specs=pl.BlockSpec((tm, d), lambda i: (i, 0)),
        out_shape=jax.ShapeDtypeStruct((m, d), F32),
        scratch_shapes=[pltpu.VMEM((tm, 4 * GROUP_WIDTH), BF16),
                        pltpu.VMEM((tm, d), F32)],
        compiler_params=_cparams(("parallel",)),
        name="out_proj",
    )(*parts, w, g, x)


def _rope_tables(pos_lo, pos_hi):
    half = DIFF_QK_DIM // 2
    inv = ROPE_THETA ** (-np.arange(half, dtype=np.float64) / half)
    lane = np.arange(LANES)
    pos = np.where(lane[None, :] < DIFF_QK_DIM, pos_lo[:, None], pos_hi[:, None]).astype(np.float64)
    ang = pos * inv[lane % half][None, :]
    sign = np.where((lane % DIFF_QK_DIM) < half, -1.0, 1.0)
    return (jnp.asarray(np.cos(ang), dtype=F32),
            jnp.asarray(np.sin(ang) * sign[None, :], dtype=F32))


def _rope(x, cos, sin_signed):
    lane = lax.broadcasted_iota(jnp.int32, x.shape, 1)
    first_half = (lane & (DIFF_QK_DIM // 2)) == 0
    partner = jnp.where(first_half, pltpu.roll(x, LANES - DIFF_QK_DIM // 2, 1),
                        pltpu.roll(x, DIFF_QK_DIM // 2, 1))
    return x * cos + partner * sin_signed


def _rms(x):
    return x * lax.rsqrt(jnp.mean(x * x, axis=-1, keepdims=True) + EPS)


LOG2E = math.log2(math.e)


def _attend(qs_ref, krot_ref, vt_ref, acc_ref, *, tk):
    seq = krot_ref.shape[0]
    n_tiles = seq // tk

    def scores(j):
        k = krot_ref[j * tk:(j + 1) * tk, :]
        return lax.dot_general(k, qs_ref[...], (((1,), (1,)), ((), ())), preferred_element_type=F32)

    m = l = None
    s_next = scores(0)
    for j in range(n_tiles):
        s = s_next
        if j + 1 < n_tiles:
            s_next = scores(j + 1)
        m_cur = jnp.max(s, axis=0, keepdims=True)
        m_next = m_cur if m is None else jnp.maximum(m, m_cur)
        p = jnp.exp2(s - m_next)
        pv = jnp.dot(vt_ref[:, j * tk:(j + 1) * tk], p.astype(BF16), preferred_element_type=F32)
        if m is None:
            l = jnp.sum(p, axis=0, keepdims=True)
            acc_ref[...] = pv
        else:
            alpha = jnp.exp2(m - m_next)
            l = alpha * l + jnp.sum(p, axis=0, keepdims=True)
            acc_ref[...] = alpha * acc_ref[...] + pv
        m = m_next
    return l


def _prepare_keys_values(k_ref, v_ref, cos_ref, sin_ref, krot_ref, vt_ref, gain):
    seq = k_ref.shape[0]

    def body(c, _):
        r0 = pl.multiple_of(c * ROW_CHUNK, ROW_CHUNK)
        k = k_ref[pl.ds(r0, ROW_CHUNK), :].astype(F32)
        if gain is not None:
            k = _rms(k) * gain
        krot_ref[pl.ds(r0, ROW_CHUNK), :] = _rope(
            k, cos_ref[pl.ds(r0, ROW_CHUNK), :], sin_ref[pl.ds(r0, ROW_CHUNK), :]).astype(BF16)
        vt_ref[:, pl.ds(r0, ROW_CHUNK)] = v_ref[pl.ds(r0, ROW_CHUNK), :].astype(F32).T.astype(BF16)
        return 0

    lax.fori_loop(0, seq // ROW_CHUNK, body, 0)


def _diff_attn_kernel(q_ref, k_ref, v_ref, cq_ref, sq_ref, ck_ref, sk_ref, lam_ref, g_ref, o_ref,
                      krot_ref, vt_ref, qs_ref, acc_ref, *, tq, tk, lam_init):
    @pl.when(pl.program_id(1) == 0)
    def _():
        _prepare_keys_values(k_ref, v_ref, ck_ref, sk_ref, krot_ref, vt_ref, None)

    q = _rope(q_ref[...].astype(F32), cq_ref[...], sq_ref[...]) * (DIFF_QK_DIM ** -0.5 * LOG2E)
    lane = lax.broadcasted_iota(jnp.int32, q.shape, 1)
    qs_ref[0:tq, :] = jnp.where(lane < DIFF_QK_DIM, q, 0.0).astype(BF16)
    qs_ref[tq:2 * tq, :] = jnp.where(lane >= DIFF_QK_DIM, q, 0.0).astype(BF16)

    l = _attend(qs_ref, krot_ref, vt_ref, acc_ref, tk=tk)

    lp = lam_ref[...]
    lam = (jnp.exp(jnp.sum(lp[0:1] * lp[1:2], keepdims=True))
           - jnp.exp(jnp.sum(lp[2:3] * lp[3:4], keepdims=True)) + lam_init)
    ot = (acc_ref[:, 0:tq] / l[:, 0:tq] - lam * (acc_ref[:, tq:2 * tq] / l[:, tq:2 * tq]))
    ms = jnp.mean(ot * ot, axis=0, keepdims=True)
    ot = ot * lax.rsqrt(ms + EPS) * (g_ref[...] * (1.0 - lam_init))
    o_ref[...] = ot.T.astype(o_ref.dtype)


def _gqa_attn_kernel(q_ref, k_ref, v_ref, cq_ref, sq_ref, ck_ref, sk_ref, qn_ref, kn_ref, o_ref,
                     krot_ref, vt_ref, qs_ref, acc_ref, *, tq, tk):
    @pl.when(pl.program_id(1) == 0)
    def _():
        _prepare_keys_values(k_ref, v_ref, ck_ref, sk_ref, krot_ref, vt_ref, kn_ref[...])

    for r in range(2):
        q = q_ref[:, r * HEAD_DIM:(r + 1) * HEAD_DIM].astype(F32)
        q = _rope(_rms(q) * qn_ref[...], cq_ref[...], sq_ref[...]) * (HEAD_DIM ** -0.5 * LOG2E)
        qs_ref[r * tq:(r + 1) * tq, :] = q.astype(BF16)

    l = _attend(qs_ref, krot_ref, vt_ref, acc_ref, tk=tk)

    for r in range(2):
        ot = acc_ref[:, r * tq:(r + 1) * tq] / l[:, r * tq:(r + 1) * tq]
        o_ref[:, r * HEAD_DIM:(r + 1) * HEAD_DIM] = ot.T.astype(o_ref.dtype)


def _flash_call(kernel, proj3, tables, params, *, n_groups, q_width, q_col, k_col, v_col,
                out_width, tq, name):
    batch, seq, _ = proj3.shape
    cos_t, sin_t = tables
    qb, kb, vb = q_col // q_width, k_col // HEAD_DIM, v_col // HEAD_DIM

    def grp(i):
        return i // n_groups, i % n_groups

    in_specs = [
        pl.BlockSpec((None, tq, q_width), lambda i, j: (grp(i)[0], j, qb + grp(i)[1])),
        pl.BlockSpec((None, seq, HEAD_DIM), lambda i, j: (grp(i)[0], 0, kb + grp(i)[1])),
        pl.BlockSpec((None, seq, HEAD_DIM), lambda i, j: (grp(i)[0], 0, vb + grp(i)[1])),
        pl.BlockSpec((tq, LANES), lambda i, j: (j, 0)),
        pl.BlockSpec((tq, LANES), lambda i, j: (j, 0)),
        pl.BlockSpec((seq, LANES), lambda i, j: (0, 0)),
        pl.BlockSpec((seq, LANES), lambda i, j: (0, 0)),
    ] + [pl.BlockSpec(p.shape, lambda i, j: (0, 0)) for p in params]
    return pl.pallas_call(
        kernel,
        grid=(batch * n_groups, seq // tq),
        in_specs=in_specs,
        out_specs=pl.BlockSpec((None, tq, out_width), lambda i, j: (grp(i)[0], j, grp(i)[1])),
        out_shape=jax.ShapeDtypeStruct((batch, seq, GROUP_WIDTH), BF16),
        scratch_shapes=[pltpu.VMEM((seq, HEAD_DIM), BF16),
                        pltpu.VMEM((HEAD_DIM, seq), BF16),
                        pltpu.VMEM((2 * tq, HEAD_DIM), BF16),
                        pltpu.VMEM((HEAD_DIM, 2 * tq), F32)],
        compiler_params=_cparams(("parallel", "arbitrary")),
        name=name,
    )(proj3, proj3, proj3, cos_t, sin_t, cos_t, sin_t, *params)


def _conv_kernel(a_ref, g_ref, dw_ref, dwb_ref, lng_ref, lnb_ref, pw_ref, pwb_ref, o_ref, u_ref, *, tm):
    seq = a_ref.shape[0]
    t = pl.program_id(1)

    @pl.when(t == 0)
    def _():
        zeros = jnp.zeros((CONV_HALO, u_ref.shape[1]), F32)
        u_ref[0:CONV_HALO, :] = zeros
        u_ref[CONV_HALO + seq:2 * CONV_HALO + seq, :] = zeros

        def body(c, _):
            r0 = pl.multiple_of(c * ROW_CHUNK, ROW_CHUNK)
            a = a_ref[pl.ds(r0, ROW_CHUNK), :].astype(F32)
            g = g_ref[pl.ds(r0, ROW_CHUNK), :].astype(F32)
            u_ref[pl.ds(CONV_HALO + r0, ROW_CHUNK), :] = a * jax.nn.sigmoid(g)
            return 0

        lax.fori_loop(0, seq // ROW_CHUNK, body, 0)

    pad = CONV_WIDTH // 2
    t0 = pl.multiple_of(t * tm, tm)
    win = u_ref[pl.ds(t0, tm + 2 * CONV_HALO), :]
    acc = jnp.zeros((tm, u_ref.shape[1]), F32) + dwb_ref[...]
    for k in range(CONV_WIDTH):
        s = CONV_HALO - pad + k
        acc = acc + win[s:s + tm, :] * dw_ref[k:k + 1, :]
    mu = jnp.mean(acc, axis=-1, keepdims=True)
    cen = acc - mu
    var = jnp.mean(cen * cen, axis=-1, keepdims=True)
    y = cen * lax.rsqrt(var + EPS) * lng_ref[...] + lnb_ref[...]
    y = y * jax.nn.sigmoid(y)
    o_ref[...] = (jnp.dot(y.astype(BF16), pw_ref[...], preferred_element_type=F32)
                  + pwb_ref[...]).astype(o_ref.dtype)


def _conv_mixer(proj3, dw, dwb, lng, lnb, pw, pwb, *, tm):
    batch, seq, _ = proj3.shape
    ch = GROUP_WIDTH
    vec = pl.BlockSpec((1, ch), lambda b, t: (0, 0))
    return pl.pallas_call(
        functools.partial(_conv_kernel, tm=tm),
        grid=(batch, seq // tm),
        in_specs=[pl.BlockSpec((None, seq, ch), lambda b, t: (b, 0, COL_B_A // ch)),
                  pl.BlockSpec((None, seq, ch), lambda b, t: (b, 0, COL_B_G // ch)),
                  pl.BlockSpec((CONV_WIDTH, ch), lambda b, t: (0, 0)),
                  vec, vec, vec,
                  pl.BlockSpec((ch, ch), lambda b, t: (0, 0)),
                  vec],
        out_specs=pl.BlockSpec((None, tm, ch), lambda b, t: (b, t, 0)),
        out_shape=jax.ShapeDtypeStruct((batch, seq, ch), BF16),
        scratch_shapes=[pltpu.VMEM((seq + 2 * CONV_HALO, ch), F32)],
        compiler_params=_cparams(("parallel", "arbitrary")),
        name="conv_mixer",
    )(proj3, proj3, dw, dwb, lng, lnb, pw, pwb)


def _na_geometry(seq):
    rows = seq // GRID_W
    qr_blk = Q_BLOCK // GRID_W
    n_blk = seq // Q_BLOCK
    reps = (0, 1, 2, n_blk - 2, n_blk - 1)
    mask_add = np.zeros((NA_CLASSES, Q_BLOCK, NA_BAND), np.float32)
    q_local = np.arange(Q_BLOCK)
    p = np.arange(NA_BAND)
    for c, blk in enumerate(reps):
        bs = int(np.clip(blk * qr_blk - NA_KH // 2, 0, rows - NA_BAND_ROWS))
        q_row = (blk * qr_blk + q_local // GRID_W)[:, None]
        q_col = (q_local % GRID_W)[:, None]
        k_row = (bs + p // GRID_W)[None, :]
        k_col = (p % GRID_W)[None, :]
        win_r = np.clip(q_row - NA_KH // 2, 0, rows - NA_KH)
        win_c = np.clip(q_col - NA_KW // 2, 0, GRID_W - NA_KW)
        ok = (k_row >= win_r) & (k_row < win_r + NA_KH) & (k_col >= win_c) & (k_col < win_c + NA_KW)
        mask_add[c] = np.where(ok, 0.0, -1e30)
    return jnp.asarray(mask_add)


def _rpb_tiles_kernel(rpb_ref, o_ref):
    h = pl.program_id(0)
    n_c = 2 * NA_KW - 1
    n_r = 2 * NA_KH - 1
    shape = (GRID_W, LANES)
    lane = lax.broadcasted_iota(jnp.int32, shape, 1)
    q_col = lax.broadcasted_iota(jnp.int32, shape, 0)
    ic = jnp.clip((lane & (GRID_W - 1)) - q_col + (NA_KW - 1), 0, n_c - 1)
    left = lane < GRID_W
    base = h * (n_r * n_c)
    for i in range(NA_T2):
        i_l = min(max(i - 1, 0), n_r - 1)
        i_r = min(max(i, 0), n_r - 1)

        def body(j, acc, i_l=i_l, i_r=i_r):
            coef = jnp.where(left, rpb_ref[base + i_l * n_c + j], rpb_ref[base + i_r * n_c + j])
            return acc + jnp.where(ic == j, coef, 0.0)

        o_ref[i] = lax.fori_loop(0, n_c, body, jnp.zeros(shape, F32))


def _rpb_tiles(rpb):
    heads = rpb.shape[0]
    return pl.pallas_call(
        _rpb_tiles_kernel,
        grid=(heads,),
        in_specs=[pl.BlockSpec(memory_space=pltpu.SMEM)],
        out_specs=pl.BlockSpec((None, NA_T2, GRID_W, LANES), lambda h: (h, 0, 0, 0)),
        out_shape=jax.ShapeDtypeStruct((heads, NA_T2, GRID_W, LANES), F32),
        compiler_params=_cparams(("parallel",)),
        name="na_rpb_tiles",
    )(rpb.reshape(-1))


def _na_kernel(q_ref, k_ref, v_ref, t2_ref, mask_ref, o_ref):
    seq = q_ref.shape[0]
    rows = seq // GRID_W
    qr_blk = Q_BLOCK // GRID_W
    n_blk = seq // Q_BLOCK

    def body(blk, _):
        bs = jnp.clip(blk * qr_blk - NA_KH // 2, 0, rows - NA_BAND_ROWS)
        cls = jnp.where(blk < 2, blk, jnp.where(blk >= n_blk - 2, blk - (n_blk - NA_CLASSES), 2))
        q0 = pl.multiple_of(blk * Q_BLOCK, Q_BLOCK)
        k0 = pl.multiple_of(bs * GRID_W, GRID_W)
        q = q_ref[pl.ds(q0, Q_BLOCK), :]
        k = k_ref[pl.ds(k0, NA_BAND), :]
        v = v_ref[pl.ds(k0, NA_BAND), :]
        s = lax.dot_general(q, k, (((1,), (1,)), ((), ())), preferred_element_type=F32)
        bias_rows = []
        for a in range(qr_blk):
            tiles = []
            for r2 in range(NA_BAND_ROWS // 2):
                i1 = bs + 2 * r2 - (blk * qr_blk + a) + (NA_KH - 1)
                tiles.append(t2_ref[jnp.clip(i1, -1, NA_T2 - 2) + 1])
            bias_rows.append(jnp.concatenate(tiles, axis=1))
        s = s * (HEAD_DIM ** -0.5) + jnp.concatenate(bias_rows, axis=0) + mask_ref[cls]
        m = jnp.max(s, axis=1, keepdims=True)
        p = jnp.exp(s - m)
        l = jnp.sum(p, axis=1, keepdims=True)
        o = jnp.dot(p.astype(BF16), v, preferred_element_type=F32) / l
        o_ref[pl.ds(q0, Q_BLOCK), :] = o.astype(o_ref.dtype)
        return 0

    lax.fori_loop(0, n_blk, body, 0)


def _na_mixer(proj3, t2, mask_add):
    batch, seq, _ = proj3.shape
    heads = GROUP_HEADS

    def head_spec(col):
        return pl.BlockSpec((None, seq, HEAD_DIM), lambda b, h: (b, 0, col // HEAD_DIM + h))

    return pl.pallas_call(
        _na_kernel,
        grid=(batch, heads),
        in_specs=[head_spec(COL_D_Q), head_spec(COL_D_K), head_spec(COL_D_V),
                  pl.BlockSpec((None, NA_T2, GRID_W, LANES), lambda b, h: (h, 0, 0, 0)),
                  pl.BlockSpec(mask_add.shape, lambda b, h: (0, 0, 0))],
        out_specs=pl.BlockSpec((None, seq, HEAD_DIM), lambda b, h: (b, 0, h)),
        out_shape=jax.ShapeDtypeStruct((batch, seq, GROUP_WIDTH), BF16),
        compiler_params=_cparams(("parallel", "parallel")),
        name="na_mixer",
    )(proj3, proj3, proj3, t2, mask_add)


def kernel(x, norm_mix_pre, norm_mix_post, norm_ffn_pre, norm_ffn_post, w_in, w_out, diff_lambda, diff_subln, conv_dw, conv_dw_b, conv_ln_g, conv_ln_b, conv_pw, conv_pw_b, gqa_q_norm, gqa_k_norm, na_rpb, ffn_gate, ffn_up, ffn_down):
    batch, seq, d = x.shape
    depth = w_in.shape[0]
    tokens = batch * seq

    t = np.arange(seq)
    diff_tables = _rope_tables(t, t)
    axial_tables = _rope_tables(t // GRID_W, t % GRID_W)
    mask_add = _na_geometry(seq)

    def row(v):
        return v.reshape(1, -1)

    xf = x.reshape(tokens, d)
    for l in range(depth):
        lam_init = 0.8 - 0.6 * math.exp(-0.3 * l)
        proj = _in_proj(xf, row(norm_mix_pre[l]), w_in[l].astype(BF16), tm=1024, tn=1024)
        proj3 = proj.reshape(batch, seq, IN_COLS)

        out_a = _flash_call(
            functools.partial(_diff_attn_kernel, tq=256, tk=512, lam_init=lam_init),
            proj3, diff_tables, (diff_lambda[l], diff_subln[l].reshape(-1, 1)),
            n_groups=GROUP_HEADS, q_width=HEAD_DIM, q_col=COL_A_Q, k_col=COL_A_K, v_col=COL_A_V,
            out_width=HEAD_DIM, tq=256, name="diff_attn")
        out_b = _conv_mixer(proj3, conv_dw[l], row(conv_dw_b[l]), row(conv_ln_g[l]), row(conv_ln_b[l]),
                            conv_pw[l].astype(BF16), row(conv_pw_b[l]), tm=256)
        out_c = _flash_call(
            functools.partial(_gqa_attn_kernel, tq=256, tk=512),
            proj3, axial_tables, (row(gqa_q_norm[l]), row(gqa_k_norm[l])),
            n_groups=GQA_KV_HEADS, q_width=2 * HEAD_DIM, q_col=COL_C_Q, k_col=COL_C_K, v_col=COL_C_V,
            out_width=2 * HEAD_DIM, tq=256, name="gqa_attn")
        out_d = _na_mixer(proj3, _rpb_tiles(na_rpb[l]), mask_add)

        parts = [o.reshape(tokens, GROUP_WIDTH) for o in (out_a, out_b, out_c, out_d)]
        xf = _out_proj(parts, w_out[l].astype(BF16), row(norm_mix_post[l]), xf, tm=512)

        act = _ffn_up(xf, row(norm_ffn_pre[l]), ffn_gate[l].astype(BF16), ffn_up[l].astype(BF16),
                      tm=512, tn=1408)
        xf = _ffn_down(act, ffn_down[l].astype(BF16), row(norm_ffn_post[l]), xf, tm=1024, tk=512)
    return xf.reshape(batch, seq, d)
```

```python
import functools
import math

import numpy as np
import jax
import jax.numpy as jnp
from jax import lax
from jax.experimental import pallas as pl
from jax.experimental.pallas import tpu as pltpu

D_MODEL = 2048
HEAD_DIM = 128
GROUP_HEADS = 4
GROUP_WIDTH = GROUP_HEADS * HEAD_DIM
DIFF_QK_DIM = HEAD_DIM // 2
CONV_WIDTH = 31
GQA_KV_HEADS = 2
NA_KH = 8
NA_KW = 16
GRID_W = 64
Q_BLOCK = 128
ROPE_THETA = 10000.0
EPS = 1e-6

COL_A_Q, COL_A_K, COL_A_V = 0, 512, 1024
COL_B_A, COL_B_G = 1536, 2048
COL_C_Q, COL_C_K, COL_C_V = 2560, 3072, 3328
COL_D_Q, COL_D_K, COL_D_V = 3584, 4096, 4608
IN_COLS = 5120

LANES = 128
VMEM_LIMIT = 56 * 1024 * 1024
ROW_CHUNK = 128
ATTN_TQ = 512
ATTN_TK = 512

NA_BAND_ROWS = 10
NA_BAND = NA_BAND_ROWS * GRID_W
NA_CLASSES = 5
NA_T2 = 16
CONV_HALO = 16

F32 = jnp.float32
BF16 = jnp.bfloat16


def _cparams(semantics):
    return pltpu.CompilerParams(dimension_semantics=semantics,
                                vmem_limit_bytes=VMEM_LIMIT)


def _rms(x):
    return x * lax.rsqrt(jnp.mean(x * x, axis=-1, keepdims=True) + EPS)


def _rmsnorm_kernel(x_ref, g_ref, h_ref):
    def body(c, _):
        r0 = pl.multiple_of(c * ROW_CHUNK, ROW_CHUNK)
        h_ref[pl.ds(r0, ROW_CHUNK), :] = (_rms(x_ref[pl.ds(r0, ROW_CHUNK), :]) * g_ref[...]).astype(h_ref.dtype)
        return 0

    lax.fori_loop(0, x_ref.shape[0] // ROW_CHUNK, body, 0)


def _rmsnorm(x, g, *, tm):
    m, d = x.shape
    return pl.pallas_call(
        _rmsnorm_kernel,
        grid=(m // tm,),
        in_specs=[pl.BlockSpec((tm, d), lambda i: (i, 0)),
                  pl.BlockSpec((1, d), lambda i: (0, 0))],
        out_specs=pl.BlockSpec((tm, d), lambda i: (i, 0)),
        out_shape=jax.ShapeDtypeStruct((m, d), BF16),
        compiler_params=_cparams(("parallel",)),
        name="rmsnorm",
    )(x, g)


def _norm_residual_rows(acc_ref, g_ref, x_ref, o_ref, gn_ref, hn_ref):
    def body(c, _):
        r0 = pl.multiple_of(c * ROW_CHUNK, ROW_CHUNK)
        o = x_ref[pl.ds(r0, ROW_CHUNK), :] + _rms(acc_ref[pl.ds(r0, ROW_CHUNK), :]) * g_ref[...]
        o_ref[pl.ds(r0, ROW_CHUNK), :] = o
        if hn_ref is not None:
            hn_ref[pl.ds(r0, ROW_CHUNK), :] = (_rms(o) * gn_ref[...]).astype(hn_ref.dtype)
        return 0

    lax.fori_loop(0, acc_ref.shape[0] // ROW_CHUNK, body, 0)


def _cast_weight_tile(w_ref, wb_ref):
    def body(c, _):
        r0 = pl.multiple_of(c * ROW_CHUNK, ROW_CHUNK)
        wb_ref[pl.ds(r0, ROW_CHUNK), :] = w_ref[pl.ds(r0, ROW_CHUNK), :].astype(wb_ref.dtype)
        return 0

    lax.fori_loop(0, w_ref.shape[0] // ROW_CHUNK, body, 0)


def _in_proj_kernel(h_ref, w_ref, o_ref, wb_ref):
    @pl.when(pl.program_id(1) == 0)
    def _():
        _cast_weight_tile(w_ref, wb_ref)

    o_ref[...] = jnp.dot(h_ref[...], wb_ref[...], preferred_element_type=F32).astype(o_ref.dtype)


def _in_proj(h, w, *, tm, tn):
    m, d = h.shape
    n = w.shape[1]
    return pl.pallas_call(
        _in_proj_kernel,
        grid=(n // tn, m // tm),
        in_specs=[pl.BlockSpec((tm, d), lambda j, i: (i, 0)),
                  pl.BlockSpec((d, tn), lambda j, i: (0, j))],
        out_specs=pl.BlockSpec((tm, tn), lambda j, i: (i, j)),
        out_shape=jax.ShapeDtypeStruct((m, n), BF16),
        scratch_shapes=[pltpu.VMEM((d, tn), BF16)],
        compiler_params=_cparams(("parallel", "arbitrary")),
        name="in_proj",
    )(h, w)


def _ffn_up_kernel(h_ref, wg_ref, wu_ref, o_ref, wgb_ref, wub_ref):
    @pl.when(pl.program_id(1) == 0)
    def _():
        _cast_weight_tile(wg_ref, wgb_ref)
        _cast_weight_tile(wu_ref, wub_ref)

    h = h_ref[...]
    gate = jnp.dot(h, wgb_ref[...], preferred_element_type=F32)
    up = jnp.dot(h, wub_ref[...], preferred_element_type=F32)
    o_ref[...] = (gate * jax.nn.sigmoid(gate) * up).astype(o_ref.dtype)


def _ffn_up(h, wg, wu, *, tm, tn):
    m, d = h.shape
    n = wg.shape[1]
    w_spec = pl.BlockSpec((d, tn), lambda j, i: (0, j))
    return pl.pallas_call(
        _ffn_up_kernel,
        grid=(n // tn, m // tm),
        in_specs=[pl.BlockSpec((tm, d), lambda j, i: (i, 0)), w_spec, w_spec],
        out_specs=pl.BlockSpec((tm, tn), lambda j, i: (i, j)),
        out_shape=jax.ShapeDtypeStruct((m, n), BF16),
        scratch_shapes=[pltpu.VMEM((d, tn), BF16), pltpu.VMEM((d, tn), BF16)],
        compiler_params=_cparams(("parallel", "arbitrary")),
        name="ffn_up",
    )(h, wg, wu)


def _ffn_down_kernel(*refs, emit_next):
    if emit_next:
        a_ref, w_ref, g_ref, x_ref, gn_ref, o_ref, hn_ref, acc_ref = refs
    else:
        a_ref, w_ref, g_ref, x_ref, o_ref, acc_ref = refs
        gn_ref = hn_ref = None
    k = pl.program_id(1)

    @pl.when(k == 0)
    def _():
        acc_ref[...] = jnp.zeros_like(acc_ref)

    acc_ref[...] += jnp.dot(a_ref[...], w_ref[...], preferred_element_type=F32)

    @pl.when(k == pl.num_programs(1) - 1)
    def _():
        _norm_residual_rows(acc_ref, g_ref, x_ref, o_ref, gn_ref, hn_ref)


def _ffn_down(a, w, g, x, gn, *, tm, tk):
    m, kdim = a.shape
    d = w.shape[1]
    emit_next = gn is not None
    vec = pl.BlockSpec((1, d), lambda i, k: (0, 0))
    row_tile = pl.BlockSpec((tm, d), lambda i, k: (i, 0))
    outs = pl.pallas_call(
        functools.partial(_ffn_down_kernel, emit_next=emit_next),
        grid=(m // tm, kdim // tk),
        in_specs=[pl.BlockSpec((tm, tk), lambda i, k: (i, k)),
                  pl.BlockSpec((tk, d), lambda i, k: (k, 0)),
                  vec, row_tile] + [vec] * emit_next,
        out_specs=[row_tile] + [row_tile] * emit_next,
        out_shape=[jax.ShapeDtypeStruct((m, d), F32)] + [jax.ShapeDtypeStruct((m, d), BF16)] * emit_next,
        scratch_shapes=[pltpu.VMEM((tm, d), F32)],
        compiler_params=_cparams(("parallel", "arbitrary")),
        name="ffn_down",
    )(a, w, g, x, *([gn] * emit_next))
    return (outs[0], outs[1]) if emit_next else (outs[0], None)


def _out_proj_kernel(a_ref, b_ref, c_ref, d_ref, w_ref, g_ref, x_ref, gn_ref, o_ref, hn_ref,
                     lhs_ref, acc_ref):
    for i, part in enumerate((a_ref, b_ref, c_ref, d_ref)):
        lhs_ref[:, i * GROUP_WIDTH:(i + 1) * GROUP_WIDTH] = part[...]
    acc_ref[...] = jnp.dot(lhs_ref[...], w_ref[...], preferred_element_type=F32)
    _norm_residual_rows(acc_ref, g_ref, x_ref, o_ref, gn_ref, hn_ref)


def _out_proj(parts, w, g, x, gn, *, tm):
    m, d = x.shape
    part_spec = pl.BlockSpec((tm, GROUP_WIDTH), lambda i: (i, 0))
    vec = pl.BlockSpec((1, d), lambda i: (0, 0))
    row_tile = pl.BlockSpec((tm, d), lambda i: (i, 0))
    return pl.pallas_call(
        _out_proj_kernel,
        grid=(m // tm,),
        in_specs=[part_spec, part_spec, part_spec, part_spec,
                  pl.BlockSpec((4 * GROUP_WIDTH, d), lambda i: (0, 0)),
                  vec, row_tile, vec],
        out_specs=[row_tile, row_tile],
        out_shape=[jax.ShapeDtypeStruct((m, d), F32), jax.ShapeDtypeStruct((m, d), BF16)],
        scratch_shapes=[pltpu.VMEM((tm, 4 * GROUP_WIDTH), BF16),
                        pltpu.VMEM((tm, d), F32)],
        compiler_params=_cparams(("parallel",)),
        name="out_proj",
    )(*parts, w, g, x, gn)


def _rope_tables(pos_lo, pos_hi):
    half = DIFF_QK_DIM // 2
    inv = ROPE_THETA ** (-np.arange(half, dtype=np.float64) / half)
    lane = np.arange(LANES)
    pos = np.where(lane[None, :] < DIFF_QK_DIM, pos_lo[:, None], pos_hi[:, None]).astype(np.float64)
    ang = pos * inv[lane % half][None, :]
    sign = np.where((lane % DIFF_QK_DIM) < half, -1.0, 1.0)
    return (jnp.asarray(np.cos(ang), dtype=F32),
            jnp.asarray(np.sin(ang) * sign[None, :], dtype=F32))


def _rope(x, cos, sin_signed):
    lane = lax.broadcasted_iota(jnp.int32, x.shape, 1)
    first_half = (lane & (DIFF_QK_DIM // 2)) == 0
    partner = jnp.where(first_half, pltpu.roll(x, LANES - DIFF_QK_DIM // 2, 1),
                        pltpu.roll(x, DIFF_QK_DIM // 2, 1))
    return x * cos + partner * sin_signed


LOG2E = math.log2(math.e)


def _attend(qs_ref, krot_ref, vt_ref, acc_ref, *, tk):
    seq = krot_ref.shape[0]
    n_tiles = seq // tk

    def scores(j):
        k = krot_ref[j * tk:(j + 1) * tk, :]
        return lax.dot_general(k, qs_ref[...], (((1,), (1,)), ((), ())), preferred_element_type=F32)

    m = l = None
    s_next = scores(0)
    for j in range(n_tiles):
        s = s_next
        if j + 1 < n_tiles:
            s_next = scores(j + 1)
        m_cur = jnp.max(s, axis=0, keepdims=True)
        m_next = m_cur if m is None else jnp.maximum(m, m_cur)
        p = jnp.exp2(s - m_next)
        pv = jnp.dot(vt_ref[:, j * tk:(j + 1) * tk], p.astype(BF16), preferred_element_type=F32)
        if m is None:
            l = jnp.sum(p, axis=0, keepdims=True)
            acc_ref[...] = pv
        else:
            alpha = jnp.exp2(m - m_next)
            l = alpha * l + jnp.sum(p, axis=0, keepdims=True)
            acc_ref[...] = alpha * acc_ref[...] + pv
        m = m_next
    return l


def _prepare_keys_values(k_ref, v_ref, cos_ref, sin_ref, krot_ref, vt_ref, gain):
    seq = k_ref.shape[0]

    def body(c, _):
        r0 = pl.multiple_of(c * ROW_CHUNK, ROW_CHUNK)
        k = k_ref[pl.ds(r0, ROW_CHUNK), :].astype(F32)
        if gain is not None:
            k = _rms(k) * gain
        krot_ref[pl.ds(r0, ROW_CHUNK), :] = _rope(
            k, cos_ref[pl.ds(r0, ROW_CHUNK), :], sin_ref[pl.ds(r0, ROW_CHUNK), :]).astype(BF16)
        vt_ref[:, pl.ds(r0, ROW_CHUNK)] = v_ref[pl.ds(r0, ROW_CHUNK), :].astype(F32).T.astype(BF16)
        return 0

    lax.fori_loop(0, seq // ROW_CHUNK, body, 0)


def _diff_attn_kernel(q_ref, k_ref, v_ref, cq_ref, sq_ref, ck_ref, sk_ref, lam_ref, g_ref, o_ref,
                      krot_ref, vt_ref, qs_ref, acc_ref, *, tq, tk, lam_init):
    @pl.when(pl.program_id(1) == 0)
    def _():
        _prepare_keys_values(k_ref, v_ref, ck_ref, sk_ref, krot_ref, vt_ref, None)

    q = _rope(q_ref[...].astype(F32), cq_ref[...], sq_ref[...]) * (DIFF_QK_DIM ** -0.5 * LOG2E)
    lane = lax.broadcasted_iota(jnp.int32, q.shape, 1)
    qs_ref[0:tq, :] = jnp.where(lane < DIFF_QK_DIM, q, 0.0).astype(BF16)
    qs_ref[tq:2 * tq, :] = jnp.where(lane >= DIFF_QK_DIM, q, 0.0).astype(BF16)

    l = _attend(qs_ref, krot_ref, vt_ref, acc_ref, tk=tk)

    lp = lam_ref[...]
    lam = (jnp.exp(jnp.sum(lp[0:1] * lp[1:2], keepdims=True))
           - jnp.exp(jnp.sum(lp[2:3] * lp[3:4], keepdims=True)) + lam_init)
    ot = (acc_ref[:, 0:tq] / l[:, 0:tq] - lam * (acc_ref[:, tq:2 * tq] / l[:, tq:2 * tq]))
    ms = jnp.mean(ot * ot, axis=0, keepdims=True)
    ot = ot * lax.rsqrt(ms + EPS) * (g_ref[...] * (1.0 - lam_init))
    o_ref[...] = ot.T.astype(o_ref.dtype)


def _gqa_attn_kernel(q_ref, k_ref, v_ref, cq_ref, sq_ref, ck_ref, sk_ref, qn_ref, kn_ref, o_ref,
                     krot_ref, vt_ref, qs_ref, acc_ref, *, tq, tk):
    @pl.when(pl.program_id(1) == 0)
    def _():
        _prepare_keys_values(k_ref, v_ref, ck_ref, sk_ref, krot_ref, vt_ref, kn_ref[...])

    for r in range(2):
        q = q_ref[:, r * HEAD_DIM:(r + 1) * HEAD_DIM].astype(F32)
        q = _rope(_rms(q) * qn_ref[...], cq_ref[...], sq_ref[...]) * (HEAD_DIM ** -0.5 * LOG2E)
        qs_ref[r * tq:(r + 1) * tq, :] = q.astype(BF16)

    l = _attend(qs_ref, krot_ref, vt_ref, acc_ref, tk=tk)

    for r in range(2):
        ot = acc_ref[:, r * tq:(r + 1) * tq] / l[:, r * tq:(r + 1) * tq]
        o_ref[:, r * HEAD_DIM:(r + 1) * HEAD_DIM] = ot.T.astype(o_ref.dtype)


def _flash_call(kernel, proj3, tables, params, *, n_groups, q_width, q_col, k_col, v_col,
                out_width, tq, name):
    batch, seq, _ = proj3.shape
    cos_t, sin_t = tables
    qb, kb, vb = q_col // q_width, k_col // HEAD_DIM, v_col // HEAD_DIM

    def grp(i):
        return i // n_groups, i % n_groups

    in_specs = [
        pl.BlockSpec((None, tq, q_width), lambda i, j: (grp(i)[0], j, qb + grp(i)[1])),
        pl.BlockSpec((None, seq, HEAD_DIM), lambda i, j: (grp(i)[0], 0, kb + grp(i)[1])),
        pl.BlockSpec((None, seq, HEAD_DIM), lambda i, j: (grp(i)[0], 0, vb + grp(i)[1])),
        pl.BlockSpec((tq, LANES), lambda i, j: (j, 0)),
        pl.BlockSpec((tq, LANES), lambda i, j: (j, 0)),
        pl.BlockSpec((seq, LANES), lambda i, j: (0, 0)),
        pl.BlockSpec((seq, LANES), lambda i, j: (0, 0)),
    ] + [pl.BlockSpec(p.shape, lambda i, j: (0, 0)) for p in params]
    return pl.pallas_call(
        kernel,
        grid=(batch * n_groups, seq // tq),
        in_specs=in_specs,
        out_specs=pl.BlockSpec((None, tq, out_width), lambda i, j: (grp(i)[0], j, grp(i)[1])),
        out_shape=jax.ShapeDtypeStruct((batch, seq, GROUP_WIDTH), BF16),
        scratch_shapes=[pltpu.VMEM((seq, HEAD_DIM), BF16),
                        pltpu.VMEM((HEAD_DIM, seq), BF16),
                        pltpu.VMEM((2 * tq, HEAD_DIM), BF16),
                        pltpu.VMEM((HEAD_DIM, 2 * tq), F32)],
        compiler_params=_cparams(("parallel", "arbitrary")),
        name=name,
    )(proj3, proj3, proj3, cos_t, sin_t, cos_t, sin_t, *params)


def _conv_kernel(a_ref, g_ref, dw_ref, dwb_ref, lng_ref, lnb_ref, pw_ref, pwb_ref, o_ref,
                 u_ref, acc_ref, y_ref, *, tm):
    seq = a_ref.shape[0]
    t = pl.program_id(1)

    @pl.when(t == 0)
    def _():
        zeros = jnp.zeros((CONV_HALO, u_ref.shape[1]), F32)
        u_ref[0:CONV_HALO, :] = zeros
        u_ref[CONV_HALO + seq:2 * CONV_HALO + seq, :] = zeros

        def body(c, _):
            r0 = pl.multiple_of(c * ROW_CHUNK, ROW_CHUNK)
            a = a_ref[pl.ds(r0, ROW_CHUNK), :].astype(F32)
            g = g_ref[pl.ds(r0, ROW_CHUNK), :].astype(F32)
            u_ref[pl.ds(CONV_HALO + r0, ROW_CHUNK), :] = a * jax.nn.sigmoid(g)
            return 0

        lax.fori_loop(0, seq // ROW_CHUNK, body, 0)

    pad = CONV_WIDTH // 2
    sub = 8
    n_ch = u_ref.shape[1]

    def row_block(rb, _):
        base = pl.multiple_of(t * tm + rb * ROW_CHUNK, ROW_CHUNK)
        for cg in range(n_ch // LANES):
            cols = slice(cg * LANES, (cg + 1) * LANES)
            win = u_ref[pl.ds(base, ROW_CHUNK + 2 * CONV_HALO), cols]
            acc = jnp.zeros((ROW_CHUNK, LANES), F32) + dwb_ref[:, cols]
            for r in range(sub):
                shifted = win if r == 0 else pltpu.roll(win, win.shape[0] - r, 0)
                for a in range(2 * CONV_HALO // sub):
                    k = sub * a + r - (CONV_HALO - pad)
                    if 0 <= k < CONV_WIDTH:
                        acc = acc + shifted[sub * a:sub * a + ROW_CHUNK, :] * dw_ref[k:k + 1, cols]
            acc_ref[:, cols] = acc
        acc = acc_ref[...]
        mu = jnp.mean(acc, axis=-1, keepdims=True)
        cen = acc - mu
        var = jnp.mean(cen * cen, axis=-1, keepdims=True)
        y = cen * lax.rsqrt(var + EPS) * lng_ref[...] + lnb_ref[...]
        y_ref[pl.ds(pl.multiple_of(rb * ROW_CHUNK, ROW_CHUNK), ROW_CHUNK), :] = (
            y * jax.nn.sigmoid(y)).astype(BF16)
        return 0

    lax.fori_loop(0, tm // ROW_CHUNK, row_block, 0)
    o_ref[...] = (jnp.dot(y_ref[...], pw_ref[...], preferred_element_type=F32)
                  + pwb_ref[...]).astype(o_ref.dtype)


def _conv_mixer(proj3, dw, dwb, lng, lnb, pw, pwb, *, tm):
    batch, seq, _ = proj3.shape
    ch = GROUP_WIDTH
    vec = pl.BlockSpec((1, ch), lambda b, t: (0, 0))
    return pl.pallas_call(
        functools.partial(_conv_kernel, tm=tm),
        grid=(batch, seq // tm),
        in_specs=[pl.BlockSpec((None, seq, ch), lambda b, t: (b, 0, COL_B_A // ch)),
                  pl.BlockSpec((None, seq, ch), lambda b, t: (b, 0, COL_B_G // ch)),
                  pl.BlockSpec((CONV_WIDTH, ch), lambda b, t: (0, 0)),
                  vec, vec, vec,
                  pl.BlockSpec((ch, ch), lambda b, t: (0, 0)),
                  vec],
        out_specs=pl.BlockSpec((None, tm, ch), lambda b, t: (b, t, 0)),
        out_shape=jax.ShapeDtypeStruct((batch, seq, ch), BF16),
        scratch_shapes=[pltpu.VMEM((seq + 2 * CONV_HALO, ch), F32),
                        pltpu.VMEM((ROW_CHUNK, ch), F32),
                        pltpu.VMEM((tm, ch), BF16)],
        compiler_params=_cparams(("parallel", "arbitrary")),
        name="conv_mixer",
    )(proj3, proj3, dw, dwb, lng, lnb, pw, pwb)


def _na_geometry(seq):
    rows = seq // GRID_W
    qr_blk = Q_BLOCK // GRID_W
    n_blk = seq // Q_BLOCK
    reps = (0, 1, 2, n_blk - 2, n_blk - 1)
    mask_add = np.zeros((NA_CLASSES, Q_BLOCK, NA_BAND), np.float32)
    q_local = np.arange(Q_BLOCK)
    p = np.arange(NA_BAND)
    for c, blk in enumerate(reps):
        bs = int(np.clip(blk * qr_blk - NA_KH // 2, 0, rows - NA_BAND_ROWS))
        q_row = (blk * qr_blk + q_local // GRID_W)[:, None]
        q_col = (q_local % GRID_W)[:, None]
        k_row = (bs + p // GRID_W)[None, :]
        k_col = (p % GRID_W)[None, :]
        win_r = np.clip(q_row - NA_KH // 2, 0, rows - NA_KH)
        win_c = np.clip(q_col - NA_KW // 2, 0, GRID_W - NA_KW)
        ok = (k_row >= win_r) & (k_row < win_r + NA_KH) & (k_col >= win_c) & (k_col < win_c + NA_KW)
        mask_add[c] = np.where(ok, 0.0, -1e30)
    return jnp.asarray(mask_add)


def _rpb_tiles_kernel(rpb_ref, o_ref):
    h = pl.program_id(0)
    n_c = 2 * NA_KW - 1
    n_r = 2 * NA_KH - 1
    shape = (GRID_W, LANES)
    lane = lax.broadcasted_iota(jnp.int32, shape, 1)
    q_col = lax.broadcasted_iota(jnp.int32, shape, 0)
    ic = jnp.clip((lane & (GRID_W - 1)) - q_col + (NA_KW - 1), 0, n_c - 1)
    left = lane < GRID_W
    base = h * (n_r * n_c)
    for i in range(NA_T2):
        i_l = min(max(i - 1, 0), n_r - 1)
        i_r = min(max(i, 0), n_r - 1)

        def body(j, acc, i_l=i_l, i_r=i_r):
            coef = jnp.where(left, rpb_ref[base + i_l * n_c + j], rpb_ref[base + i_r * n_c + j])
            return acc + jnp.where(ic == j, coef, 0.0)

        o_ref[i] = lax.fori_loop(0, n_c, body, jnp.zeros(shape, F32))


def _rpb_tiles(rpb):
    heads = rpb.shape[0]
    return pl.pallas_call(
        _rpb_tiles_kernel,
        grid=(heads,),
        in_specs=[pl.BlockSpec(memory_space=pltpu.SMEM)],
        out_specs=pl.BlockSpec((None, NA_T2, GRID_W, LANES), lambda h: (h, 0, 0, 0)),
        out_shape=jax.ShapeDtypeStruct((heads, NA_T2, GRID_W, LANES), F32),
        compiler_params=_cparams(("parallel",)),
        name="na_rpb_tiles",
    )(rpb.reshape(-1))


def _na_kernel(q_ref, k_ref, v_ref, t2_ref, mask_ref, o_ref):
    seq = q_ref.shape[0]
    rows = seq // GRID_W
    qr_blk = Q_BLOCK // GRID_W
    n_blk = seq // Q_BLOCK

    def body(blk, _):
        bs = jnp.clip(blk * qr_blk - NA_KH // 2, 0, rows - NA_BAND_ROWS)
        cls = jnp.where(blk < 2, blk, jnp.where(blk >= n_blk - 2, blk - (n_blk - NA_CLASSES), 2))
        q0 = pl.multiple_of(blk * Q_BLOCK, Q_BLOCK)
        k0 = pl.multiple_of(bs * GRID_W, GRID_W)
        q = q_ref[pl.ds(q0, Q_BLOCK), :]
        k = k_ref[pl.ds(k0, NA_BAND), :]
        v = v_ref[pl.ds(k0, NA_BAND), :]
        s = lax.dot_general(q, k, (((1,), (1,)), ((), ())), preferred_element_type=F32)
        bias_rows = []
        for a in range(qr_blk):
            tiles = []
            for r2 in range(NA_BAND_ROWS // 2):
                i1 = bs + 2 * r2 - (blk * qr_blk + a) + (NA_KH - 1)
                tiles.append(t2_ref[jnp.clip(i1, -1, NA_T2 - 2) + 1])
            bias_rows.append(jnp.concatenate(tiles, axis=1))
        s = s * (HEAD_DIM ** -0.5) + jnp.concatenate(bias_rows, axis=0) + mask_ref[cls]
        m = jnp.max(s, axis=1, keepdims=True)
        p = jnp.exp(s - m)
        l = jnp.sum(p, axis=1, keepdims=True)
        o = jnp.dot(p.astype(BF16), v, preferred_element_type=F32) / l
        o_ref[pl.ds(q0, Q_BLOCK), :] = o.astype(o_ref.dtype)
        return 0

    lax.fori_loop(0, n_blk, body, 0, unroll=4)


def _na_mixer(proj3, t2, mask_add):
    batch, seq, _ = proj3.shape
    heads = GROUP_HEADS

    def head_spec(col):
        return pl.BlockSpec((None, seq, HEAD_DIM), lambda b, h: (b, 0, col // HEAD_DIM + h))

    return pl.pallas_call(
        _na_kernel,
        grid=(batch, heads),
        in_specs=[head_spec(COL_D_Q), head_spec(COL_D_K), head_spec(COL_D_V),
                  pl.BlockSpec((None, NA_T2, GRID_W, LANES), lambda b, h: (h, 0, 0, 0)),
                  pl.BlockSpec(mask_add.shape, lambda b, h: (0, 0, 0))],
        out_specs=pl.BlockSpec((None, seq, HEAD_DIM), lambda b, h: (b, 0, h)),
        out_shape=jax.ShapeDtypeStruct((batch, seq, GROUP_WIDTH), BF16),
        compiler_params=_cparams(("parallel", "parallel")),
        name="na_mixer",
    )(proj3, proj3, proj3, t2, mask_add)


def kernel(x, norm_mix_pre, norm_mix_post, norm_ffn_pre, norm_ffn_post, w_in, w_out, diff_lambda, diff_subln, conv_dw, conv_dw_b, conv_ln_g, conv_ln_b, conv_pw, conv_pw_b, gqa_q_norm, gqa_k_norm, na_rpb, ffn_gate, ffn_up, ffn_down):
    batch, seq, d = x.shape
    depth = w_in.shape[0]
    tokens = batch * seq

    t = np.arange(seq)
    diff_tables = _rope_tables(t, t)
    axial_tables = _rope_tables(t // GRID_W, t % GRID_W)
    mask_add = _na_geometry(seq)

    def row(v):
        return v.reshape(1, -1)

    xf = x.reshape(tokens, d)
    h = _rmsnorm(xf, row(norm_mix_pre[0]), tm=1024)
    for l in range(depth):
        lam_init = 0.8 - 0.6 * math.exp(-0.3 * l)
        proj = _in_proj(h, w_in[l], tm=1024, tn=1024)
        proj3 = proj.reshape(batch, seq, IN_COLS)

        out_a = _flash_call(
            functools.partial(_diff_attn_kernel, tq=ATTN_TQ, tk=ATTN_TK, lam_init=lam_init),
            proj3, diff_tables, (diff_lambda[l], diff_subln[l].reshape(-1, 1)),
            n_groups=GROUP_HEADS, q_width=HEAD_DIM, q_col=COL_A_Q, k_col=COL_A_K, v_col=COL_A_V,
            out_width=HEAD_DIM, tq=ATTN_TQ, name="diff_attn")
        out_b = _conv_mixer(proj3, conv_dw[l], row(conv_dw_b[l]), row(conv_ln_g[l]), row(conv_ln_b[l]),
                            conv_pw[l].astype(BF16), row(conv_pw_b[l]), tm=512)
        out_c = _flash_call(
            functools.partial(_gqa_attn_kernel, tq=ATTN_TQ, tk=ATTN_TK),
            proj3, axial_tables, (row(gqa_q_norm[l]), row(gqa_k_norm[l])),
            n_groups=GQA_KV_HEADS, q_width=2 * HEAD_DIM, q_col=COL_C_Q, k_col=COL_C_K, v_col=COL_C_V,
            out_width=2 * HEAD_DIM, tq=ATTN_TQ, name="gqa_attn")
        out_d = _na_mixer(proj3, _rpb_tiles(na_rpb[l]), mask_add)

        parts = [o.reshape(tokens, GROUP_WIDTH) for o in (out_a, out_b, out_c, out_d)]
        xf, h = _out_proj(parts, w_out[l].astype(BF16), row(norm_mix_post[l]), xf,
                          row(norm_ffn_pre[l]), tm=512)

        act = _ffn_up(h, ffn_gate[l], ffn_up[l], tm=1024, tn=512)
        next_gain = row(norm_mix_pre[l + 1]) if l + 1 < depth else None
        xf, h = _ffn_down(act, ffn_down[l].astype(BF16), row(norm_ffn_post[l]), xf, next_gain,
                          tm=1024, tk=512)
    return xf.reshape(batch, seq, d)
```

```python
import functools
import math

import numpy as np
import jax
import jax.numpy as jnp
from jax import lax
from jax.experimental import pallas as pl
from jax.experimental.pallas import tpu as pltpu

D_MODEL = 2048
HEAD_DIM = 128
GROUP_HEADS = 4
GROUP_WIDTH = GROUP_HEADS * HEAD_DIM
DIFF_QK_DIM = HEAD_DIM // 2
CONV_WIDTH = 31
GQA_KV_HEADS = 2
NA_KH = 8
NA_KW = 16
GRID_W = 64
Q_BLOCK = 128
ROPE_THETA = 10000.0
EPS = 1e-6

COL_A_Q, COL_A_K, COL_A_V = 0, 512, 1024
COL_B_A, COL_B_G = 1536, 2048
COL_C_Q, COL_C_K, COL_C_V = 2560, 3072, 3328
COL_D_Q, COL_D_K, COL_D_V = 3584, 4096, 4608
IN_COLS = 5120

LANES = 128
VMEM_LIMIT = 56 * 1024 * 1024
ROW_CHUNK = 128
ATTN_TQ = 512
ATTN_TK = 512

NA_BAND_ROWS = 10
NA_BAND = NA_BAND_ROWS * GRID_W
NA_CLASSES = 5
NA_T2 = 16
NA_LOOKAHEAD = 2
CONV_HALO = 16

F32 = jnp.float32
BF16 = jnp.bfloat16


def _cparams(semantics):
    return pltpu.CompilerParams(dimension_semantics=semantics,
                                vmem_limit_bytes=VMEM_LIMIT)


def _rms(x):
    return x * lax.rsqrt(jnp.mean(x * x, axis=-1, keepdims=True) + EPS)


def _rmsnorm_kernel(x_ref, g_ref, h_ref):
    def body(c, _):
        r0 = pl.multiple_of(c * ROW_CHUNK, ROW_CHUNK)
        h_ref[pl.ds(r0, ROW_CHUNK), :] = (_rms(x_ref[pl.ds(r0, ROW_CHUNK), :]) * g_ref[...]).astype(h_ref.dtype)
        return 0

    lax.fori_loop(0, x_ref.shape[0] // ROW_CHUNK, body, 0)


def _rmsnorm(x, g, *, tm):
    m, d = x.shape
    return pl.pallas_call(
        _rmsnorm_kernel,
        grid=(m // tm,),
        in_specs=[pl.BlockSpec((tm, d), lambda i: (i, 0)),
                  pl.BlockSpec((1, d), lambda i: (0, 0))],
        out_specs=pl.BlockSpec((tm, d), lambda i: (i, 0)),
        out_shape=jax.ShapeDtypeStruct((m, d), BF16),
        compiler_params=_cparams(("parallel",)),
        name="rmsnorm",
    )(x, g)


def _norm_residual_rows(acc_ref, g_ref, x_ref, o_ref, gn_ref, hn_ref):
    def body(c, _):
        r0 = pl.multiple_of(c * ROW_CHUNK, ROW_CHUNK)
        o = x_ref[pl.ds(r0, ROW_CHUNK), :] + _rms(acc_ref[pl.ds(r0, ROW_CHUNK), :]) * g_ref[...]
        o_ref[pl.ds(r0, ROW_CHUNK), :] = o
        if hn_ref is not None:
            hn_ref[pl.ds(r0, ROW_CHUNK), :] = (_rms(o) * gn_ref[...]).astype(hn_ref.dtype)
        return 0

    lax.fori_loop(0, acc_ref.shape[0] // ROW_CHUNK, body, 0)


def _cast_weight_tile(w_ref, wb_ref):
    def body(c, _):
        r0 = pl.multiple_of(c * ROW_CHUNK, ROW_CHUNK)
        wb_ref[pl.ds(r0, ROW_CHUNK), :] = w_ref[pl.ds(r0, ROW_CHUNK), :].astype(wb_ref.dtype)
        return 0

    lax.fori_loop(0, w_ref.shape[0] // ROW_CHUNK, body, 0)


def _in_proj_kernel(h_ref, w_ref, o_ref, wb_ref):
    @pl.when(pl.program_id(1) == 0)
    def _():
        _cast_weight_tile(w_ref, wb_ref)

    o_ref[...] = jnp.dot(h_ref[...], wb_ref[...], preferred_element_type=F32).astype(o_ref.dtype)


def _in_proj(h, w, layer, *, tm, tn):
    m, d = h.shape
    n = w.shape[2]
    return pl.pallas_call(
        _in_proj_kernel,
        grid=(n // tn, m // tm),
        in_specs=[pl.BlockSpec((tm, d), lambda j, i: (i, 0)),
                  pl.BlockSpec((None, d, tn), lambda j, i: (layer, 0, j))],
        out_specs=pl.BlockSpec((tm, tn), lambda j, i: (i, j)),
        out_shape=jax.ShapeDtypeStruct((m, n), BF16),
        scratch_shapes=[pltpu.VMEM((d, tn), BF16)],
        compiler_params=_cparams(("parallel", "arbitrary")),
        name="in_proj",
    )(h, w)


def _ffn_up_kernel(h_ref, wg_ref, wu_ref, o_ref, wgb_ref, wub_ref):
    @pl.when(pl.program_id(1) == 0)
    def _():
        _cast_weight_tile(wg_ref, wgb_ref)
        _cast_weight_tile(wu_ref, wub_ref)

    h = h_ref[...]
    gate = jnp.dot(h, wgb_ref[...], preferred_element_type=F32)
    up = jnp.dot(h, wub_ref[...], preferred_element_type=F32)
    o_ref[...] = (gate * jax.nn.sigmoid(gate) * up).astype(o_ref.dtype)


def _ffn_up(h, wg, wu, layer, *, tm, tn):
    m, d = h.shape
    n = wg.shape[2]
    w_spec = pl.BlockSpec((None, d, tn), lambda j, i: (layer, 0, j))
    return pl.pallas_call(
        _ffn_up_kernel,
        grid=(n // tn, m // tm),
        in_specs=[pl.BlockSpec((tm, d), lambda j, i: (i, 0)), w_spec, w_spec],
        out_specs=pl.BlockSpec((tm, tn), lambda j, i: (i, j)),
        out_shape=jax.ShapeDtypeStruct((m, n), BF16),
        scratch_shapes=[pltpu.VMEM((d, tn), BF16), pltpu.VMEM((d, tn), BF16)],
        compiler_params=_cparams(("parallel", "arbitrary")),
        name="ffn_up",
    )(h, wg, wu)


def _ffn_down_kernel(*refs, emit_next):
    if emit_next:
        a_ref, w_ref, g_ref, x_ref, gn_ref, o_ref, hn_ref, acc_ref = refs
    else:
        a_ref, w_ref, g_ref, x_ref, o_ref, acc_ref = refs
        gn_ref = hn_ref = None
    k = pl.program_id(1)

    @pl.when(k == 0)
    def _():
        acc_ref[...] = jnp.zeros_like(acc_ref)

    acc_ref[...] += jnp.dot(a_ref[...], w_ref[...], preferred_element_type=F32)

    @pl.when(k == pl.num_programs(1) - 1)
    def _():
        _norm_residual_rows(acc_ref, g_ref, x_ref, o_ref, gn_ref, hn_ref)


def _ffn_down(a, w, g, x, gn, *, tm, tk):
    m, kdim = a.shape
    d = w.shape[1]
    emit_next = gn is not None
    vec = pl.BlockSpec((1, d), lambda i, k: (0, 0))
    row_tile = pl.BlockSpec((tm, d), lambda i, k: (i, 0))
    outs = pl.pallas_call(
        functools.partial(_ffn_down_kernel, emit_next=emit_next),
        grid=(m // tm, kdim // tk),
        in_specs=[pl.BlockSpec((tm, tk), lambda i, k: (i, k)),
                  pl.BlockSpec((tk, d), lambda i, k: (k, 0)),
                  vec, row_tile] + [vec] * emit_next,
        out_specs=[row_tile] + [row_tile] * emit_next,
        out_shape=[jax.ShapeDtypeStruct((m, d), F32)] + [jax.ShapeDtypeStruct((m, d), BF16)] * emit_next,
        scratch_shapes=[pltpu.VMEM((tm, d), F32)],
        compiler_params=_cparams(("parallel", "arbitrary")),
        name="ffn_down",
    )(a, w, g, x, *([gn] * emit_next))
    return (outs[0], outs[1]) if emit_next else (outs[0], None)


def _out_proj_kernel(a_ref, b_ref, c_ref, d_ref, w_ref, g_ref, x_ref, gn_ref, o_ref, hn_ref,
                     lhs_ref, acc_ref):
    for i, part in enumerate((a_ref, b_ref, c_ref, d_ref)):
        lhs_ref[:, i * GROUP_WIDTH:(i + 1) * GROUP_WIDTH] = part[...]
    acc_ref[...] = jnp.dot(lhs_ref[...], w_ref[...], preferred_element_type=F32)
    _norm_residual_rows(acc_ref, g_ref, x_ref, o_ref, gn_ref, hn_ref)


def _out_proj(parts, w, g, x, gn, *, tm):
    m, d = x.shape
    part_spec = pl.BlockSpec((tm, GROUP_WIDTH), lambda i: (i, 0))
    vec = pl.BlockSpec((1, d), lambda i: (0, 0))
    row_tile = pl.BlockSpec((tm, d), lambda i: (i, 0))
    return pl.pallas_call(
        _out_proj_kernel,
        grid=(m // tm,),
        in_specs=[part_spec, part_spec, part_spec, part_spec,
                  pl.BlockSpec((4 * GROUP_WIDTH, d), lambda i: (0, 0)),
                  vec, row_tile, vec],
        out_specs=[row_tile, row_tile],
        out_shape=[jax.ShapeDtypeStruct((m, d), F32), jax.ShapeDtypeStruct((m, d), BF16)],
        scratch_shapes=[pltpu.VMEM((tm, 4 * GROUP_WIDTH), BF16),
                        pltpu.VMEM((tm, d), F32)],
        compiler_params=_cparams(("parallel",)),
        name="out_proj",
    )(*parts, w, g, x, gn)


def _rope_tables(pos_lo, pos_hi):
    half = DIFF_QK_DIM // 2
    inv = ROPE_THETA ** (-np.arange(half, dtype=np.float64) / half)
    lane = np.arange(LANES)
    pos = np.where(lane[None, :] < DIFF_QK_DIM, pos_lo[:, None], pos_hi[:, None]).astype(np.float64)
    ang = pos * inv[lane % half][None, :]
    sign = np.where((lane % DIFF_QK_DIM) < half, -1.0, 1.0)
    return (jnp.asarray(np.cos(ang), dtype=F32),
            jnp.asarray(np.sin(ang) * sign[None, :], dtype=F32))


def _rope(x, cos, sin_signed):
    lane = lax.broadcasted_iota(jnp.int32, x.shape, 1)
    first_half = (lane & (DIFF_QK_DIM // 2)) == 0
    partner = jnp.where(first_half, pltpu.roll(x, LANES - DIFF_QK_DIM // 2, 1),
                        pltpu.roll(x, DIFF_QK_DIM // 2, 1))
    return x * cos + partner * sin_signed


LOG2E = math.log2(math.e)


def _attend(qs_ref, krot_ref, vt_ref, acc_ref, *, tk):
    seq = krot_ref.shape[0]
    n_tiles = seq // tk

    def scores(j):
        k = krot_ref[j * tk:(j + 1) * tk, :]
        return lax.dot_general(k, qs_ref[...], (((1,), (1,)), ((), ())), preferred_element_type=F32)

    m = l = None
    s_next = scores(0)
    for j in range(n_tiles):
        s = s_next
        if j + 1 < n_tiles:
            s_next = scores(j + 1)
        m_cur = jnp.max(s, axis=0, keepdims=True)
        m_next = m_cur if m is None else jnp.maximum(m, m_cur)
        p = jnp.exp2(s - m_next)
        pv = jnp.dot(vt_ref[:, j * tk:(j + 1) * tk], p.astype(BF16), preferred_element_type=F32)
        if m is None:
            l = jnp.sum(p, axis=0, keepdims=True)
            acc_ref[...] = pv
        else:
            alpha = jnp.exp2(m - m_next)
            l = alpha * l + jnp.sum(p, axis=0, keepdims=True)
            acc_ref[...] = alpha * acc_ref[...] + pv
        m = m_next
    return l


def _prepare_keys_values(k_ref, v_ref, cos_ref, sin_ref, krot_ref, vt_ref, gain):
    seq = k_ref.shape[0]

    def body(c, _):
        r0 = pl.multiple_of(c * ROW_CHUNK, ROW_CHUNK)
        k = k_ref[pl.ds(r0, ROW_CHUNK), :].astype(F32)
        if gain is not None:
            k = _rms(k) * gain
        krot_ref[pl.ds(r0, ROW_CHUNK), :] = _rope(
            k, cos_ref[pl.ds(r0, ROW_CHUNK), :], sin_ref[pl.ds(r0, ROW_CHUNK), :]).astype(BF16)
        vt_ref[:, pl.ds(r0, ROW_CHUNK)] = v_ref[pl.ds(r0, ROW_CHUNK), :].astype(F32).T.astype(BF16)
        return 0

    lax.fori_loop(0, seq // ROW_CHUNK, body, 0)


def _diff_attn_kernel(q_ref, k_ref, v_ref, cq_ref, sq_ref, ck_ref, sk_ref, lam_ref, g_ref, o_ref,
                      krot_ref, vt_ref, qs_ref, acc_ref, *, tq, tk, lam_init):
    @pl.when(pl.program_id(1) == 0)
    def _():
        _prepare_keys_values(k_ref, v_ref, ck_ref, sk_ref, krot_ref, vt_ref, None)

    q = _rope(q_ref[...].astype(F32), cq_ref[...], sq_ref[...]) * (DIFF_QK_DIM ** -0.5 * LOG2E)
    lane = lax.broadcasted_iota(jnp.int32, q.shape, 1)
    qs_ref[0:tq, :] = jnp.where(lane < DIFF_QK_DIM, q, 0.0).astype(BF16)
    qs_ref[tq:2 * tq, :] = jnp.where(lane >= DIFF_QK_DIM, q, 0.0).astype(BF16)

    l = _attend(qs_ref, krot_ref, vt_ref, acc_ref, tk=tk)

    lp = lam_ref[...]
    lam = (jnp.exp(jnp.sum(lp[0:1] * lp[1:2], keepdims=True))
           - jnp.exp(jnp.sum(lp[2:3] * lp[3:4], keepdims=True)) + lam_init)
    ot = (acc_ref[:, 0:tq] / l[:, 0:tq] - lam * (acc_ref[:, tq:2 * tq] / l[:, tq:2 * tq]))
    ms = jnp.mean(ot * ot, axis=0, keepdims=True)
    ot = ot * lax.rsqrt(ms + EPS) * (g_ref[...] * (1.0 - lam_init))
    o_ref[...] = ot.T.astype(o_ref.dtype)


def _gqa_attn_kernel(q_ref, k_ref, v_ref, cq_ref, sq_ref, ck_ref, sk_ref, qn_ref, kn_ref, o_ref,
                     krot_ref, vt_ref, qs_ref, acc_ref, *, tq, tk):
    @pl.when(pl.program_id(1) == 0)
    def _():
        _prepare_keys_values(k_ref, v_ref, ck_ref, sk_ref, krot_ref, vt_ref, kn_ref[...])

    for r in range(2):
        q = q_ref[:, r * HEAD_DIM:(r + 1) * HEAD_DIM].astype(F32)
        q = _rope(_rms(q) * qn_ref[...], cq_ref[...], sq_ref[...]) * (HEAD_DIM ** -0.5 * LOG2E)
        qs_ref[r * tq:(r + 1) * tq, :] = q.astype(BF16)

    l = _attend(qs_ref, krot_ref, vt_ref, acc_ref, tk=tk)

    for r in range(2):
        ot = acc_ref[:, r * tq:(r + 1) * tq] / l[:, r * tq:(r + 1) * tq]
        o_ref[:, r * HEAD_DIM:(r + 1) * HEAD_DIM] = ot.T.astype(o_ref.dtype)


def _flash_call(kernel, proj3, tables, params, *, n_groups, q_width, q_col, k_col, v_col,
                out_width, tq, name):
    batch, seq, _ = proj3.shape
    cos_t, sin_t = tables
    qb, kb, vb = q_col // q_width, k_col // HEAD_DIM, v_col // HEAD_DIM

    def grp(i):
        return i // n_groups, i % n_groups

    in_specs = [
        pl.BlockSpec((None, tq, q_width), lambda i, j: (grp(i)[0], j, qb + grp(i)[1])),
        pl.BlockSpec((None, seq, HEAD_DIM), lambda i, j: (grp(i)[0], 0, kb + grp(i)[1])),
        pl.BlockSpec((None, seq, HEAD_DIM), lambda i, j: (grp(i)[0], 0, vb + grp(i)[1])),
        pl.BlockSpec((tq, LANES), lambda i, j: (j, 0)),
        pl.BlockSpec((tq, LANES), lambda i, j: (j, 0)),
        pl.BlockSpec((seq, LANES), lambda i, j: (0, 0)),
        pl.BlockSpec((seq, LANES), lambda i, j: (0, 0)),
    ] + [pl.BlockSpec(p.shape, lambda i, j: (0, 0)) for p in params]
    return pl.pallas_call(
        kernel,
        grid=(batch * n_groups, seq // tq),
        in_specs=in_specs,
        out_specs=pl.BlockSpec((None, tq, out_width), lambda i, j: (grp(i)[0], j, grp(i)[1])),
        out_shape=jax.ShapeDtypeStruct((batch, seq, GROUP_WIDTH), BF16),
        scratch_shapes=[pltpu.VMEM((seq, HEAD_DIM), BF16),
                        pltpu.VMEM((HEAD_DIM, seq), BF16),
                        pltpu.VMEM((2 * tq, HEAD_DIM), BF16),
                        pltpu.VMEM((HEAD_DIM, 2 * tq), F32)],
        compiler_params=_cparams(("parallel", "arbitrary")),
        name=name,
    )(proj3, proj3, proj3, cos_t, sin_t, cos_t, sin_t, *params)


def _conv_kernel(a_ref, g_ref, dw_ref, dwb_ref, lng_ref, lnb_ref, pw_ref, pwb_ref, o_ref,
                 u_ref, acc_ref, y_ref, *, tm):
    seq = a_ref.shape[0]
    t = pl.program_id(1)

    @pl.when(t == 0)
    def _():
        zeros = jnp.zeros((CONV_HALO, u_ref.shape[1]), F32)
        u_ref[0:CONV_HALO, :] = zeros
        u_ref[CONV_HALO + seq:2 * CONV_HALO + seq, :] = zeros

        def body(c, _):
            r0 = pl.multiple_of(c * ROW_CHUNK, ROW_CHUNK)
            a = a_ref[pl.ds(r0, ROW_CHUNK), :].astype(F32)
            g = g_ref[pl.ds(r0, ROW_CHUNK), :].astype(F32)
            u_ref[pl.ds(CONV_HALO + r0, ROW_CHUNK), :] = a * jax.nn.sigmoid(g)
            return 0

        lax.fori_loop(0, seq // ROW_CHUNK, body, 0)

    pad = CONV_WIDTH // 2
    sub = 8
    n_ch = u_ref.shape[1]

    def row_block(rb, _):
        base = pl.multiple_of(t * tm + rb * ROW_CHUNK, ROW_CHUNK)
        for cg in range(n_ch // LANES):
            cols = slice(cg * LANES, (cg + 1) * LANES)
            win = u_ref[pl.ds(base, ROW_CHUNK + 2 * CONV_HALO), cols]
            acc = jnp.zeros((ROW_CHUNK, LANES), F32) + dwb_ref[:, cols]
            for r in range(sub):
                shifted = win if r == 0 else pltpu.roll(win, win.shape[0] - r, 0)
                for a in range(2 * CONV_HALO // sub):
                    k = sub * a + r - (CONV_HALO - pad)
                    if 0 <= k < CONV_WIDTH:
                        acc = acc + shifted[sub * a:sub * a + ROW_CHUNK, :] * dw_ref[k:k + 1, cols]
            acc_ref[:, cols] = acc
        acc = acc_ref[...]
        mu = jnp.mean(acc, axis=-1, keepdims=True)
        cen = acc - mu
        var = jnp.mean(cen * cen, axis=-1, keepdims=True)
        y = cen * lax.rsqrt(var + EPS) * lng_ref[...] + lnb_ref[...]
        y_ref[pl.ds(pl.multiple_of(rb * ROW_CHUNK, ROW_CHUNK), ROW_CHUNK), :] = (
            y * jax.nn.sigmoid(y)).astype(BF16)
        return 0

    lax.fori_loop(0, tm // ROW_CHUNK, row_block, 0)
    o_ref[...] = (jnp.dot(y_ref[...], pw_ref[...], preferred_element_type=F32)
                  + pwb_ref[...]).astype(o_ref.dtype)


def _conv_mixer(proj3, dw, dwb, lng, lnb, pw, pwb, *, tm):
    batch, seq, _ = proj3.shape
    ch = GROUP_WIDTH
    vec = pl.BlockSpec((1, ch), lambda b, t: (0, 0))
    return pl.pallas_call(
        functools.partial(_conv_kernel, tm=tm),
        grid=(batch, seq // tm),
        in_specs=[pl.BlockSpec((None, seq, ch), lambda b, t: (b, 0, COL_B_A // ch)),
                  pl.BlockSpec((None, seq, ch), lambda b, t: (b, 0, COL_B_G // ch)),
                  pl.BlockSpec((CONV_WIDTH, ch), lambda b, t: (0, 0)),
                  vec, vec, vec,
                  pl.BlockSpec((ch, ch), lambda b, t: (0, 0)),
                  vec],
        out_specs=pl.BlockSpec((None, tm, ch), lambda b, t: (b, t, 0)),
        out_shape=jax.ShapeDtypeStruct((batch, seq, ch), BF16),
        scratch_shapes=[pltpu.VMEM((seq + 2 * CONV_HALO, ch), F32),
                        pltpu.VMEM((ROW_CHUNK, ch), F32),
                        pltpu.VMEM((tm, ch), BF16)],
        compiler_params=_cparams(("parallel", "arbitrary")),
        name="conv_mixer",
    )(proj3, proj3, dw, dwb, lng, lnb, pw, pwb)


def _na_geometry(seq):
    rows = seq // GRID_W
    qr_blk = Q_BLOCK // GRID_W
    mask_add = np.zeros((NA_CLASSES, Q_BLOCK, NA_BAND), np.float32)
    q_local = np.arange(Q_BLOCK)
    p = np.arange(NA_BAND)
    for c, blk in enumerate(_na_class_blocks(seq)):
        bs = int(np.clip(blk * qr_blk - NA_KH // 2, 0, rows - NA_BAND_ROWS))
        q_row = (blk * qr_blk + q_local // GRID_W)[:, None]
        q_col = (q_local % GRID_W)[:, None]
        k_row = (bs + p // GRID_W)[None, :]
        k_col = (p % GRID_W)[None, :]
        win_r = np.clip(q_row - NA_KH // 2, 0, rows - NA_KH)
        win_c = np.clip(q_col - NA_KW // 2, 0, GRID_W - NA_KW)
        ok = (k_row >= win_r) & (k_row < win_r + NA_KH) & (k_col >= win_c) & (k_col < win_c + NA_KW)
        mask_add[c] = np.where(ok, 0.0, -1e30)
    return jnp.asarray(mask_add)


def _na_class_blocks(seq):
    n_blk = seq // Q_BLOCK
    return (0, 1, 2, n_blk - 2, n_blk - 1)


def _rpb_tiles_kernel(rpb_ref, mask_ref, o_ref, t2_ref, *, seq):
    h = pl.program_id(0)
    n_c = 2 * NA_KW - 1
    n_r = 2 * NA_KH - 1
    shape = (GRID_W, LANES)
    lane = lax.broadcasted_iota(jnp.int32, shape, 1)
    q_col = lax.broadcasted_iota(jnp.int32, shape, 0)
    ic = jnp.clip((lane & (GRID_W - 1)) - q_col + (NA_KW - 1), 0, n_c - 1)
    left = lane < GRID_W
    base = h * (n_r * n_c)
    for i in range(NA_T2):
        i_l = min(max(i - 1, 0), n_r - 1)
        i_r = min(max(i, 0), n_r - 1)

        def body(j, acc, i_l=i_l, i_r=i_r):
            coef = jnp.where(left, rpb_ref[base + i_l * n_c + j], rpb_ref[base + i_r * n_c + j])
            return acc + jnp.where(ic == j, coef, 0.0)

        t2_ref[i] = lax.fori_loop(0, n_c, body, jnp.zeros(shape, F32))

    rows = seq // GRID_W
    qr_blk = Q_BLOCK // GRID_W
    for c, blk in enumerate(_na_class_blocks(seq)):
        bs = min(max(blk * qr_blk - NA_KH // 2, 0), rows - NA_BAND_ROWS)
        for a in range(qr_blk):
            for r2 in range(NA_BAND_ROWS // 2):
                i1 = bs + 2 * r2 - (blk * qr_blk + a) + (NA_KH - 1)
                idx = min(max(i1, -1), NA_T2 - 2) + 1
                rs = slice(a * GRID_W, (a + 1) * GRID_W)
                cs = slice(r2 * LANES, (r2 + 1) * LANES)
                o_ref[c, rs, cs] = t2_ref[idx] + mask_ref[c, rs, cs]


def _rpb_tiles(rpb, mask_add, seq):
    heads = rpb.shape[0]
    return pl.pallas_call(
        functools.partial(_rpb_tiles_kernel, seq=seq),
        grid=(heads,),
        in_specs=[pl.BlockSpec(memory_space=pltpu.SMEM),
                  pl.BlockSpec(mask_add.shape, lambda h: (0, 0, 0))],
        out_specs=pl.BlockSpec((None,) + mask_add.shape, lambda h: (h, 0, 0, 0)),
        out_shape=jax.ShapeDtypeStruct((heads,) + mask_add.shape, F32),
        scratch_shapes=[pltpu.VMEM((NA_T2, GRID_W, LANES), F32)],
        compiler_params=_cparams(("parallel",)),
        name="na_rpb_tiles",
    )(rpb.reshape(-1), mask_add)


def _na_kernel(q_ref, k_ref, v_ref, bias_ref, o_ref):
    seq = q_ref.shape[0]
    rows = seq // GRID_W
    qr_blk = Q_BLOCK // GRID_W
    n_blk = seq // Q_BLOCK

    classes = _na_class_blocks(seq)

    def band(blk):
        k0 = min(max(blk * qr_blk - NA_KH // 2, 0), rows - NA_BAND_ROWS) * GRID_W
        return slice(k0, k0 + NA_BAND)

    def scores(blk):
        q = q_ref[blk * Q_BLOCK:(blk + 1) * Q_BLOCK, :]
        return lax.dot_general(q, k_ref[band(blk), :], (((1,), (1,)), ((), ())),
                               preferred_element_type=F32)

    pending = [scores(b) for b in range(NA_LOOKAHEAD)]
    for blk in range(n_blk):
        s = pending.pop(0)
        if blk + NA_LOOKAHEAD < n_blk:
            pending.append(scores(blk + NA_LOOKAHEAD))
        cls = classes.index(blk) if blk in classes else 2
        s = s * (HEAD_DIM ** -0.5) + bias_ref[cls]
        m = jnp.max(s, axis=1, keepdims=True)
        p = jnp.exp(s - m)
        l = jnp.sum(p, axis=1, keepdims=True)
        o = jnp.dot(p.astype(BF16), v_ref[band(blk), :], preferred_element_type=F32) / l
        o_ref[blk * Q_BLOCK:(blk + 1) * Q_BLOCK, :] = o.astype(o_ref.dtype)


def _na_mixer(proj3, bias):
    batch, seq, _ = proj3.shape
    heads = GROUP_HEADS

    def head_spec(col):
        return pl.BlockSpec((None, seq, HEAD_DIM), lambda b, h: (b, 0, col // HEAD_DIM + h))

    return pl.pallas_call(
        _na_kernel,
        grid=(batch, heads),
        in_specs=[head_spec(COL_D_Q), head_spec(COL_D_K), head_spec(COL_D_V),
                  pl.BlockSpec((None,) + bias.shape[1:], lambda b, h: (h, 0, 0, 0))],
        out_specs=pl.BlockSpec((None, seq, HEAD_DIM), lambda b, h: (b, 0, h)),
        out_shape=jax.ShapeDtypeStruct((batch, seq, GROUP_WIDTH), BF16),
        compiler_params=_cparams(("parallel", "parallel")),
        name="na_mixer",
    )(proj3, proj3, proj3, bias)


def kernel(x, norm_mix_pre, norm_mix_post, norm_ffn_pre, norm_ffn_post, w_in, w_out, diff_lambda, diff_subln, conv_dw, conv_dw_b, conv_ln_g, conv_ln_b, conv_pw, conv_pw_b, gqa_q_norm, gqa_k_norm, na_rpb, ffn_gate, ffn_up, ffn_down):
    batch, seq, d = x.shape
    depth = w_in.shape[0]
    tokens = batch * seq

    t = np.arange(seq)
    diff_tables = _rope_tables(t, t)
    axial_tables = _rope_tables(t // GRID_W, t % GRID_W)
    mask_add = _na_geometry(seq)

    def row(v):
        return v.reshape(1, -1)

    xf = x.reshape(tokens, d)
    h = _rmsnorm(xf, row(norm_mix_pre[0]), tm=1024)
    for l in range(depth):
        lam_init = 0.8 - 0.6 * math.exp(-0.3 * l)
        proj = _in_proj(h, w_in, l, tm=1024, tn=1024)
        proj3 = proj.reshape(batch, seq, IN_COLS)

        out_a = _flash_call(
            functools.partial(_diff_attn_kernel, tq=ATTN_TQ, tk=ATTN_TK, lam_init=lam_init),
            proj3, diff_tables, (diff_lambda[l], diff_subln[l].reshape(-1, 1)),
            n_groups=GROUP_HEADS, q_width=HEAD_DIM, q_col=COL_A_Q, k_col=COL_A_K, v_col=COL_A_V,
            out_width=HEAD_DIM, tq=ATTN_TQ, name="diff_attn")
        out_b = _conv_mixer(proj3, conv_dw[l], row(conv_dw_b[l]), row(conv_ln_g[l]), row(conv_ln_b[l]),
                            conv_pw[l].astype(BF16), row(conv_pw_b[l]), tm=512)
        out_c = _flash_call(
            functools.partial(_gqa_attn_kernel, tq=ATTN_TQ, tk=ATTN_TK),
            proj3, axial_tables, (row(gqa_q_norm[l]), row(gqa_k_norm[l])),
            n_groups=GQA_KV_HEADS, q_width=2 * HEAD_DIM, q_col=COL_C_Q, k_col=COL_C_K, v_col=COL_C_V,
            out_width=2 * HEAD_DIM, tq=ATTN_TQ, name="gqa_attn")
        out_d = _na_mixer(proj3, _rpb_tiles(na_rpb[l], mask_add, seq))

        parts = [o.reshape(tokens, GROUP_WIDTH) for o in (out_a, out_b, out_c, out_d)]
        xf, h = _out_proj(parts, w_out[l].astype(BF16), row(norm_mix_post[l]), xf,
                          row(norm_ffn_pre[l]), tm=512)

        act = _ffn_up(h, ffn_gate, ffn_up, l, tm=1024, tn=512)
        next_gain = row(norm_mix_pre[l + 1]) if l + 1 < depth else None
        xf, h = _ffn_down(act, ffn_down[l].astype(BF16), row(norm_ffn_post[l]), xf, next_gain,
                          tm=1024, tk=512)
    return xf.reshape(batch, seq, d)
```

```python
import functools
import math

import numpy as np
import jax
import jax.numpy as jnp
from jax import lax
from jax.experimental import pallas as pl
from jax.experimental.pallas import tpu as pltpu

D_MODEL = 2048
HEAD_DIM = 128
GROUP_HEADS = 4
GROUP_WIDTH = GROUP_HEADS * HEAD_DIM
DIFF_QK_DIM = HEAD_DIM // 2
CONV_WIDTH = 31
GQA_KV_HEADS = 2
NA_KH = 8
NA_KW = 16
GRID_W = 64
Q_BLOCK = 128
ROPE_THETA = 10000.0
EPS = 1e-6

COL_A_Q, COL_A_K, COL_A_V = 0, 512, 1024
COL_B_A, COL_B_G = 1536, 2048
COL_C_Q, COL_C_K, COL_C_V = 2560, 3072, 3328
COL_D_Q, COL_D_K, COL_D_V = 3584, 4096, 4608
IN_COLS = 5120

LANES = 128
VMEM_LIMIT = 56 * 1024 * 1024
ROW_CHUNK = 128
ATTN_TQ = 512
ATTN_TK = 512

NA_BAND_ROWS = 10
NA_BAND = NA_BAND_ROWS * GRID_W
NA_CLASSES = 5
NA_T2 = 16
NA_LOOKAHEAD = 2
CONV_HALO = 16

F32 = jnp.float32
BF16 = jnp.bfloat16


def _cparams(semantics):
    return pltpu.CompilerParams(dimension_semantics=semantics,
                                vmem_limit_bytes=VMEM_LIMIT)


def _rms(x):
    return x * lax.rsqrt(jnp.mean(x * x, axis=-1, keepdims=True) + EPS)


def _rmsnorm_kernel(x_ref, g_ref, h_ref):
    def body(c, _):
        r0 = pl.multiple_of(c * ROW_CHUNK, ROW_CHUNK)
        h_ref[pl.ds(r0, ROW_CHUNK), :] = (_rms(x_ref[pl.ds(r0, ROW_CHUNK), :]) * g_ref[...]).astype(h_ref.dtype)
        return 0

    lax.fori_loop(0, x_ref.shape[0] // ROW_CHUNK, body, 0)


def _rmsnorm(x, g, *, tm):
    m, d = x.shape
    return pl.pallas_call(
        _rmsnorm_kernel,
        grid=(m // tm,),
        in_specs=[pl.BlockSpec((tm, d), lambda i: (i, 0)),
                  pl.BlockSpec((1, d), lambda i: (0, 0))],
        out_specs=pl.BlockSpec((tm, d), lambda i: (i, 0)),
        out_shape=jax.ShapeDtypeStruct((m, d), BF16),
        compiler_params=_cparams(("parallel",)),
        name="rmsnorm",
    )(x, g)


def _cast_weight_tile(w_ref, wb_ref):
    def body(c, _):
        r0 = pl.multiple_of(c * ROW_CHUNK, ROW_CHUNK)
        wb_ref[pl.ds(r0, ROW_CHUNK), :] = w_ref[pl.ds(r0, ROW_CHUNK), :].astype(wb_ref.dtype)
        return 0

    lax.fori_loop(0, w_ref.shape[0] // ROW_CHUNK, body, 0)


def _in_proj_kernel(h_ref, w_ref, o_ref, wb_ref):
    @pl.when(pl.program_id(1) == 0)
    def _():
        _cast_weight_tile(w_ref, wb_ref)

    o_ref[...] = jnp.dot(h_ref[...], wb_ref[...], preferred_element_type=F32).astype(o_ref.dtype)


def _in_proj(h, w, layer, *, tm, tn):
    m, d = h.shape
    n = w.shape[2]
    return pl.pallas_call(
        _in_proj_kernel,
        grid=(n // tn, m // tm),
        in_specs=[pl.BlockSpec((tm, d), lambda j, i: (i, 0)),
                  pl.BlockSpec((None, d, tn), lambda j, i: (layer, 0, j))],
        out_specs=pl.BlockSpec((tm, tn), lambda j, i: (i, j)),
        out_shape=jax.ShapeDtypeStruct((m, n), BF16),
        scratch_shapes=[pltpu.VMEM((d, tn), BF16)],
        compiler_params=_cparams(("parallel", "arbitrary")),
        name="in_proj",
    )(h, w)


def _ffn_up_kernel(h_ref, wg_ref, wu_ref, o_ref, wgb_ref, wub_ref):
    @pl.when(pl.program_id(1) == 0)
    def _():
        _cast_weight_tile(wg_ref, wgb_ref)
        _cast_weight_tile(wu_ref, wub_ref)

    h = h_ref[...]
    gate = jnp.dot(h, wgb_ref[...], preferred_element_type=F32)
    up = jnp.dot(h, wub_ref[...], preferred_element_type=F32)
    o_ref[...] = (gate * jax.nn.sigmoid(gate) * up).astype(o_ref.dtype)


def _ffn_up(h, wg, wu, layer, *, tm, tn):
    m, d = h.shape
    n = wg.shape[2]
    w_spec = pl.BlockSpec((None, d, tn), lambda j, i: (layer, 0, j))
    return pl.pallas_call(
        _ffn_up_kernel,
        grid=(n // tn, m // tm),
        in_specs=[pl.BlockSpec((tm, d), lambda j, i: (i, 0)), w_spec, w_spec],
        out_specs=pl.BlockSpec((tm, tn), lambda j, i: (i, j)),
        out_shape=jax.ShapeDtypeStruct((m, n), BF16),
        scratch_shapes=[pltpu.VMEM((d, tn), BF16), pltpu.VMEM((d, tn), BF16)],
        compiler_params=_cparams(("parallel", "arbitrary")),
        name="ffn_up",
    )(h, wg, wu)


def _ffn_down_kernel(*refs, emit_next):
    if emit_next:
        a_ref, w_ref, g_ref, x_ref, gn_ref, o_ref, hn_ref, f_ref = refs
    else:
        a_ref, w_ref, g_ref, x_ref, o_ref, f_ref = refs
        gn_ref = hn_ref = None
    j = pl.program_id(1)
    n_col, _, tn = f_ref.shape
    f_ref[j] = jnp.dot(a_ref[...], w_ref[...], preferred_element_type=F32)

    @pl.when(j == n_col - 1)
    def _():
        d = n_col * tn

        def body(c, _):
            r0 = pl.multiple_of(c * ROW_CHUNK, ROW_CHUNK)
            rows = pl.ds(r0, ROW_CHUNK)
            f = [f_ref[b, rows, :] for b in range(n_col)]
            r = lax.rsqrt(sum(jnp.sum(fb * fb, axis=-1, keepdims=True) for fb in f) * (1.0 / d) + EPS)
            o = [x_ref[rows, b * tn:(b + 1) * tn] + f[b] * r * g_ref[:, b * tn:(b + 1) * tn]
                 for b in range(n_col)]
            for b in range(n_col):
                o_ref[rows, b * tn:(b + 1) * tn] = o[b]
            if hn_ref is not None:
                r = lax.rsqrt(sum(jnp.sum(ob * ob, axis=-1, keepdims=True) for ob in o) * (1.0 / d) + EPS)
                for b in range(n_col):
                    hn_ref[rows, b * tn:(b + 1) * tn] = (
                        o[b] * r * gn_ref[:, b * tn:(b + 1) * tn]).astype(hn_ref.dtype)
            return 0

        lax.fori_loop(0, f_ref.shape[1] // ROW_CHUNK, body, 0)


def _ffn_down(a, w, g, x, gn, *, tm, tn):
    m, kdim = a.shape
    d = w.shape[1]
    emit_next = gn is not None
    vec = pl.BlockSpec((1, d), lambda i, j: (0, 0))
    row_tile = pl.BlockSpec((tm, d), lambda i, j: (i, 0))
    outs = pl.pallas_call(
        functools.partial(_ffn_down_kernel, emit_next=emit_next),
        grid=(m // tm, d // tn),
        in_specs=[pl.BlockSpec((tm, kdim), lambda i, j: (i, 0)),
                  pl.BlockSpec((kdim, tn), lambda i, j: (0, j)),
                  vec, row_tile] + [vec] * emit_next,
        out_specs=[row_tile] + [row_tile] * emit_next,
        out_shape=[jax.ShapeDtypeStruct((m, d), F32)] + [jax.ShapeDtypeStruct((m, d), BF16)] * emit_next,
        scratch_shapes=[pltpu.VMEM((d // tn, tm, tn), F32)],
        compiler_params=_cparams(("parallel", "arbitrary")),
        name="ffn_down",
    )(a, w, g, x, *([gn] * emit_next))
    return (outs[0], outs[1]) if emit_next else (outs[0], None)


def _out_proj_kernel(a_ref, b_ref, c_ref, d_ref, w_ref, g_ref, x_ref, gn_ref, o_ref, hn_ref,
                     lhs_ref, acc_ref):
    for p, part in enumerate((a_ref, b_ref, c_ref, d_ref)):
        lhs_ref[:, p * GROUP_WIDTH:(p + 1) * GROUP_WIDTH] = part[...]
    n_half, half, _ = acc_ref.shape
    for b in range(n_half):
        acc_ref[b] = jnp.dot(lhs_ref[b * half:(b + 1) * half, :], w_ref[...],
                             preferred_element_type=F32)
    for b in range(n_half):
        for c in range(half // ROW_CHUNK):
            rows = slice(b * half + c * ROW_CHUNK, b * half + (c + 1) * ROW_CHUNK)
            o = x_ref[rows, :] + _rms(acc_ref[b, c * ROW_CHUNK:(c + 1) * ROW_CHUNK, :]) * g_ref[...]
            o_ref[rows, :] = o
            hn_ref[rows, :] = (_rms(o) * gn_ref[...]).astype(hn_ref.dtype)


def _out_proj(parts, w, g, x, gn, *, tm):
    m, d = x.shape
    part_spec = pl.BlockSpec((tm, GROUP_WIDTH), lambda i: (i, 0))
    vec = pl.BlockSpec((1, d), lambda i: (0, 0))
    row_tile = pl.BlockSpec((tm, d), lambda i: (i, 0))
    return pl.pallas_call(
        _out_proj_kernel,
        grid=(m // tm,),
        in_specs=[part_spec, part_spec, part_spec, part_spec,
                  pl.BlockSpec((4 * GROUP_WIDTH, d), lambda i: (0, 0)),
                  vec, row_tile, vec],
        out_specs=[row_tile, row_tile],
        out_shape=[jax.ShapeDtypeStruct((m, d), F32), jax.ShapeDtypeStruct((m, d), BF16)],
        scratch_shapes=[pltpu.VMEM((tm, 4 * GROUP_WIDTH), BF16),
                        pltpu.VMEM((2, tm // 2, d), F32)],
        compiler_params=_cparams(("parallel",)),
        name="out_proj",
    )(*parts, w, g, x, gn)


def _rope_tables(pos_lo, pos_hi):
    half = DIFF_QK_DIM // 2
    inv = ROPE_THETA ** (-np.arange(half, dtype=np.float64) / half)
    lane = np.arange(LANES)
    pos = np.where(lane[None, :] < DIFF_QK_DIM, pos_lo[:, None], pos_hi[:, None]).astype(np.float64)
    ang = pos * inv[lane % half][None, :]
    sign = np.where((lane % DIFF_QK_DIM) < half, -1.0, 1.0)
    return (jnp.asarray(np.cos(ang), dtype=F32),
            jnp.asarray(np.sin(ang) * sign[None, :], dtype=F32))


def _rope(x, cos, sin_signed):
    lane = lax.broadcasted_iota(jnp.int32, x.shape, 1)
    first_half = (lane & (DIFF_QK_DIM // 2)) == 0
    partner = jnp.where(first_half, pltpu.roll(x, LANES - DIFF_QK_DIM // 2, 1),
                        pltpu.roll(x, DIFF_QK_DIM // 2, 1))
    return x * cos + partner * sin_signed


LOG2E = math.log2(math.e)


ATTN_OFFSET_SLACK = 64.0
SUBLANES = 8
BF16_ROWS = 16


def _scores(krot_ref, qs_ref, j, tk):
    k = krot_ref[j * tk:(j + 1) * tk, :]
    return lax.dot_general(k, qs_ref[...], (((1,), (1,)), ((), ())), preferred_element_type=F32)


def _attend_online(qs_ref, krot_ref, vt_ref, acc_ref, *, tk):
    n_tiles = krot_ref.shape[0] // tk
    m = l = None
    s_next = _scores(krot_ref, qs_ref, 0, tk)
    for j in range(n_tiles):
        s = s_next
        if j + 1 < n_tiles:
            s_next = _scores(krot_ref, qs_ref, j + 1, tk)
        m_cur = jnp.max(s, axis=0, keepdims=True)
        m_next = m_cur if m is None else jnp.maximum(m, m_cur)
        p = jnp.exp2(s - m_next)
        pv = jnp.dot(vt_ref[:, j * tk:(j + 1) * tk], p.astype(BF16), preferred_element_type=F32)
        if m is None:
            l = jnp.sum(p, axis=0, keepdims=True)
            acc_ref[...] = pv
        else:
            alpha = jnp.exp2(m - m_next)
            l = alpha * l + jnp.sum(p, axis=0, keepdims=True)
            acc_ref[...] = alpha * acc_ref[...] + pv
        m = m_next
    return l


def _attend_fixed(qs_ref, krot_ref, vt_ref, acc_ref, offset, *, tk):
    n_tiles = krot_ref.shape[0] // tk
    l = None
    s_next = _scores(krot_ref, qs_ref, 0, tk)
    for j in range(n_tiles):
        s = s_next
        if j + 1 < n_tiles:
            s_next = _scores(krot_ref, qs_ref, j + 1, tk)
        p = jnp.exp2(s - offset)
        pv = jnp.dot(vt_ref[:, j * tk:(j + 1) * tk], p.astype(BF16), preferred_element_type=F32)
        if l is None:
            l = jnp.sum(p, axis=0, keepdims=True)
            acc_ref[...] = pv
        else:
            l = l + jnp.sum(p, axis=0, keepdims=True)
            acc_ref[...] += pv
    return l


def _attend(qs_ref, q_norm2_max, krot_ref, vt_ref, kstat_ref, acc_ref, finish, *, tk):
    worst_exponent = jnp.sqrt(q_norm2_max) * kstat_ref[1:2, 0:1]
    use_fixed = jnp.max(worst_exponent) <= ATTN_OFFSET_SLACK

    @pl.when(use_fixed)
    def _():
        qf = qs_ref[...].astype(F32)
        ones = jnp.ones((BF16_ROWS, qf.shape[1]), BF16)
        q_norm2 = lax.dot_general(ones, (qf * qf).astype(BF16), (((1,), (1,)), ((), ())),
                                  preferred_element_type=F32)
        offset = jnp.sqrt(q_norm2[0:1]) * kstat_ref[0:1, 0:1]
        finish(_attend_fixed(qs_ref, krot_ref, vt_ref, acc_ref, offset, tk=tk))

    @pl.when(jnp.logical_not(use_fixed))
    def _():
        finish(_attend_online(qs_ref, krot_ref, vt_ref, acc_ref, tk=tk))


def _prepare_keys_values(k_ref, v_ref, cos_ref, sin_ref, krot_ref, vt_ref, kstat_ref, gain):
    seq = k_ref.shape[0]

    def body(c, carry):
        k_sum, k_max2 = carry
        r0 = pl.multiple_of(c * ROW_CHUNK, ROW_CHUNK)
        k = k_ref[pl.ds(r0, ROW_CHUNK), :].astype(F32)
        if gain is not None:
            k = _rms(k) * gain
        kb = _rope(k, cos_ref[pl.ds(r0, ROW_CHUNK), :], sin_ref[pl.ds(r0, ROW_CHUNK), :]).astype(BF16)
        krot_ref[pl.ds(r0, ROW_CHUNK), :] = kb
        vt_ref[:, pl.ds(r0, ROW_CHUNK)] = v_ref[pl.ds(r0, ROW_CHUNK), :].astype(F32).T.astype(BF16)
        kf = kb.astype(F32)
        norm2 = jnp.sum(kf * kf, axis=1, keepdims=True)
        return (k_sum + jnp.sum(kf, axis=0, keepdims=True),
                jnp.maximum(k_max2, jnp.max(norm2, axis=0, keepdims=True)))

    k_sum, k_max2 = lax.fori_loop(0, seq // ROW_CHUNK, body,
                                  (jnp.zeros((1, k_ref.shape[1]), F32), jnp.zeros((1, 1), F32)))
    k_bar = k_sum * (1.0 / seq)
    k_bar_norm = jnp.sqrt(jnp.sum(k_bar * k_bar, axis=1, keepdims=True))
    k_max = jnp.sqrt(k_max2) * 1.01
    kstat_ref[0:1, :] = jnp.broadcast_to(k_max, (1, kstat_ref.shape[1]))
    kstat_ref[1:2, :] = jnp.broadcast_to(k_max + k_bar_norm, (1, kstat_ref.shape[1]))


def _diff_attn_kernel(q_ref, k_ref, v_ref, cq_ref, sq_ref, ck_ref, sk_ref, lam_ref, g_ref, o_ref,
                      krot_ref, vt_ref, kstat_ref, qs_ref, acc_ref, *, tq, tk, lam_init):
    @pl.when(pl.program_id(1) == 0)
    def _():
        _prepare_keys_values(k_ref, v_ref, ck_ref, sk_ref, krot_ref, vt_ref, kstat_ref, None)

    q = _rope(q_ref[...].astype(F32), cq_ref[...], sq_ref[...]) * (DIFF_QK_DIM ** -0.5 * LOG2E)
    q_norm2_max = jnp.max(jnp.sum(q * q, axis=1, keepdims=True), axis=0, keepdims=True)
    lane = lax.broadcasted_iota(jnp.int32, q.shape, 1)
    qs_ref[0:tq, :] = jnp.where(lane < DIFF_QK_DIM, q, 0.0).astype(BF16)
    qs_ref[tq:2 * tq, :] = jnp.where(lane >= DIFF_QK_DIM, q, 0.0).astype(BF16)

    def finish(l):
        lp = lam_ref[...]
        lam = (jnp.exp(jnp.sum(lp[0:1] * lp[1:2], keepdims=True))
               - jnp.exp(jnp.sum(lp[2:3] * lp[3:4], keepdims=True)) + lam_init)
        ot = (acc_ref[:, 0:tq] / l[:, 0:tq] - lam * (acc_ref[:, tq:2 * tq] / l[:, tq:2 * tq]))
        ms = jnp.mean(ot * ot, axis=0, keepdims=True)
        ot = ot * lax.rsqrt(ms + EPS) * (g_ref[...] * (1.0 - lam_init))
        o_ref[...] = ot.T.astype(o_ref.dtype)

    _attend(qs_ref, q_norm2_max, krot_ref, vt_ref, kstat_ref, acc_ref, finish, tk=tk)


def _gqa_attn_kernel(q_ref, k_ref, v_ref, cq_ref, sq_ref, ck_ref, sk_ref, qn_ref, kn_ref, o_ref,
                     krot_ref, vt_ref, kstat_ref, qs_ref, acc_ref, *, tq, tk):
    @pl.when(pl.program_id(1) == 0)
    def _():
        _prepare_keys_values(k_ref, v_ref, ck_ref, sk_ref, krot_ref, vt_ref, kstat_ref, kn_ref[...])

    q_norm2_max = jnp.zeros((1, 1), F32)
    for r in range(2):
        q = q_ref[:, r * HEAD_DIM:(r + 1) * HEAD_DIM].astype(F32)
        q = _rope(_rms(q) * qn_ref[...], cq_ref[...], sq_ref[...]) * (HEAD_DIM ** -0.5 * LOG2E)
        q_norm2_max = jnp.maximum(
            q_norm2_max, jnp.max(jnp.sum(q * q, axis=1, keepdims=True), axis=0, keepdims=True))
        qs_ref[r * tq:(r + 1) * tq, :] = q.astype(BF16)

    def finish(l):
        for r in range(2):
            ot = acc_ref[:, r * tq:(r + 1) * tq] / l[:, r * tq:(r + 1) * tq]
            o_ref[:, r * HEAD_DIM:(r + 1) * HEAD_DIM] = ot.T.astype(o_ref.dtype)

    _attend(qs_ref, q_norm2_max, krot_ref, vt_ref, kstat_ref, acc_ref, finish, tk=tk)


def _flash_call(kernel, proj3, tables, params, *, n_groups, q_width, q_col, k_col, v_col,
                out_width, tq, name):
    batch, seq, _ = proj3.shape
    cos_t, sin_t = tables
    qb, kb, vb = q_col // q_width, k_col // HEAD_DIM, v_col // HEAD_DIM

    def grp(i):
        return i // n_groups, i % n_groups

    in_specs = [
        pl.BlockSpec((None, tq, q_width), lambda i, j: (grp(i)[0], j, qb + grp(i)[1])),
        pl.BlockSpec((None, seq, HEAD_DIM), lambda i, j: (grp(i)[0], 0, kb + grp(i)[1])),
        pl.BlockSpec((None, seq, HEAD_DIM), lambda i, j: (grp(i)[0], 0, vb + grp(i)[1])),
        pl.BlockSpec((tq, LANES), lambda i, j: (j, 0)),
        pl.BlockSpec((tq, LANES), lambda i, j: (j, 0)),
        pl.BlockSpec((seq, LANES), lambda i, j: (0, 0)),
        pl.BlockSpec((seq, LANES), lambda i, j: (0, 0)),
    ] + [pl.BlockSpec(p.shape, lambda i, j: (0, 0)) for p in params]
    return pl.pallas_call(
        kernel,
        grid=(batch * n_groups, seq // tq),
        in_specs=in_specs,
        out_specs=pl.BlockSpec((None, tq, out_width), lambda i, j: (grp(i)[0], j, grp(i)[1])),
        out_shape=jax.ShapeDtypeStruct((batch, seq, GROUP_WIDTH), BF16),
        scratch_shapes=[pltpu.VMEM((seq, HEAD_DIM), BF16),
                        pltpu.VMEM((HEAD_DIM, seq), BF16),
                        pltpu.VMEM((SUBLANES, LANES), F32),
                        pltpu.VMEM((2 * tq, HEAD_DIM), BF16),
                        pltpu.VMEM((HEAD_DIM, 2 * tq), F32)],
        compiler_params=_cparams(("parallel", "arbitrary")),
        name=name,
    )(proj3, proj3, proj3, cos_t, sin_t, cos_t, sin_t, *params)


def _conv_kernel(a_ref, g_ref, dw_ref, dwb_ref, lng_ref, lnb_ref, pw_ref, pwb_ref, o_ref,
                 u_ref, acc_ref, y_ref, *, tm):
    seq = a_ref.shape[0]
    t = pl.program_id(1)

    @pl.when(t == 0)
    def _():
        zeros = jnp.zeros((CONV_HALO, u_ref.shape[1]), F32)
        u_ref[0:CONV_HALO, :] = zeros
        u_ref[CONV_HALO + seq:2 * CONV_HALO + seq, :] = zeros

        def body(c, _):
            r0 = pl.multiple_of(c * ROW_CHUNK, ROW_CHUNK)
            a = a_ref[pl.ds(r0, ROW_CHUNK), :].astype(F32)
            g = g_ref[pl.ds(r0, ROW_CHUNK), :].astype(F32)
            u_ref[pl.ds(CONV_HALO + r0, ROW_CHUNK), :] = a * jax.nn.sigmoid(g)
            return 0

        lax.fori_loop(0, seq // ROW_CHUNK, body, 0)

    pad = CONV_WIDTH // 2
    sub = 8
    n_ch = u_ref.shape[1]

    def row_block(rb, _):
        base = pl.multiple_of(t * tm + rb * ROW_CHUNK, ROW_CHUNK)
        for cg in range(n_ch // LANES):
            cols = slice(cg * LANES, (cg + 1) * LANES)
            win = u_ref[pl.ds(base, ROW_CHUNK + 2 * CONV_HALO), cols]
            acc = jnp.zeros((ROW_CHUNK, LANES), F32) + dwb_ref[:, cols]
            for r in range(sub):
                shifted = win if r == 0 else pltpu.roll(win, win.shape[0] - r, 0)
                for a in range(2 * CONV_HALO // sub):
                    k = sub * a + r - (CONV_HALO - pad)
                    if 0 <= k < CONV_WIDTH:
                        acc = acc + shifted[sub * a:sub * a + ROW_CHUNK, :] * dw_ref[k:k + 1, cols]
            acc_ref[:, cols] = acc
        acc = acc_ref[...]
        mu = jnp.mean(acc, axis=-1, keepdims=True)
        cen = acc - mu
        var = jnp.mean(cen * cen, axis=-1, keepdims=True)
        y = cen * lax.rsqrt(var + EPS) * lng_ref[...] + lnb_ref[...]
        y_ref[pl.ds(pl.multiple_of(rb * ROW_CHUNK, ROW_CHUNK), ROW_CHUNK), :] = (
            y * jax.nn.sigmoid(y)).astype(BF16)
        return 0

    lax.fori_loop(0, tm // ROW_CHUNK, row_block, 0)
    o_ref[...] = (jnp.dot(y_ref[...], pw_ref[...], preferred_element_type=F32)
                  + pwb_ref[...]).astype(o_ref.dtype)


def _conv_mixer(proj3, dw, dwb, lng, lnb, pw, pwb, *, tm):
    batch, seq, _ = proj3.shape
    ch = GROUP_WIDTH
    vec = pl.BlockSpec((1, ch), lambda b, t: (0, 0))
    return pl.pallas_call(
        functools.partial(_conv_kernel, tm=tm),
        grid=(batch, seq // tm),
        in_specs=[pl.BlockSpec((None, seq, ch), lambda b, t: (b, 0, COL_B_A // ch)),
                  pl.BlockSpec((None, seq, ch), lambda b, t: (b, 0, COL_B_G // ch)),
                  pl.BlockSpec((CONV_WIDTH, ch), lambda b, t: (0, 0)),
                  vec, vec, vec,
                  pl.BlockSpec((ch, ch), lambda b, t: (0, 0)),
                  vec],
        out_specs=pl.BlockSpec((None, tm, ch), lambda b, t: (b, t, 0)),
        out_shape=jax.ShapeDtypeStruct((batch, seq, ch), BF16),
        scratch_shapes=[pltpu.VMEM((seq + 2 * CONV_HALO, ch), F32),
                        pltpu.VMEM((ROW_CHUNK, ch), F32),
                        pltpu.VMEM((tm, ch), BF16)],
        compiler_params=_cparams(("parallel", "arbitrary")),
        name="conv_mixer",
    )(proj3, proj3, dw, dwb, lng, lnb, pw, pwb)


def _na_geometry(seq):
    rows = seq // GRID_W
    qr_blk = Q_BLOCK // GRID_W
    mask_add = np.zeros((NA_CLASSES, Q_BLOCK, NA_BAND), np.float32)
    q_local = np.arange(Q_BLOCK)
    p = np.arange(NA_BAND)
    for c, blk in enumerate(_na_class_blocks(seq)):
        bs = int(np.clip(blk * qr_blk - NA_KH // 2, 0, rows - NA_BAND_ROWS))
        q_row = (blk * qr_blk + q_local // GRID_W)[:, None]
        q_col = (q_local % GRID_W)[:, None]
        k_row = (bs + p // GRID_W)[None, :]
        k_col = (p % GRID_W)[None, :]
        win_r = np.clip(q_row - NA_KH // 2, 0, rows - NA_KH)
        win_c = np.clip(q_col - NA_KW // 2, 0, GRID_W - NA_KW)
        ok = (k_row >= win_r) & (k_row < win_r + NA_KH) & (k_col >= win_c) & (k_col < win_c + NA_KW)
        mask_add[c] = np.where(ok, 0.0, -1e30)
    return jnp.asarray(mask_add)


def _na_class_blocks(seq):
    n_blk = seq // Q_BLOCK
    return (0, 1, 2, n_blk - 2, n_blk - 1)


def _rpb_tiles_kernel(rpb_ref, mask_ref, o_ref, t2_ref, *, seq):
    h = pl.program_id(0)
    n_c = 2 * NA_KW - 1
    n_r = 2 * NA_KH - 1
    shape = (GRID_W, LANES)
    lane = lax.broadcasted_iota(jnp.int32, shape, 1)
    q_col = lax.broadcasted_iota(jnp.int32, shape, 0)
    ic = jnp.clip((lane & (GRID_W - 1)) - q_col + (NA_KW - 1), 0, n_c - 1)
    left = lane < GRID_W
    base = h * (n_r * n_c)
    for i in range(NA_T2):
        i_l = min(max(i - 1, 0), n_r - 1)
        i_r = min(max(i, 0), n_r - 1)

        def body(j, acc, i_l=i_l, i_r=i_r):
            coef = jnp.where(left, rpb_ref[base + i_l * n_c + j], rpb_ref[base + i_r * n_c + j])
            return acc + jnp.where(ic == j, coef, 0.0)

        t2_ref[i] = lax.fori_loop(0, n_c, body, jnp.zeros(shape, F32))

    rows = seq // GRID_W
    qr_blk = Q_BLOCK // GRID_W
    for c, blk in enumerate(_na_class_blocks(seq)):
        bs = min(max(blk * qr_blk - NA_KH // 2, 0), rows - NA_BAND_ROWS)
        for a in range(qr_blk):
            for r2 in range(NA_BAND_ROWS // 2):
                i1 = bs + 2 * r2 - (blk * qr_blk + a) + (NA_KH - 1)
                idx = min(max(i1, -1), NA_T2 - 2) + 1
                rs = slice(a * GRID_W, (a + 1) * GRID_W)
                cs = slice(r2 * LANES, (r2 + 1) * LANES)
                o_ref[c, rs, cs] = t2_ref[idx] + mask_ref[c, rs, cs]


def _rpb_tiles(rpb, mask_add, seq):
    heads = rpb.shape[0]
    return pl.pallas_call(
        functools.partial(_rpb_tiles_kernel, seq=seq),
        grid=(heads,),
        in_specs=[pl.BlockSpec(memory_space=pltpu.SMEM),
                  pl.BlockSpec(mask_add.shape, lambda h: (0, 0, 0))],
        out_specs=pl.BlockSpec((None,) + mask_add.shape, lambda h: (h, 0, 0, 0)),
        out_shape=jax.ShapeDtypeStruct((heads,) + mask_add.shape, F32),
        scratch_shapes=[pltpu.VMEM((NA_T2, GRID_W, LANES), F32)],
        compiler_params=_cparams(("parallel",)),
        name="na_rpb_tiles",
    )(rpb.reshape(-1), mask_add)


def _na_kernel(q_ref, k_ref, v_ref, bias_ref, o_ref):
    seq = q_ref.shape[0]
    rows = seq // GRID_W
    qr_blk = Q_BLOCK // GRID_W
    n_blk = seq // Q_BLOCK

    classes = _na_class_blocks(seq)

    def band(blk):
        k0 = min(max(blk * qr_blk - NA_KH // 2, 0), rows - NA_BAND_ROWS) * GRID_W
        return slice(k0, k0 + NA_BAND)

    def scores(blk):
        q = q_ref[blk * Q_BLOCK:(blk + 1) * Q_BLOCK, :]
        return lax.dot_general(q, k_ref[band(blk), :], (((1,), (1,)), ((), ())),
                               preferred_element_type=F32)

    pending = [scores(b) for b in range(NA_LOOKAHEAD)]
    for blk in range(n_blk):
        s = pending.pop(0)
        if blk + NA_LOOKAHEAD < n_blk:
            pending.append(scores(blk + NA_LOOKAHEAD))
        cls = classes.index(blk) if blk in classes else 2
        s = s * (HEAD_DIM ** -0.5) + bias_ref[cls]
        m = jnp.max(s, axis=1, keepdims=True)
        p = jnp.exp(s - m)
        l = jnp.sum(p, axis=1, keepdims=True)
        o = jnp.dot(p.astype(BF16), v_ref[band(blk), :], preferred_element_type=F32) / l
        o_ref[blk * Q_BLOCK:(blk + 1) * Q_BLOCK, :] = o.astype(o_ref.dtype)


def _na_mixer(proj3, bias):
    batch, seq, _ = proj3.shape
    heads = GROUP_HEADS

    def head_spec(col):
        return pl.BlockSpec((None, seq, HEAD_DIM), lambda b, h: (b, 0, col // HEAD_DIM + h))

    return pl.pallas_call(
        _na_kernel,
        grid=(batch, heads),
        in_specs=[head_spec(COL_D_Q), head_spec(COL_D_K), head_spec(COL_D_V),
                  pl.BlockSpec((None,) + bias.shape[1:], lambda b, h: (h, 0, 0, 0))],
        out_specs=pl.BlockSpec((None, seq, HEAD_DIM), lambda b, h: (b, 0, h)),
        out_shape=jax.ShapeDtypeStruct((batch, seq, GROUP_WIDTH), BF16),
        compiler_params=_cparams(("parallel", "parallel")),
        name="na_mixer",
    )(proj3, proj3, proj3, bias)


def kernel(x, norm_mix_pre, norm_mix_post, norm_ffn_pre, norm_ffn_post, w_in, w_out, diff_lambda, diff_subln, conv_dw, conv_dw_b, conv_ln_g, conv_ln_b, conv_pw, conv_pw_b, gqa_q_norm, gqa_k_norm, na_rpb, ffn_gate, ffn_up, ffn_down):
    batch, seq, d = x.shape
    depth = w_in.shape[0]
    tokens = batch * seq

    t = np.arange(seq)
    diff_tables = _rope_tables(t, t)
    axial_tables = _rope_tables(t // GRID_W, t % GRID_W)
    mask_add = _na_geometry(seq)

    def row(v):
        return v.reshape(1, -1)

    xf = x.reshape(tokens, d)
    h = _rmsnorm(xf, row(norm_mix_pre[0]), tm=1024)
    for l in range(depth):
        lam_init = 0.8 - 0.6 * math.exp(-0.3 * l)
        proj = _in_proj(h, w_in, l, tm=1024, tn=1024)
        proj3 = proj.reshape(batch, seq, IN_COLS)

        out_a = _flash_call(
            functools.partial(_diff_attn_kernel, tq=ATTN_TQ, tk=ATTN_TK, lam_init=lam_init),
            proj3, diff_tables, (diff_lambda[l], diff_subln[l].reshape(-1, 1)),
            n_groups=GROUP_HEADS, q_width=HEAD_DIM, q_col=COL_A_Q, k_col=COL_A_K, v_col=COL_A_V,
            out_width=HEAD_DIM, tq=ATTN_TQ, name="diff_attn")
        out_b = _conv_mixer(proj3, conv_dw[l], row(conv_dw_b[l]), row(conv_ln_g[l]), row(conv_ln_b[l]),
                            conv_pw[l].astype(BF16), row(conv_pw_b[l]), tm=512)
        out_c = _flash_call(
            functools.partial(_gqa_attn_kernel, tq=ATTN_TQ, tk=ATTN_TK),
            proj3, axial_tables, (row(gqa_q_norm[l]), row(gqa_k_norm[l])),
            n_groups=GQA_KV_HEADS, q_width=2 * HEAD_DIM, q_col=COL_C_Q, k_col=COL_C_K, v_col=COL_C_V,
            out_width=2 * HEAD_DIM, tq=ATTN_TQ, name="gqa_attn")
        out_d = _na_mixer(proj3, _rpb_tiles(na_rpb[l], mask_add, seq))

        parts = [o.reshape(tokens, GROUP_WIDTH) for o in (out_a, out_b, out_c, out_d)]
        xf, h = _out_proj(parts, w_out[l].astype(BF16), row(norm_mix_post[l]), xf,
                          row(norm_ffn_pre[l]), tm=512)

        act = _ffn_up(h, ffn_gate, ffn_up, l, tm=1024, tn=512)
        next_gain = row(norm_mix_pre[l + 1]) if l + 1 < depth else None
        xf, h = _ffn_down(act, ffn_down[l].astype(BF16), row(norm_ffn_post[l]), xf, next_gain,
                          tm=512, tn=512)
    return xf.reshape(batch, seq, d)
```

```python
import functools
import math

import numpy as np
import jax
import jax.numpy as jnp
from jax import lax
from jax.experimental import pallas as pl
from jax.experimental.pallas import tpu as pltpu

D_MODEL = 2048
HEAD_DIM = 128
GROUP_HEADS = 4
GROUP_WIDTH = GROUP_HEADS * HEAD_DIM
DIFF_QK_DIM = HEAD_DIM // 2
CONV_WIDTH = 31
GQA_KV_HEADS = 2
NA_KH = 8
NA_KW = 16
GRID_W = 64
Q_BLOCK = 128
ROPE_THETA = 10000.0
EPS = 1e-6

COL_A_Q, COL_A_K, COL_A_V = 0, 512, 1024
COL_B_A, COL_B_G = 1536, 2048
COL_C_Q, COL_C_K, COL_C_V = 2560, 3072, 3328
COL_D_Q, COL_D_K, COL_D_V = 3584, 4096, 4608
IN_COLS = 5120

LANES = 128
VMEM_LIMIT = 56 * 1024 * 1024
VMEM_LIMIT_LARGE = 60 * 1024 * 1024
ROW_CHUNK = 128
ATTN_TQ = 512
ATTN_TK = 512

NA_BAND_ROWS = 10
NA_BAND = NA_BAND_ROWS * GRID_W
NA_CLASSES = 5
NA_T2 = 16
NA_LOOKAHEAD = 2
CONV_HALO = 16

F32 = jnp.float32
BF16 = jnp.bfloat16


def _cparams(semantics, vmem_limit=VMEM_LIMIT):
    return pltpu.CompilerParams(dimension_semantics=semantics,
                                vmem_limit_bytes=vmem_limit)


def _rms(x):
    return x * lax.rsqrt(jnp.mean(x * x, axis=-1, keepdims=True) + EPS)


def _rmsnorm_kernel(x_ref, g_ref, h_ref):
    def body(c, _):
        r0 = pl.multiple_of(c * ROW_CHUNK, ROW_CHUNK)
        h_ref[pl.ds(r0, ROW_CHUNK), :] = (_rms(x_ref[pl.ds(r0, ROW_CHUNK), :]) * g_ref[...]).astype(h_ref.dtype)
        return 0

    lax.fori_loop(0, x_ref.shape[0] // ROW_CHUNK, body, 0)


def _rmsnorm(x, g, *, tm):
    m, d = x.shape
    return pl.pallas_call(
        _rmsnorm_kernel,
        grid=(m // tm,),
        in_specs=[pl.BlockSpec((tm, d), lambda i: (i, 0)),
                  pl.BlockSpec((1, d), lambda i: (0, 0))],
        out_specs=pl.BlockSpec((tm, d), lambda i: (i, 0)),
        out_shape=jax.ShapeDtypeStruct((m, d), BF16),
        compiler_params=_cparams(("parallel",)),
        name="rmsnorm",
    )(x, g)


def _cast_weight_tile(w_ref, wb_ref):
    def body(c, _):
        r0 = pl.multiple_of(c * ROW_CHUNK, ROW_CHUNK)
        wb_ref[pl.ds(r0, ROW_CHUNK), :] = w_ref[pl.ds(r0, ROW_CHUNK), :].astype(wb_ref.dtype)
        return 0

    lax.fori_loop(0, w_ref.shape[0] // ROW_CHUNK, body, 0)


def _in_proj_kernel(h_ref, w_ref, o_ref, wb_ref):
    @pl.when(pl.program_id(1) == 0)
    def _():
        _cast_weight_tile(w_ref, wb_ref)

    o_ref[...] = jnp.dot(h_ref[...], wb_ref[...], preferred_element_type=F32).astype(o_ref.dtype)


def _in_proj(h, w, layer, *, tm, tn):
    m, d = h.shape
    n = w.shape[2]
    return pl.pallas_call(
        _in_proj_kernel,
        grid=(n // tn, m // tm),
        in_specs=[pl.BlockSpec((tm, d), lambda j, i: (i, 0)),
                  pl.BlockSpec((None, d, tn), lambda j, i: (layer, 0, j))],
        out_specs=pl.BlockSpec((tm, tn), lambda j, i: (i, j)),
        out_shape=jax.ShapeDtypeStruct((m, n), BF16),
        scratch_shapes=[pltpu.VMEM((d, tn), BF16)],
        compiler_params=_cparams(("parallel", "arbitrary")),
        name="in_proj",
    )(h, w)


def _ffn_up_kernel(h_ref, wg_ref, wu_ref, wd_ref, o_ref, wdb_ref, wgb_ref, wub_ref):
    @pl.when(pl.program_id(1) == 0)
    def _():
        _cast_weight_tile(wg_ref, wgb_ref)
        _cast_weight_tile(wu_ref, wub_ref)

    wdb_ref[...] = wd_ref[...].astype(wdb_ref.dtype)

    h = h_ref[...]
    gate = jnp.dot(h, wgb_ref[...], preferred_element_type=F32)
    up = jnp.dot(h, wub_ref[...], preferred_element_type=F32)
    o_ref[...] = (gate * jax.nn.sigmoid(gate) * up).astype(o_ref.dtype)


def _ffn_up(h, wg, wu, wd, layer, *, tm, tn):
    m, d = h.shape
    n = wg.shape[2]
    n_i = m // tm
    slab = wd.shape[1] // ((n // tn) * n_i)
    assert slab * (n // tn) * n_i == wd.shape[1] and slab % BF16_ROWS == 0
    w_spec = pl.BlockSpec((None, d, tn), lambda j, i: (layer, 0, j))
    return pl.pallas_call(
        _ffn_up_kernel,
        grid=(n // tn, n_i),
        in_specs=[pl.BlockSpec((tm, d), lambda j, i: (i, 0)), w_spec, w_spec,
                  pl.BlockSpec((None, slab, wd.shape[2]), lambda j, i: (layer, j * n_i + i, 0))],
        out_specs=[pl.BlockSpec((tm, tn), lambda j, i: (i, j)),
                   pl.BlockSpec((slab, wd.shape[2]), lambda j, i: (j * n_i + i, 0))],
        out_shape=[jax.ShapeDtypeStruct((m, n), BF16),
                   jax.ShapeDtypeStruct(wd.shape[1:], BF16)],
        scratch_shapes=[pltpu.VMEM((d, tn), BF16), pltpu.VMEM((d, tn), BF16)],
        compiler_params=_cparams(("parallel", "arbitrary"), VMEM_LIMIT_LARGE),
        name="ffn_up",
    )(h, wg, wu, wd)


def _ffn_down_kernel(*refs, emit_next):
    if emit_next:
        a_ref, w_ref, g_ref, x_ref, gn_ref, o_ref, hn_ref = refs
    else:
        a_ref, w_ref, g_ref, x_ref, o_ref = refs
        gn_ref = hn_ref = None
    k = pl.program_id(1)
    prod = jnp.dot(a_ref[...], w_ref[...], preferred_element_type=F32)

    @pl.when(k == 0)
    def _():
        o_ref[...] = prod

    @pl.when(k > 0)
    def _():
        o_ref[...] += prod

    @pl.when(k == pl.num_programs(1) - 1)
    def _():
        def body(c, _):
            rows = pl.ds(pl.multiple_of(c * ROW_CHUNK, ROW_CHUNK), ROW_CHUNK)
            o = x_ref[rows, :] + _rms(o_ref[rows, :]) * g_ref[...]
            o_ref[rows, :] = o
            if hn_ref is not None:
                hn_ref[rows, :] = (_rms(o) * gn_ref[...]).astype(hn_ref.dtype)
            return 0

        lax.fori_loop(0, o_ref.shape[0] // ROW_CHUNK, body, 0)


def _ffn_down(a, w, g, x, gn, *, tm, tk):
    m, kdim = a.shape
    d = w.shape[1]
    emit_next = gn is not None
    vec = pl.BlockSpec((1, d), lambda i, k: (0, 0))
    row_tile = pl.BlockSpec((tm, d), lambda i, k: (i, 0))
    x_tile = pl.BlockSpec((tm, d), lambda i, k: (i, 0), pipeline_mode=pl.Buffered(1))
    outs = pl.pallas_call(
        functools.partial(_ffn_down_kernel, emit_next=emit_next),
        grid=(m // tm, kdim // tk),
        in_specs=[pl.BlockSpec((tm, tk), lambda i, k: (i, k)),
                  pl.BlockSpec((tk, d), lambda i, k: (k, 0)),
                  vec, x_tile] + [vec] * emit_next,
        out_specs=[row_tile] + [row_tile] * emit_next,
        out_shape=[jax.ShapeDtypeStruct((m, d), F32)] + [jax.ShapeDtypeStruct((m, d), BF16)] * emit_next,
        compiler_params=_cparams(("parallel", "arbitrary"), VMEM_LIMIT_LARGE),
        name="ffn_down",
    )(a, w, g, x, *([gn] * emit_next))
    return (outs[0], outs[1]) if emit_next else (outs[0], None)


def _out_proj_kernel(a_ref, b_ref, c_ref, d_ref, w_ref, g_ref, x_ref, gn_ref, o_ref, hn_ref,
                     lhs_ref, acc_ref):
    for p, part in enumerate((a_ref, b_ref, c_ref, d_ref)):
        lhs_ref[:, p * GROUP_WIDTH:(p + 1) * GROUP_WIDTH] = part[...]
    n_half, half, _ = acc_ref.shape
    for b in range(n_half):
        acc_ref[b] = jnp.dot(lhs_ref[b * half:(b + 1) * half, :], w_ref[...],
                             preferred_element_type=F32)
    for b in range(n_half):
        for c in range(half // ROW_CHUNK):
            rows = slice(b * half + c * ROW_CHUNK, b * half + (c + 1) * ROW_CHUNK)
            o = x_ref[rows, :] + _rms(acc_ref[b, c * ROW_CHUNK:(c + 1) * ROW_CHUNK, :]) * g_ref[...]
            o_ref[rows, :] = o
            hn_ref[rows, :] = (_rms(o) * gn_ref[...]).astype(hn_ref.dtype)


def _out_proj(parts, w, g, x, gn, *, tm):
    m, d = x.shape
    part_spec = pl.BlockSpec((tm, GROUP_WIDTH), lambda i: (i, 0))
    vec = pl.BlockSpec((1, d), lambda i: (0, 0))
    row_tile = pl.BlockSpec((tm, d), lambda i: (i, 0))
    return pl.pallas_call(
        _out_proj_kernel,
        grid=(m // tm,),
        in_specs=[part_spec, part_spec, part_spec, part_spec,
                  pl.BlockSpec((4 * GROUP_WIDTH, d), lambda i: (0, 0)),
                  vec, row_tile, vec],
        out_specs=[row_tile, row_tile],
        out_shape=[jax.ShapeDtypeStruct((m, d), F32), jax.ShapeDtypeStruct((m, d), BF16)],
        scratch_shapes=[pltpu.VMEM((tm, 4 * GROUP_WIDTH), BF16),
                        pltpu.VMEM((2, tm // 2, d), F32)],
        compiler_params=_cparams(("parallel",)),
        name="out_proj",
    )(*parts, w, g, x, gn)


def _rope_tables(pos_lo, pos_hi):
    half = DIFF_QK_DIM // 2
    inv = ROPE_THETA ** (-np.arange(half, dtype=np.float64) / half)
    lane = np.arange(LANES)
    pos = np.where(lane[None, :] < DIFF_QK_DIM, pos_lo[:, None], pos_hi[:, None]).astype(np.float64)
    ang = pos * inv[lane % half][None, :]
    sign = np.where((lane % DIFF_QK_DIM) < half, -1.0, 1.0)
    return (jnp.asarray(np.cos(ang), dtype=F32),
            jnp.asarray(np.sin(ang) * sign[None, :], dtype=F32))


def _rope(x, cos, sin_signed):
    lane = lax.broadcasted_iota(jnp.int32, x.shape, 1)
    first_half = (lane & (DIFF_QK_DIM // 2)) == 0
    partner = jnp.where(first_half, pltpu.roll(x, LANES - DIFF_QK_DIM // 2, 1),
                        pltpu.roll(x, DIFF_QK_DIM // 2, 1))
    return x * cos + partner * sin_signed


LOG2E = math.log2(math.e)


ATTN_OFFSET_SLACK = 64.0
SUBLANES = 8
BF16_ROWS = 16


def _scores(krot_ref, qs_ref, j, tk):
    k = krot_ref[j * tk:(j + 1) * tk, :]
    return lax.dot_general(k, qs_ref[...], (((1,), (1,)), ((), ())), preferred_element_type=F32)


def _attend_online(qs_ref, krot_ref, vt_ref, acc_ref, *, tk):
    n_tiles = krot_ref.shape[0] // tk
    m = l = None
    s_next = _scores(krot_ref, qs_ref, 0, tk)
    for j in range(n_tiles):
        s = s_next
        if j + 1 < n_tiles:
            s_next = _scores(krot_ref, qs_ref, j + 1, tk)
        m_cur = jnp.max(s, axis=0, keepdims=True)
        m_next = m_cur if m is None else jnp.maximum(m, m_cur)
        p = jnp.exp2(s - m_next)
        pv = jnp.dot(vt_ref[:, j * tk:(j + 1) * tk], p.astype(BF16), preferred_element_type=F32)
        if m is None:
            l = jnp.sum(p, axis=0, keepdims=True)
            acc_ref[...] = pv
        else:
            alpha = jnp.exp2(m - m_next)
            l = alpha * l + jnp.sum(p, axis=0, keepdims=True)
            acc_ref[...] = alpha * acc_ref[...] + pv
        m = m_next
    return l


def _attend_fixed(qs_ref, krot_ref, vt_ref, acc_ref, offset, *, tk):
    n_tiles = krot_ref.shape[0] // tk
    l = None
    s_next = _scores(krot_ref, qs_ref, 0, tk)
    for j in range(n_tiles):
        s = s_next
        if j + 1 < n_tiles:
            s_next = _scores(krot_ref, qs_ref, j + 1, tk)
        p = jnp.exp2(s - offset)
        pv = jnp.dot(vt_ref[:, j * tk:(j + 1) * tk], p.astype(BF16), preferred_element_type=F32)
        if l is None:
            l = jnp.sum(p, axis=0, keepdims=True)
            acc_ref[...] = pv
        else:
            l = l + jnp.sum(p, axis=0, keepdims=True)
            acc_ref[...] += pv
    return l


def _attend(qs_ref, q_norm2_max, krot_ref, vt_ref, kstat_ref, acc_ref, finish, *, tk):
    worst_exponent = jnp.sqrt(q_norm2_max) * kstat_ref[1:2, 0:1]
    use_fixed = jnp.max(worst_exponent) <= ATTN_OFFSET_SLACK

    @pl.when(use_fixed)
    def _():
        qf = qs_ref[...].astype(F32)
        ones = jnp.ones((BF16_ROWS, qf.shape[1]), BF16)
        q_norm2 = lax.dot_general(ones, (qf * qf).astype(BF16), (((1,), (1,)), ((), ())),
                                  preferred_element_type=F32)
        offset = jnp.sqrt(q_norm2[0:1]) * kstat_ref[0:1, 0:1]
        finish(_attend_fixed(qs_ref, krot_ref, vt_ref, acc_ref, offset, tk=tk))

    @pl.when(jnp.logical_not(use_fixed))
    def _():
        finish(_attend_online(qs_ref, krot_ref, vt_ref, acc_ref, tk=tk))


def _prepare_keys_values(k_ref, v_ref, cos_ref, sin_ref, krot_ref, vt_ref, kstat_ref, gain):
    seq = k_ref.shape[0]

    def body(c, carry):
        k_sum, k_max2 = carry
        r0 = pl.multiple_of(c * ROW_CHUNK, ROW_CHUNK)
        k = k_ref[pl.ds(r0, ROW_CHUNK), :].astype(F32)
        if gain is not None:
            k = _rms(k) * gain
        kb = _rope(k, cos_ref[pl.ds(r0, ROW_CHUNK), :], sin_ref[pl.ds(r0, ROW_CHUNK), :]).astype(BF16)
        krot_ref[pl.ds(r0, ROW_CHUNK), :] = kb
        vt_ref[:, pl.ds(r0, ROW_CHUNK)] = v_ref[pl.ds(r0, ROW_CHUNK), :].astype(F32).T.astype(BF16)
        kf = kb.astype(F32)
        norm2 = jnp.sum(kf * kf, axis=1, keepdims=True)
        return (k_sum + jnp.sum(kf, axis=0, keepdims=True),
                jnp.maximum(k_max2, jnp.max(norm2, axis=0, keepdims=True)))

    k_sum, k_max2 = lax.fori_loop(0, seq // ROW_CHUNK, body,
                                  (jnp.zeros((1, k_ref.shape[1]), F32), jnp.zeros((1, 1), F32)))
    k_bar = k_sum * (1.0 / seq)
    k_bar_norm = jnp.sqrt(jnp.sum(k_bar * k_bar, axis=1, keepdims=True))
    k_max = jnp.sqrt(k_max2) * 1.01
    kstat_ref[0:1, :] = jnp.broadcast_to(k_max, (1, kstat_ref.shape[1]))
    kstat_ref[1:2, :] = jnp.broadcast_to(k_max + k_bar_norm, (1, kstat_ref.shape[1]))


def _diff_attn_kernel(q_ref, k_ref, v_ref, cq_ref, sq_ref, ck_ref, sk_ref, lam_ref, g_ref, o_ref,
                      krot_ref, vt_ref, kstat_ref, qs_ref, acc_ref, *, tq, tk, lam_init):
    @pl.when(pl.program_id(1) == 0)
    def _():
        _prepare_keys_values(k_ref, v_ref, ck_ref, sk_ref, krot_ref, vt_ref, kstat_ref, None)

    q = _rope(q_ref[...].astype(F32), cq_ref[...], sq_ref[...]) * (DIFF_QK_DIM ** -0.5 * LOG2E)
    q_norm2_max = jnp.max(jnp.sum(q * q, axis=1, keepdims=True), axis=0, keepdims=True)
    lane = lax.broadcasted_iota(jnp.int32, q.shape, 1)
    qs_ref[0:tq, :] = jnp.where(lane < DIFF_QK_DIM, q, 0.0).astype(BF16)
    qs_ref[tq:2 * tq, :] = jnp.where(lane >= DIFF_QK_DIM, q, 0.0).astype(BF16)

    def finish(l):
        lp = lam_ref[...]
        lam = (jnp.exp(jnp.sum(lp[0:1] * lp[1:2], keepdims=True))
               - jnp.exp(jnp.sum(lp[2:3] * lp[3:4], keepdims=True)) + lam_init)
        ot = (acc_ref[:, 0:tq] / l[:, 0:tq] - lam * (acc_ref[:, tq:2 * tq] / l[:, tq:2 * tq]))
        ms = jnp.mean(ot * ot, axis=0, keepdims=True)
        ot = ot * lax.rsqrt(ms + EPS) * (g_ref[...] * (1.0 - lam_init))
        o_ref[...] = ot.T.astype(o_ref.dtype)

    _attend(qs_ref, q_norm2_max, krot_ref, vt_ref, kstat_ref, acc_ref, finish, tk=tk)


def _gqa_attn_kernel(q_ref, k_ref, v_ref, cq_ref, sq_ref, ck_ref, sk_ref, qn_ref, kn_ref, o_ref,
                     krot_ref, vt_ref, kstat_ref, qs_ref, acc_ref, *, tq, tk):
    @pl.when(pl.program_id(1) == 0)
    def _():
        _prepare_keys_values(k_ref, v_ref, ck_ref, sk_ref, krot_ref, vt_ref, kstat_ref, kn_ref[...])

    q_norm2_max = jnp.zeros((1, 1), F32)
    for r in range(2):
        q = q_ref[:, r * HEAD_DIM:(r + 1) * HEAD_DIM].astype(F32)
        q = _rope(_rms(q) * qn_ref[...], cq_ref[...], sq_ref[...]) * (HEAD_DIM ** -0.5 * LOG2E)
        q_norm2_max = jnp.maximum(
            q_norm2_max, jnp.max(jnp.sum(q * q, axis=1, keepdims=True), axis=0, keepdims=True))
        qs_ref[r * tq:(r + 1) * tq, :] = q.astype(BF16)

    def finish(l):
        for r in range(2):
            ot = acc_ref[:, r * tq:(r + 1) * tq] / l[:, r * tq:(r + 1) * tq]
            o_ref[:, r * HEAD_DIM:(r + 1) * HEAD_DIM] = ot.T.astype(o_ref.dtype)

    _attend(qs_ref, q_norm2_max, krot_ref, vt_ref, kstat_ref, acc_ref, finish, tk=tk)


def _flash_call(kernel, proj3, tables, params, *, n_groups, q_width, q_col, k_col, v_col,
                out_width, tq, name):
    batch, seq, _ = proj3.shape
    cos_t, sin_t = tables
    qb, kb, vb = q_col // q_width, k_col // HEAD_DIM, v_col // HEAD_DIM

    def grp(i):
        return i // n_groups, i % n_groups

    in_specs = [
        pl.BlockSpec((None, tq, q_width), lambda i, j: (grp(i)[0], j, qb + grp(i)[1])),
        pl.BlockSpec((None, seq, HEAD_DIM), lambda i, j: (grp(i)[0], 0, kb + grp(i)[1])),
        pl.BlockSpec((None, seq, HEAD_DIM), lambda i, j: (grp(i)[0], 0, vb + grp(i)[1])),
        pl.BlockSpec((tq, LANES), lambda i, j: (j, 0)),
        pl.BlockSpec((tq, LANES), lambda i, j: (j, 0)),
        pl.BlockSpec((seq, LANES), lambda i, j: (0, 0)),
        pl.BlockSpec((seq, LANES), lambda i, j: (0, 0)),
    ] + [pl.BlockSpec(p.shape, lambda i, j: (0, 0)) for p in params]
    return pl.pallas_call(
        kernel,
        grid=(batch * n_groups, seq // tq),
        in_specs=in_specs,
        out_specs=pl.BlockSpec((None, tq, out_width), lambda i, j: (grp(i)[0], j, grp(i)[1])),
        out_shape=jax.ShapeDtypeStruct((batch, seq, GROUP_WIDTH), BF16),
        scratch_shapes=[pltpu.VMEM((seq, HEAD_DIM), BF16),
                        pltpu.VMEM((HEAD_DIM, seq), BF16),
                        pltpu.VMEM((SUBLANES, LANES), F32),
                        pltpu.VMEM((2 * tq, HEAD_DIM), BF16),
                        pltpu.VMEM((HEAD_DIM, 2 * tq), F32)],
        compiler_params=_cparams(("parallel", "arbitrary")),
        name=name,
    )(proj3, proj3, proj3, cos_t, sin_t, cos_t, sin_t, *params)


def _conv_kernel(a_ref, g_ref, dw_ref, dwb_ref, lng_ref, lnb_ref, pw_ref, pwb_ref, o_ref,
                 u_ref, acc_ref, y_ref, *, tm):
    seq = a_ref.shape[0]
    t = pl.program_id(1)

    @pl.when(t == 0)
    def _():
        zeros = jnp.zeros((CONV_HALO, u_ref.shape[1]), F32)
        u_ref[0:CONV_HALO, :] = zeros
        u_ref[CONV_HALO + seq:2 * CONV_HALO + seq, :] = zeros

        def body(c, _):
            r0 = pl.multiple_of(c * ROW_CHUNK, ROW_CHUNK)
            a = a_ref[pl.ds(r0, ROW_CHUNK), :].astype(F32)
            g = g_ref[pl.ds(r0, ROW_CHUNK), :].astype(F32)
            u_ref[pl.ds(CONV_HALO + r0, ROW_CHUNK), :] = a * jax.nn.sigmoid(g)
            return 0

        lax.fori_loop(0, seq // ROW_CHUNK, body, 0)

    pad = CONV_WIDTH // 2
    sub = 8
    n_ch = u_ref.shape[1]

    def row_block(rb, _):
        base = pl.multiple_of(t * tm + rb * ROW_CHUNK, ROW_CHUNK)
        for cg in range(n_ch // LANES):
            cols = slice(cg * LANES, (cg + 1) * LANES)
            win = u_ref[pl.ds(base, ROW_CHUNK + 2 * CONV_HALO), cols]
            acc = jnp.zeros((ROW_CHUNK, LANES), F32) + dwb_ref[:, cols]
            for r in range(sub):
                shifted = win if r == 0 else pltpu.roll(win, win.shape[0] - r, 0)
                for a in range(2 * CONV_HALO // sub):
                    k = sub * a + r - (CONV_HALO - pad)
                    if 0 <= k < CONV_WIDTH:
                        acc = acc + shifted[sub * a:sub * a + ROW_CHUNK, :] * dw_ref[k:k + 1, cols]
            acc_ref[:, cols] = acc
        acc = acc_ref[...]
        mu = jnp.mean(acc, axis=-1, keepdims=True)
        cen = acc - mu
        var = jnp.mean(cen * cen, axis=-1, keepdims=True)
        y = cen * lax.rsqrt(var + EPS) * lng_ref[...] + lnb_ref[...]
        y_ref[pl.ds(pl.multiple_of(rb * ROW_CHUNK, ROW_CHUNK), ROW_CHUNK), :] = (
            y * jax.nn.sigmoid(y)).astype(BF16)
        return 0

    lax.fori_loop(0, tm // ROW_CHUNK, row_block, 0)
    o_ref[...] = (jnp.dot(y_ref[...], pw_ref[...], preferred_element_type=F32)
                  + pwb_ref[...]).astype(o_ref.dtype)


def _conv_mixer(proj3, dw, dwb, lng, lnb, pw, pwb, *, tm):
    batch, seq, _ = proj3.shape
    ch = GROUP_WIDTH
    vec = pl.BlockSpec((1, ch), lambda b, t: (0, 0))
    return pl.pallas_call(
        functools.partial(_conv_kernel, tm=tm),
        grid=(batch, seq // tm),
        in_specs=[pl.BlockSpec((None, seq, ch), lambda b, t: (b, 0, COL_B_A // ch)),
                  pl.BlockSpec((None, seq, ch), lambda b, t: (b, 0, COL_B_G // ch)),
                  pl.BlockSpec((CONV_WIDTH, ch), lambda b, t: (0, 0)),
                  vec, vec, vec,
                  pl.BlockSpec((ch, ch), lambda b, t: (0, 0)),
                  vec],
        out_specs=pl.BlockSpec((None, tm, ch), lambda b, t: (b, t, 0)),
        out_shape=jax.ShapeDtypeStruct((batch, seq, ch), BF16),
        scratch_shapes=[pltpu.VMEM((seq + 2 * CONV_HALO, ch), F32),
                        pltpu.VMEM((ROW_CHUNK, ch), F32),
                        pltpu.VMEM((tm, ch), BF16)],
        compiler_params=_cparams(("parallel", "arbitrary")),
        name="conv_mixer",
    )(proj3, proj3, dw, dwb, lng, lnb, pw, pwb)


def _na_geometry(seq):
    rows = seq // GRID_W
    qr_blk = Q_BLOCK // GRID_W
    mask_add = np.zeros((NA_CLASSES, Q_BLOCK, NA_BAND), np.float32)
    q_local = np.arange(Q_BLOCK)
    p = np.arange(NA_BAND)
    for c, blk in enumerate(_na_class_blocks(seq)):
        bs = int(np.clip(blk * qr_blk - NA_KH // 2, 0, rows - NA_BAND_ROWS))
        q_row = (blk * qr_blk + q_local // GRID_W)[:, None]
        q_col = (q_local % GRID_W)[:, None]
        k_row = (bs + p // GRID_W)[None, :]
        k_col = (p % GRID_W)[None, :]
        win_r = np.clip(q_row - NA_KH // 2, 0, rows - NA_KH)
        win_c = np.clip(q_col - NA_KW // 2, 0, GRID_W - NA_KW)
        ok = (k_row >= win_r) & (k_row < win_r + NA_KH) & (k_col >= win_c) & (k_col < win_c + NA_KW)
        mask_add[c] = np.where(ok, 0.0, -1e30)
    return jnp.asarray(mask_add)


def _na_class_blocks(seq):
    n_blk = seq // Q_BLOCK
    return (0, 1, 2, n_blk - 2, n_blk - 1)


def _rpb_tiles_kernel(rpb_ref, mask_ref, o_ref, t2_ref, *, seq):
    h = pl.program_id(0)
    n_c = 2 * NA_KW - 1
    n_r = 2 * NA_KH - 1
    shape = (GRID_W, LANES)
    lane = lax.broadcasted_iota(jnp.int32, shape, 1)
    q_col = lax.broadcasted_iota(jnp.int32, shape, 0)
    ic = jnp.clip((lane & (GRID_W - 1)) - q_col + (NA_KW - 1), 0, n_c - 1)
    left = lane < GRID_W
    base = h * (n_r * n_c)
    for i in range(NA_T2):
        i_l = min(max(i - 1, 0), n_r - 1)
        i_r = min(max(i, 0), n_r - 1)

        def body(j, acc, i_l=i_l, i_r=i_r):
            coef = jnp.where(left, rpb_ref[base + i_l * n_c + j], rpb_ref[base + i_r * n_c + j])
            return acc + jnp.where(ic == j, coef, 0.0)

        t2_ref[i] = lax.fori_loop(0, n_c, body, jnp.zeros(shape, F32))

    rows = seq // GRID_W
    qr_blk = Q_BLOCK // GRID_W
    for c, blk in enumerate(_na_class_blocks(seq)):
        bs = min(max(blk * qr_blk - NA_KH // 2, 0), rows - NA_BAND_ROWS)
        for a in range(qr_blk):
            for r2 in range(NA_BAND_ROWS // 2):
                i1 = bs + 2 * r2 - (blk * qr_blk + a) + (NA_KH - 1)
                idx = min(max(i1, -1), NA_T2 - 2) + 1
                rs = slice(a * GRID_W, (a + 1) * GRID_W)
                cs = slice(r2 * LANES, (r2 + 1) * LANES)
                o_ref[c, rs, cs] = t2_ref[idx] + mask_ref[c, rs, cs]


def _rpb_tiles(rpb, mask_add, seq):
    heads = rpb.shape[0]
    return pl.pallas_call(
        functools.partial(_rpb_tiles_kernel, seq=seq),
        grid=(heads,),
        in_specs=[pl.BlockSpec(memory_space=pltpu.SMEM),
                  pl.BlockSpec(mask_add.shape, lambda h: (0, 0, 0))],
        out_specs=pl.BlockSpec((None,) + mask_add.shape, lambda h: (h, 0, 0, 0)),
        out_shape=jax.ShapeDtypeStruct((heads,) + mask_add.shape, F32),
        scratch_shapes=[pltpu.VMEM((NA_T2, GRID_W, LANES), F32)],
        compiler_params=_cparams(("parallel",)),
        name="na_rpb_tiles",
    )(rpb.reshape(-1), mask_add)


def _na_kernel(q_ref, k_ref, v_ref, bias_ref, o_ref):
    seq = q_ref.shape[0]
    rows = seq // GRID_W
    qr_blk = Q_BLOCK // GRID_W
    n_blk = seq // Q_BLOCK

    classes = _na_class_blocks(seq)

    def band(blk):
        k0 = min(max(blk * qr_blk - NA_KH // 2, 0), rows - NA_BAND_ROWS) * GRID_W
        return slice(k0, k0 + NA_BAND)

    def scores(blk):
        q = q_ref[blk * Q_BLOCK:(blk + 1) * Q_BLOCK, :]
        return lax.dot_general(q, k_ref[band(blk), :], (((1,), (1,)), ((), ())),
                               preferred_element_type=F32)

    pending = [scores(b) for b in range(NA_LOOKAHEAD)]
    for blk in range(n_blk):
        s = pending.pop(0)
        if blk + NA_LOOKAHEAD < n_blk:
            pending.append(scores(blk + NA_LOOKAHEAD))
        cls = classes.index(blk) if blk in classes else 2
        s = s * (HEAD_DIM ** -0.5) + bias_ref[cls]
        m = jnp.max(s, axis=1, keepdims=True)
        p = jnp.exp(s - m)
        l = jnp.sum(p, axis=1, keepdims=True)
        o = jnp.dot(p.astype(BF16), v_ref[band(blk), :], preferred_element_type=F32) / l
        o_ref[blk * Q_BLOCK:(blk + 1) * Q_BLOCK, :] = o.astype(o_ref.dtype)


def _na_mixer(proj3, bias):
    batch, seq, _ = proj3.shape
    heads = GROUP_HEADS

    def head_spec(col):
        return pl.BlockSpec((None, seq, HEAD_DIM), lambda b, h: (b, 0, col // HEAD_DIM + h))

    return pl.pallas_call(
        _na_kernel,
        grid=(batch, heads),
        in_specs=[head_spec(COL_D_Q), head_spec(COL_D_K), head_spec(COL_D_V),
                  pl.BlockSpec((None,) + bias.shape[1:], lambda b, h: (h, 0, 0, 0))],
        out_specs=pl.BlockSpec((None, seq, HEAD_DIM), lambda b, h: (b, 0, h)),
        out_shape=jax.ShapeDtypeStruct((batch, seq, GROUP_WIDTH), BF16),
        compiler_params=_cparams(("parallel", "parallel")),
        name="na_mixer",
    )(proj3, proj3, proj3, bias)


def kernel(x, norm_mix_pre, norm_mix_post, norm_ffn_pre, norm_ffn_post, w_in, w_out, diff_lambda, diff_subln, conv_dw, conv_dw_b, conv_ln_g, conv_ln_b, conv_pw, conv_pw_b, gqa_q_norm, gqa_k_norm, na_rpb, ffn_gate, ffn_up, ffn_down):
    batch, seq, d = x.shape
    depth = w_in.shape[0]
    tokens = batch * seq

    t = np.arange(seq)
    diff_tables = _rope_tables(t, t)
    axial_tables = _rope_tables(t // GRID_W, t % GRID_W)
    mask_add = _na_geometry(seq)

    def row(v):
        return v.reshape(1, -1)

    xf = x.reshape(tokens, d)
    h = _rmsnorm(xf, row(norm_mix_pre[0]), tm=1024)
    for l in range(depth):
        lam_init = 0.8 - 0.6 * math.exp(-0.3 * l)
        proj = _in_proj(h, w_in, l, tm=2048, tn=1024)
        proj3 = proj.reshape(batch, seq, IN_COLS)

        out_a = _flash_call(
            functools.partial(_diff_attn_kernel, tq=ATTN_TQ, tk=ATTN_TK, lam_init=lam_init),
            proj3, diff_tables, (diff_lambda[l], diff_subln[l].reshape(-1, 1)),
            n_groups=GROUP_HEADS, q_width=HEAD_DIM, q_col=COL_A_Q, k_col=COL_A_K, v_col=COL_A_V,
            out_width=HEAD_DIM, tq=ATTN_TQ, name="diff_attn")
        out_b = _conv_mixer(proj3, conv_dw[l], row(conv_dw_b[l]), row(conv_ln_g[l]), row(conv_ln_b[l]),
                            conv_pw[l].astype(BF16), row(conv_pw_b[l]), tm=512)
        out_c = _flash_call(
            functools.partial(_gqa_attn_kernel, tq=ATTN_TQ, tk=ATTN_TK),
            proj3, axial_tables, (row(gqa_q_norm[l]), row(gqa_k_norm[l])),
            n_groups=GQA_KV_HEADS, q_width=2 * HEAD_DIM, q_col=COL_C_Q, k_col=COL_C_K, v_col=COL_C_V,
            out_width=2 * HEAD_DIM, tq=ATTN_TQ, name="gqa_attn")
        out_d = _na_mixer(proj3, _rpb_tiles(na_rpb[l], mask_add, seq))

        parts = [o.reshape(tokens, GROUP_WIDTH) for o in (out_a, out_b, out_c, out_d)]
        xf, h = _out_proj(parts, w_out[l].astype(BF16), row(norm_mix_post[l]), xf,
                          row(norm_ffn_pre[l]), tm=512)

        act, w_down = _ffn_up(h, ffn_gate, ffn_up, ffn_down, l, tm=2048, tn=512)
        next_gain = row(norm_mix_pre[l + 1]) if l + 1 < depth else None
        xf, h = _ffn_down(act, w_down, row(norm_ffn_post[l]), xf, next_gain,
                          tm=1024, tk=1408)
    return xf.reshape(batch, seq, d)
```

```python
import functools
import math

import numpy as np
import jax
import jax.numpy as jnp
from jax import lax
from jax.experimental import pallas as pl
from jax.experimental.pallas import tpu as pltpu

D_MODEL = 2048
HEAD_DIM = 128
GROUP_HEADS = 4
GROUP_WIDTH = GROUP_HEADS * HEAD_DIM
DIFF_QK_DIM = HEAD_DIM // 2
CONV_WIDTH = 31
GQA_KV_HEADS = 2
NA_KH = 8
NA_KW = 16
GRID_W = 64
Q_BLOCK = 128
ROPE_THETA = 10000.0
EPS = 1e-6

COL_A_Q, COL_A_K, COL_A_V = 0, 512, 1024
COL_B_A, COL_B_G = 1536, 2048
COL_C_Q, COL_C_K, COL_C_V = 2560, 3072, 3328
COL_D_Q, COL_D_K, COL_D_V = 3584, 4096, 4608
IN_COLS = 5120

LANES = 128
VMEM_LIMIT = 56 * 1024 * 1024
ROW_CHUNK = 128
ATTN_TQ = 512
ATTN_TK = 512

NA_BAND_ROWS = 10
NA_BAND = NA_BAND_ROWS * GRID_W
NA_CLASSES = 5
NA_T2 = 16
NA_LOOKAHEAD = 2
CONV_HALO = 16

F32 = jnp.float32
BF16 = jnp.bfloat16


def _cparams(semantics):
    return pltpu.CompilerParams(dimension_semantics=semantics,
                                vmem_limit_bytes=VMEM_LIMIT)


def _rms(x):
    return x * lax.rsqrt(jnp.mean(x * x, axis=-1, keepdims=True) + EPS)


def _rmsnorm_kernel(x_ref, g_ref, h_ref):
    def body(c, _):
        r0 = pl.multiple_of(c * ROW_CHUNK, ROW_CHUNK)
        h_ref[pl.ds(r0, ROW_CHUNK), :] = (_rms(x_ref[pl.ds(r0, ROW_CHUNK), :]) * g_ref[...]).astype(h_ref.dtype)
        return 0

    lax.fori_loop(0, x_ref.shape[0] // ROW_CHUNK, body, 0)


def _rmsnorm(x, g, *, tm):
    m, d = x.shape
    return pl.pallas_call(
        _rmsnorm_kernel,
        grid=(m // tm,),
        in_specs=[pl.BlockSpec((tm, d), lambda i: (i, 0)),
                  pl.BlockSpec((1, d), lambda i: (0, 0))],
        out_specs=pl.BlockSpec((tm, d), lambda i: (i, 0)),
        out_shape=jax.ShapeDtypeStruct((m, d), BF16),
        compiler_params=_cparams(("parallel",)),
        name="rmsnorm",
    )(x, g)


def _cast_weight_tile(w_ref, wb_ref):
    def body(c, _):
        r0 = pl.multiple_of(c * ROW_CHUNK, ROW_CHUNK)
        wb_ref[pl.ds(r0, ROW_CHUNK), :] = w_ref[pl.ds(r0, ROW_CHUNK), :].astype(wb_ref.dtype)
        return 0

    lax.fori_loop(0, w_ref.shape[0] // ROW_CHUNK, body, 0)


def _in_proj_kernel(h_ref, w_ref, o_ref, wb_ref):
    @pl.when(pl.program_id(1) == 0)
    def _():
        _cast_weight_tile(w_ref, wb_ref)

    o_ref[...] = jnp.dot(h_ref[...], wb_ref[...], preferred_element_type=F32).astype(o_ref.dtype)


def _in_proj(h, w, layer, *, tm, tn):
    m, d = h.shape
    n = w.shape[2]
    return pl.pallas_call(
        _in_proj_kernel,
        grid=(n // tn, m // tm),
        in_specs=[pl.BlockSpec((tm, d), lambda j, i: (i, 0)),
                  pl.BlockSpec((None, d, tn), lambda j, i: (layer, 0, j))],
        out_specs=pl.BlockSpec((tm, tn), lambda j, i: (i, j)),
        out_shape=jax.ShapeDtypeStruct((m, n), BF16),
        scratch_shapes=[pltpu.VMEM((d, tn), BF16)],
        compiler_params=_cparams(("parallel", "arbitrary")),
        name="in_proj",
    )(h, w)


def _ffn_up_kernel(h_ref, wg_ref, wu_ref, wd_ref, o_ref, wdb_ref, wgb_ref, wub_ref):
    @pl.when(pl.program_id(1) == 0)
    def _():
        _cast_weight_tile(wg_ref, wgb_ref)
        _cast_weight_tile(wu_ref, wub_ref)

    wdb_ref[...] = wd_ref[...].astype(wdb_ref.dtype)

    h = h_ref[...]
    gate = jnp.dot(h, wgb_ref[...], preferred_element_type=F32)
    up = jnp.dot(h, wub_ref[...], preferred_element_type=F32)
    o_ref[...] = (gate * jax.nn.sigmoid(gate) * up).astype(o_ref.dtype)


def _ffn_up(h, wg, wu, wd, layer, *, tm, tn):
    m, d = h.shape
    n = wg.shape[2]
    n_i = m // tm
    slab = wd.shape[1] // ((n // tn) * n_i)
    assert slab * (n // tn) * n_i == wd.shape[1] and slab % BF16_ROWS == 0
    w_spec = pl.BlockSpec((None, d, tn), lambda j, i: (layer, 0, j))
    return pl.pallas_call(
        _ffn_up_kernel,
        grid=(n // tn, n_i),
        in_specs=[pl.BlockSpec((tm, d), lambda j, i: (i, 0)), w_spec, w_spec,
                  pl.BlockSpec((None, slab, wd.shape[2]), lambda j, i: (layer, j * n_i + i, 0))],
        out_specs=[pl.BlockSpec((tm, tn), lambda j, i: (i, j)),
                   pl.BlockSpec((slab, wd.shape[2]), lambda j, i: (j * n_i + i, 0))],
        out_shape=[jax.ShapeDtypeStruct((m, n), BF16),
                   jax.ShapeDtypeStruct(wd.shape[1:], BF16)],
        scratch_shapes=[pltpu.VMEM((d, tn), BF16), pltpu.VMEM((d, tn), BF16)],
        compiler_params=_cparams(("parallel", "arbitrary")),
        name="ffn_up",
    )(h, wg, wu, wd)


def _ffn_down_kernel(*refs, emit_next):
    if emit_next:
        a_ref, w_ref, g_ref, x_ref, gn_ref, o_ref, hn_ref, f_ref = refs
    else:
        a_ref, w_ref, g_ref, x_ref, o_ref, f_ref = refs
        gn_ref = hn_ref = None
    j = pl.program_id(1)
    n_col, _, tn = f_ref.shape
    f_ref[j] = jnp.dot(a_ref[...], w_ref[...], preferred_element_type=F32)

    @pl.when(j == n_col - 1)
    def _():
        d = n_col * tn

        def body(c, _):
            r0 = pl.multiple_of(c * ROW_CHUNK, ROW_CHUNK)
            rows = pl.ds(r0, ROW_CHUNK)
            f = [f_ref[b, rows, :] for b in range(n_col)]
            r = lax.rsqrt(sum(jnp.sum(fb * fb, axis=-1, keepdims=True) for fb in f) * (1.0 / d) + EPS)
            o = [x_ref[rows, b * tn:(b + 1) * tn] + f[b] * r * g_ref[:, b * tn:(b + 1) * tn]
                 for b in range(n_col)]
            for b in range(n_col):
                o_ref[rows, b * tn:(b + 1) * tn] = o[b]
            if hn_ref is not None:
                r = lax.rsqrt(sum(jnp.sum(ob * ob, axis=-1, keepdims=True) for ob in o) * (1.0 / d) + EPS)
                for b in range(n_col):
                    hn_ref[rows, b * tn:(b + 1) * tn] = (
                        o[b] * r * gn_ref[:, b * tn:(b + 1) * tn]).astype(hn_ref.dtype)
            return 0

        lax.fori_loop(0, f_ref.shape[1] // ROW_CHUNK, body, 0)


def _ffn_down(a, w, g, x, gn, *, tm, tn):
    m, kdim = a.shape
    d = w.shape[1]
    emit_next = gn is not None
    vec = pl.BlockSpec((1, d), lambda i, j: (0, 0))
    row_tile = pl.BlockSpec((tm, d), lambda i, j: (i, 0))
    outs = pl.pallas_call(
        functools.partial(_ffn_down_kernel, emit_next=emit_next),
        grid=(m // tm, d // tn),
        in_specs=[pl.BlockSpec((tm, kdim), lambda i, j: (i, 0)),
                  pl.BlockSpec((kdim, tn), lambda i, j: (0, j)),
                  vec, row_tile] + [vec] * emit_next,
        out_specs=[row_tile] + [row_tile] * emit_next,
        out_shape=[jax.ShapeDtypeStruct((m, d), F32)] + [jax.ShapeDtypeStruct((m, d), BF16)] * emit_next,
        scratch_shapes=[pltpu.VMEM((d // tn, tm, tn), F32)],
        compiler_params=_cparams(("parallel", "arbitrary")),
        name="ffn_down",
    )(a, w, g, x, *([gn] * emit_next))
    return (outs[0], outs[1]) if emit_next else (outs[0], None)


def _out_proj_kernel(a_ref, b_ref, c_ref, d_ref, w_ref, g_ref, x_ref, gn_ref, o_ref, hn_ref,
                     lhs_ref, acc_ref):
    for p, part in enumerate((a_ref, b_ref, c_ref, d_ref)):
        lhs_ref[:, p * GROUP_WIDTH:(p + 1) * GROUP_WIDTH] = part[...]
    n_half, half, _ = acc_ref.shape
    for b in range(n_half):
        acc_ref[b] = jnp.dot(lhs_ref[b * half:(b + 1) * half, :], w_ref[...],
                             preferred_element_type=F32)
    for b in range(n_half):
        for c in range(half // ROW_CHUNK):
            rows = slice(b * half + c * ROW_CHUNK, b * half + (c + 1) * ROW_CHUNK)
            o = x_ref[rows, :] + _rms(acc_ref[b, c * ROW_CHUNK:(c + 1) * ROW_CHUNK, :]) * g_ref[...]
            o_ref[rows, :] = o
            hn_ref[rows, :] = (_rms(o) * gn_ref[...]).astype(hn_ref.dtype)


def _out_proj(parts, w, g, x, gn, *, tm):
    m, d = x.shape
    part_spec = pl.BlockSpec((tm, GROUP_WIDTH), lambda i: (i, 0))
    vec = pl.BlockSpec((1, d), lambda i: (0, 0))
    row_tile = pl.BlockSpec((tm, d), lambda i: (i, 0))
    return pl.pallas_call(
        _out_proj_kernel,
        grid=(m // tm,),
        in_specs=[part_spec, part_spec, part_spec, part_spec,
                  pl.BlockSpec((4 * GROUP_WIDTH, d), lambda i: (0, 0)),
                  vec, row_tile, vec],
        out_specs=[row_tile, row_tile],
        out_shape=[jax.ShapeDtypeStruct((m, d), F32), jax.ShapeDtypeStruct((m, d), BF16)],
        scratch_shapes=[pltpu.VMEM((tm, 4 * GROUP_WIDTH), BF16),
                        pltpu.VMEM((2, tm // 2, d), F32)],
        compiler_params=_cparams(("parallel",)),
        name="out_proj",
    )(*parts, w, g, x, gn)


def _rope_tables(pos_lo, pos_hi):
    half = DIFF_QK_DIM // 2
    inv = ROPE_THETA ** (-np.arange(half, dtype=np.float64) / half)
    lane = np.arange(LANES)
    pos = np.where(lane[None, :] < DIFF_QK_DIM, pos_lo[:, None], pos_hi[:, None]).astype(np.float64)
    ang = pos * inv[lane % half][None, :]
    sign = np.where((lane % DIFF_QK_DIM) < half, -1.0, 1.0)
    return (jnp.asarray(np.cos(ang), dtype=F32),
            jnp.asarray(np.sin(ang) * sign[None, :], dtype=F32))


def _rope(x, cos, sin_signed):
    lane = lax.broadcasted_iota(jnp.int32, x.shape, 1)
    first_half = (lane & (DIFF_QK_DIM // 2)) == 0
    partner = jnp.where(first_half, pltpu.roll(x, LANES - DIFF_QK_DIM // 2, 1),
                        pltpu.roll(x, DIFF_QK_DIM // 2, 1))
    return x * cos + partner * sin_signed


LOG2E = math.log2(math.e)


ATTN_OFFSET_SLACK = 64.0
SUBLANES = 8
BF16_ROWS = 16


def _scores(krot_ref, qs_ref, j, tk):
    k = krot_ref[j * tk:(j + 1) * tk, :]
    return lax.dot_general(k, qs_ref[...], (((1,), (1,)), ((), ())), preferred_element_type=F32)


def _attend_online(qs_ref, krot_ref, vt_ref, acc_ref, *, tk):
    n_tiles = krot_ref.shape[0] // tk
    m = l = None
    s_next = _scores(krot_ref, qs_ref, 0, tk)
    for j in range(n_tiles):
        s = s_next
        if j + 1 < n_tiles:
            s_next = _scores(krot_ref, qs_ref, j + 1, tk)
        m_cur = jnp.max(s, axis=0, keepdims=True)
        m_next = m_cur if m is None else jnp.maximum(m, m_cur)
        p = jnp.exp2(s - m_next)
        pv = jnp.dot(vt_ref[:, j * tk:(j + 1) * tk], p.astype(BF16), preferred_element_type=F32)
        if m is None:
            l = jnp.sum(p, axis=0, keepdims=True)
            acc_ref[...] = pv
        else:
            alpha = jnp.exp2(m - m_next)
            l = alpha * l + jnp.sum(p, axis=0, keepdims=True)
            acc_ref[...] = alpha * acc_ref[...] + pv
        m = m_next
    return l


def _attend_fixed(qs_ref, krot_ref, vt_ref, acc_ref, offset, *, tk):
    n_tiles = krot_ref.shape[0] // tk
    l = None
    s_next = _scores(krot_ref, qs_ref, 0, tk)
    for j in range(n_tiles):
        s = s_next
        if j + 1 < n_tiles:
            s_next = _scores(krot_ref, qs_ref, j + 1, tk)
        p = jnp.exp2(s - offset)
        pv = jnp.dot(vt_ref[:, j * tk:(j + 1) * tk], p.astype(BF16), preferred_element_type=F32)
        if l is None:
            l = jnp.sum(p, axis=0, keepdims=True)
            acc_ref[...] = pv
        else:
            l = l + jnp.sum(p, axis=0, keepdims=True)
            acc_ref[...] += pv
    return l


def _attend(qs_ref, q_norm2_max, krot_ref, vt_ref, kstat_ref, acc_ref, finish, *, tk):
    worst_exponent = jnp.sqrt(q_norm2_max) * kstat_ref[1:2, 0:1]
    use_fixed = jnp.max(worst_exponent) <= ATTN_OFFSET_SLACK

    @pl.when(use_fixed)
    def _():
        qf = qs_ref[...].astype(F32)
        ones = jnp.ones((BF16_ROWS, qf.shape[1]), BF16)
        q_norm2 = lax.dot_general(ones, (qf * qf).astype(BF16), (((1,), (1,)), ((), ())),
                                  preferred_element_type=F32)
        offset = jnp.sqrt(q_norm2[0:1]) * kstat_ref[0:1, 0:1]
        finish(_attend_fixed(qs_ref, krot_ref, vt_ref, acc_ref, offset, tk=tk))

    @pl.when(jnp.logical_not(use_fixed))
    def _():
        finish(_attend_online(qs_ref, krot_ref, vt_ref, acc_ref, tk=tk))


def _prepare_keys_values(k_ref, v_ref, cos_ref, sin_ref, krot_ref, vt_ref, kstat_ref, gain):
    seq = k_ref.shape[0]

    def body(c, carry):
        k_sum, k_max2 = carry
        r0 = pl.multiple_of(c * ROW_CHUNK, ROW_CHUNK)
        k = k_ref[pl.ds(r0, ROW_CHUNK), :].astype(F32)
        if gain is not None:
            k = _rms(k) * gain
        kb = _rope(k, cos_ref[pl.ds(r0, ROW_CHUNK), :], sin_ref[pl.ds(r0, ROW_CHUNK), :]).astype(BF16)
        krot_ref[pl.ds(r0, ROW_CHUNK), :] = kb
        vt_ref[:, pl.ds(r0, ROW_CHUNK)] = v_ref[pl.ds(r0, ROW_CHUNK), :].astype(F32).T.astype(BF16)
        kf = kb.astype(F32)
        norm2 = jnp.sum(kf * kf, axis=1, keepdims=True)
        return (k_sum + jnp.sum(kf, axis=0, keepdims=True),
                jnp.maximum(k_max2, jnp.max(norm2, axis=0, keepdims=True)))

    k_sum, k_max2 = lax.fori_loop(0, seq // ROW_CHUNK, body,
                                  (jnp.zeros((1, k_ref.shape[1]), F32), jnp.zeros((1, 1), F32)))
    k_bar = k_sum * (1.0 / seq)
    k_bar_norm = jnp.sqrt(jnp.sum(k_bar * k_bar, axis=1, keepdims=True))
    k_max = jnp.sqrt(k_max2) * 1.01
    kstat_ref[0:1, :] = jnp.broadcast_to(k_max, (1, kstat_ref.shape[1]))
    kstat_ref[1:2, :] = jnp.broadcast_to(k_max + k_bar_norm, (1, kstat_ref.shape[1]))


def _diff_attn_kernel(q_ref, k_ref, v_ref, cq_ref, sq_ref, ck_ref, sk_ref, lam_ref, g_ref, o_ref,
                      krot_ref, vt_ref, kstat_ref, qs_ref, acc_ref, *, tq, tk, lam_init):
    @pl.when(pl.program_id(1) == 0)
    def _():
        _prepare_keys_values(k_ref, v_ref, ck_ref, sk_ref, krot_ref, vt_ref, kstat_ref, None)

    q = _rope(q_ref[...].astype(F32), cq_ref[...], sq_ref[...]) * (DIFF_QK_DIM ** -0.5 * LOG2E)
    q_norm2_max = jnp.max(jnp.sum(q * q, axis=1, keepdims=True), axis=0, keepdims=True)
    lane = lax.broadcasted_iota(jnp.int32, q.shape, 1)
    qs_ref[0:tq, :] = jnp.where(lane < DIFF_QK_DIM, q, 0.0).astype(BF16)
    qs_ref[tq:2 * tq, :] = jnp.where(lane >= DIFF_QK_DIM, q, 0.0).astype(BF16)

    def finish(l):
        lp = lam_ref[...]
        lam = (jnp.exp(jnp.sum(lp[0:1] * lp[1:2], keepdims=True))
               - jnp.exp(jnp.sum(lp[2:3] * lp[3:4], keepdims=True)) + lam_init)
        ot = (acc_ref[:, 0:tq] / l[:, 0:tq] - lam * (acc_ref[:, tq:2 * tq] / l[:, tq:2 * tq]))
        ms = jnp.mean(ot * ot, axis=0, keepdims=True)
        ot = ot * lax.rsqrt(ms + EPS) * (g_ref[...] * (1.0 - lam_init))
        o_ref[...] = ot.T.astype(o_ref.dtype)

    _attend(qs_ref, q_norm2_max, krot_ref, vt_ref, kstat_ref, acc_ref, finish, tk=tk)


def _gqa_attn_kernel(q_ref, k_ref, v_ref, cq_ref, sq_ref, ck_ref, sk_ref, qn_ref, kn_ref, o_ref,
                     krot_ref, vt_ref, kstat_ref, qs_ref, acc_ref, *, tq, tk):
    @pl.when(pl.program_id(1) == 0)
    def _():
        _prepare_keys_values(k_ref, v_ref, ck_ref, sk_ref, krot_ref, vt_ref, kstat_ref, kn_ref[...])

    q_norm2_max = jnp.zeros((1, 1), F32)
    for r in range(2):
        q = q_ref[:, r * HEAD_DIM:(r + 1) * HEAD_DIM].astype(F32)
        q = _rope(_rms(q) * qn_ref[...], cq_ref[...], sq_ref[...]) * (HEAD_DIM ** -0.5 * LOG2E)
        q_norm2_max = jnp.maximum(
            q_norm2_max, jnp.max(jnp.sum(q * q, axis=1, keepdims=True), axis=0, keepdims=True))
        qs_ref[r * tq:(r + 1) * tq, :] = q.astype(BF16)

    def finish(l):
        for r in range(2):
            ot = acc_ref[:, r * tq:(r + 1) * tq] / l[:, r * tq:(r + 1) * tq]
            o_ref[:, r * HEAD_DIM:(r + 1) * HEAD_DIM] = ot.T.astype(o_ref.dtype)

    _attend(qs_ref, q_norm2_max, krot_ref, vt_ref, kstat_ref, acc_ref, finish, tk=tk)


def _flash_call(kernel, proj3, tables, params, *, n_groups, q_width, q_col, k_col, v_col,
                out_width, tq, name):
    batch, seq, _ = proj3.shape
    cos_t, sin_t = tables
    qb, kb, vb = q_col // q_width, k_col // HEAD_DIM, v_col // HEAD_DIM

    def grp(i):
        return i // n_groups, i % n_groups

    in_specs = [
        pl.BlockSpec((None, tq, q_width), lambda i, j: (grp(i)[0], j, qb + grp(i)[1])),
        pl.BlockSpec((None, seq, HEAD_DIM), lambda i, j: (grp(i)[0], 0, kb + grp(i)[1])),
        pl.BlockSpec((None, seq, HEAD_DIM), lambda i, j: (grp(i)[0], 0, vb + grp(i)[1])),
        pl.BlockSpec((tq, LANES), lambda i, j: (j, 0)),
        pl.BlockSpec((tq, LANES), lambda i, j: (j, 0)),
        pl.BlockSpec((seq, LANES), lambda i, j: (0, 0)),
        pl.BlockSpec((seq, LANES), lambda i, j: (0, 0)),
    ] + [pl.BlockSpec(p.shape, lambda i, j: (0, 0)) for p in params]
    return pl.pallas_call(
        kernel,
        grid=(batch * n_groups, seq // tq),
        in_specs=in_specs,
        out_specs=pl.BlockSpec((None, tq, out_width), lambda i, j: (grp(i)[0], j, grp(i)[1])),
        out_shape=jax.ShapeDtypeStruct((batch, seq, GROUP_WIDTH), BF16),
        scratch_shapes=[pltpu.VMEM((seq, HEAD_DIM), BF16),
                        pltpu.VMEM((HEAD_DIM, seq), BF16),
                        pltpu.VMEM((SUBLANES, LANES), F32),
                        pltpu.VMEM((2 * tq, HEAD_DIM), BF16),
                        pltpu.VMEM((HEAD_DIM, 2 * tq), F32)],
        compiler_params=_cparams(("parallel", "arbitrary")),
        name=name,
    )(proj3, proj3, proj3, cos_t, sin_t, cos_t, sin_t, *params)


def _conv_kernel(a_ref, g_ref, dw_ref, dwb_ref, lng_ref, lnb_ref, pw_ref, pwb_ref, o_ref,
                 u_ref, acc_ref, y_ref, *, tm):
    seq = a_ref.shape[0]
    t = pl.program_id(1)

    @pl.when(t == 0)
    def _():
        zeros = jnp.zeros((CONV_HALO, u_ref.shape[1]), F32)
        u_ref[0:CONV_HALO, :] = zeros
        u_ref[CONV_HALO + seq:2 * CONV_HALO + seq, :] = zeros

        def body(c, _):
            r0 = pl.multiple_of(c * ROW_CHUNK, ROW_CHUNK)
            a = a_ref[pl.ds(r0, ROW_CHUNK), :].astype(F32)
            g = g_ref[pl.ds(r0, ROW_CHUNK), :].astype(F32)
            u_ref[pl.ds(CONV_HALO + r0, ROW_CHUNK), :] = a * jax.nn.sigmoid(g)
            return 0

        lax.fori_loop(0, seq // ROW_CHUNK, body, 0)

    pad = CONV_WIDTH // 2
    sub = 8
    n_ch = u_ref.shape[1]

    def row_block(rb, _):
        base = pl.multiple_of(t * tm + rb * ROW_CHUNK, ROW_CHUNK)
        for cg in range(n_ch // LANES):
            cols = slice(cg * LANES, (cg + 1) * LANES)
            win = u_ref[pl.ds(base, ROW_CHUNK + 2 * CONV_HALO), cols]
            acc = jnp.zeros((ROW_CHUNK, LANES), F32) + dwb_ref[:, cols]
            for r in range(sub):
                shifted = win if r == 0 else pltpu.roll(win, win.shape[0] - r, 0)
                for a in range(2 * CONV_HALO // sub):
                    k = sub * a + r - (CONV_HALO - pad)
                    if 0 <= k < CONV_WIDTH:
                        acc = acc + shifted[sub * a:sub * a + ROW_CHUNK, :] * dw_ref[k:k + 1, cols]
            acc_ref[:, cols] = acc
        acc = acc_ref[...]
        mu = jnp.mean(acc, axis=-1, keepdims=True)
        cen = acc - mu
        var = jnp.mean(cen * cen, axis=-1, keepdims=True)
        y = cen * lax.rsqrt(var + EPS) * lng_ref[...] + lnb_ref[...]
        y_ref[pl.ds(pl.multiple_of(rb * ROW_CHUNK, ROW_CHUNK), ROW_CHUNK), :] = (
            y * jax.nn.sigmoid(y)).astype(BF16)
        return 0

    lax.fori_loop(0, tm // ROW_CHUNK, row_block, 0)
    o_ref[...] = (jnp.dot(y_ref[...], pw_ref[...], preferred_element_type=F32)
                  + pwb_ref[...]).astype(o_ref.dtype)


def _conv_mixer(proj3, dw, dwb, lng, lnb, pw, pwb, *, tm):
    batch, seq, _ = proj3.shape
    ch = GROUP_WIDTH
    vec = pl.BlockSpec((1, ch), lambda b, t: (0, 0))
    return pl.pallas_call(
        functools.partial(_conv_kernel, tm=tm),
        grid=(batch, seq // tm),
        in_specs=[pl.BlockSpec((None, seq, ch), lambda b, t: (b, 0, COL_B_A // ch)),
                  pl.BlockSpec((None, seq, ch), lambda b, t: (b, 0, COL_B_G // ch)),
                  pl.BlockSpec((CONV_WIDTH, ch), lambda b, t: (0, 0)),
                  vec, vec, vec,
                  pl.BlockSpec((ch, ch), lambda b, t: (0, 0)),
                  vec],
        out_specs=pl.BlockSpec((None, tm, ch), lambda b, t: (b, t, 0)),
        out_shape=jax.ShapeDtypeStruct((batch, seq, ch), BF16),
        scratch_shapes=[pltpu.VMEM((seq + 2 * CONV_HALO, ch), F32),
                        pltpu.VMEM((ROW_CHUNK, ch), F32),
                        pltpu.VMEM((tm, ch), BF16)],
        compiler_params=_cparams(("parallel", "arbitrary")),
        name="conv_mixer",
    )(proj3, proj3, dw, dwb, lng, lnb, pw, pwb)


def _na_geometry(seq):
    rows = seq // GRID_W
    qr_blk = Q_BLOCK // GRID_W
    mask_add = np.zeros((NA_CLASSES, Q_BLOCK, NA_BAND), np.float32)
    q_local = np.arange(Q_BLOCK)
    p = np.arange(NA_BAND)
    for c, blk in enumerate(_na_class_blocks(seq)):
        bs = int(np.clip(blk * qr_blk - NA_KH // 2, 0, rows - NA_BAND_ROWS))
        q_row = (blk * qr_blk + q_local // GRID_W)[:, None]
        q_col = (q_local % GRID_W)[:, None]
        k_row = (bs + p // GRID_W)[None, :]
        k_col = (p % GRID_W)[None, :]
        win_r = np.clip(q_row - NA_KH // 2, 0, rows - NA_KH)
        win_c = np.clip(q_col - NA_KW // 2, 0, GRID_W - NA_KW)
        ok = (k_row >= win_r) & (k_row < win_r + NA_KH) & (k_col >= win_c) & (k_col < win_c + NA_KW)
        mask_add[c] = np.where(ok, 0.0, -1e30)
    return jnp.asarray(mask_add)


def _na_class_blocks(seq):
    n_blk = seq // Q_BLOCK
    return (0, 1, 2, n_blk - 2, n_blk - 1)


def _rpb_tiles_kernel(rpb_ref, mask_ref, o_ref, t2_ref, *, seq):
    h = pl.program_id(0)
    n_c = 2 * NA_KW - 1
    n_r = 2 * NA_KH - 1
    shape = (GRID_W, LANES)
    lane = lax.broadcasted_iota(jnp.int32, shape, 1)
    q_col = lax.broadcasted_iota(jnp.int32, shape, 0)
    ic = jnp.clip((lane & (GRID_W - 1)) - q_col + (NA_KW - 1), 0, n_c - 1)
    left = lane < GRID_W
    base = h * (n_r * n_c)
    for i in range(NA_T2):
        i_l = min(max(i - 1, 0), n_r - 1)
        i_r = min(max(i, 0), n_r - 1)

        acc = jnp.zeros(shape, F32)
        for j in range(n_c):
            coef = jnp.where(left, rpb_ref[base + i_l * n_c + j], rpb_ref[base + i_r * n_c + j])
            acc = acc + jnp.where(ic == j, coef, 0.0)
        t2_ref[i] = acc

    rows = seq // GRID_W
    qr_blk = Q_BLOCK // GRID_W
    for c, blk in enumerate(_na_class_blocks(seq)):
        bs = min(max(blk * qr_blk - NA_KH // 2, 0), rows - NA_BAND_ROWS)
        for a in range(qr_blk):
            for r2 in range(NA_BAND_ROWS // 2):
                i1 = bs + 2 * r2 - (blk * qr_blk + a) + (NA_KH - 1)
                idx = min(max(i1, -1), NA_T2 - 2) + 1
                rs = slice(a * GRID_W, (a + 1) * GRID_W)
                cs = slice(r2 * LANES, (r2 + 1) * LANES)
                o_ref[c, rs, cs] = t2_ref[idx] + mask_ref[c, rs, cs]


def _rpb_tiles(rpb, mask_add, seq):
    heads = rpb.shape[0]
    return pl.pallas_call(
        functools.partial(_rpb_tiles_kernel, seq=seq),
        grid=(heads,),
        in_specs=[pl.BlockSpec(memory_space=pltpu.SMEM),
                  pl.BlockSpec(mask_add.shape, lambda h: (0, 0, 0))],
        out_specs=pl.BlockSpec((None,) + mask_add.shape, lambda h: (h, 0, 0, 0)),
        out_shape=jax.ShapeDtypeStruct((heads,) + mask_add.shape, F32),
        scratch_shapes=[pltpu.VMEM((NA_T2, GRID_W, LANES), F32)],
        compiler_params=_cparams(("parallel",)),
        name="na_rpb_tiles",
    )(rpb.reshape(-1), mask_add)


def _na_kernel(q_ref, k_ref, v_ref, bias_ref, o_ref):
    seq = q_ref.shape[0]
    rows = seq // GRID_W
    qr_blk = Q_BLOCK // GRID_W
    n_blk = seq // Q_BLOCK

    classes = _na_class_blocks(seq)

    def band(blk):
        k0 = min(max(blk * qr_blk - NA_KH // 2, 0), rows - NA_BAND_ROWS) * GRID_W
        return slice(k0, k0 + NA_BAND)

    def scores(blk):
        q = q_ref[blk * Q_BLOCK:(blk + 1) * Q_BLOCK, :]
        return lax.dot_general(q, k_ref[band(blk), :], (((1,), (1,)), ((), ())),
                               preferred_element_type=F32)

    pending = [scores(b) for b in range(NA_LOOKAHEAD)]
    for blk in range(n_blk):
        s = pending.pop(0)
        if blk + NA_LOOKAHEAD < n_blk:
            pending.append(scores(blk + NA_LOOKAHEAD))
        cls = classes.index(blk) if blk in classes else 2
        s = s * (HEAD_DIM ** -0.5) + bias_ref[cls]
        m = jnp.max(s, axis=1, keepdims=True)
        p = jnp.exp(s - m)
        l = jnp.sum(p, axis=1, keepdims=True)
        o = jnp.dot(p.astype(BF16), v_ref[band(blk), :], preferred_element_type=F32) / l
        o_ref[blk * Q_BLOCK:(blk + 1) * Q_BLOCK, :] = o.astype(o_ref.dtype)


def _na_mixer(proj3, bias):
    batch, seq, _ = proj3.shape
    heads = GROUP_HEADS

    def head_spec(col):
        return pl.BlockSpec((None, seq, HEAD_DIM), lambda b, h: (b, 0, col // HEAD_DIM + h))

    return pl.pallas_call(
        _na_kernel,
        grid=(batch, heads),
        in_specs=[head_spec(COL_D_Q), head_spec(COL_D_K), head_spec(COL_D_V),
                  pl.BlockSpec((None,) + bias.shape[1:], lambda b, h: (h, 0, 0, 0))],
        out_specs=pl.BlockSpec((None, seq, HEAD_DIM), lambda b, h: (b, 0, h)),
        out_shape=jax.ShapeDtypeStruct((batch, seq, GROUP_WIDTH), BF16),
        compiler_params=_cparams(("parallel", "parallel")),
        name="na_mixer",
    )(proj3, proj3, proj3, bias)


def kernel(x, norm_mix_pre, norm_mix_post, norm_ffn_pre, norm_ffn_post, w_in, w_out, diff_lambda, diff_subln, conv_dw, conv_dw_b, conv_ln_g, conv_ln_b, conv_pw, conv_pw_b, gqa_q_norm, gqa_k_norm, na_rpb, ffn_gate, ffn_up, ffn_down):
    batch, seq, d = x.shape
    depth = w_in.shape[0]
    tokens = batch * seq

    t = np.arange(seq)
    diff_tables = _rope_tables(t, t)
    axial_tables = _rope_tables(t // GRID_W, t % GRID_W)
    mask_add = _na_geometry(seq)

    def row(v):
        return v.reshape(1, -1)

    xf = x.reshape(tokens, d)
    h = _rmsnorm(xf, row(norm_mix_pre[0]), tm=1024)
    for l in range(depth):
        lam_init = 0.8 - 0.6 * math.exp(-0.3 * l)
        proj = _in_proj(h, w_in, l, tm=2048, tn=1024)
        proj3 = proj.reshape(batch, seq, IN_COLS)

        out_a = _flash_call(
            functools.partial(_diff_attn_kernel, tq=ATTN_TQ, tk=ATTN_TK, lam_init=lam_init),
            proj3, diff_tables, (diff_lambda[l], diff_subln[l].reshape(-1, 1)),
            n_groups=GROUP_HEADS, q_width=HEAD_DIM, q_col=COL_A_Q, k_col=COL_A_K, v_col=COL_A_V,
            out_width=HEAD_DIM, tq=ATTN_TQ, name="diff_attn")
        out_b = _conv_mixer(proj3, conv_dw[l], row(conv_dw_b[l]), row(conv_ln_g[l]), row(conv_ln_b[l]),
                            conv_pw[l].astype(BF16), row(conv_pw_b[l]), tm=512)
        out_c = _flash_call(
            functools.partial(_gqa_attn_kernel, tq=ATTN_TQ, tk=ATTN_TK),
            proj3, axial_tables, (row(gqa_q_norm[l]), row(gqa_k_norm[l])),
            n_groups=GQA_KV_HEADS, q_width=2 * HEAD_DIM, q_col=COL_C_Q, k_col=COL_C_K, v_col=COL_C_V,
            out_width=2 * HEAD_DIM, tq=ATTN_TQ, name="gqa_attn")
        out_d = _na_mixer(proj3, _rpb_tiles(na_rpb[l], mask_add, seq))

        parts = [o.reshape(tokens, GROUP_WIDTH) for o in (out_a, out_b, out_c, out_d)]
        xf, h = _out_proj(parts, w_out[l].astype(BF16), row(norm_mix_post[l]), xf,
                          row(norm_ffn_pre[l]), tm=512)

        act, w_down = _ffn_up(h, ffn_gate, ffn_up, ffn_down, l, tm=1024, tn=512)
        next_gain = row(norm_mix_pre[l + 1]) if l + 1 < depth else None
        xf, h = _ffn_down(act, w_down, row(norm_ffn_post[l]), xf, next_gain,
                          tm=512, tn=512)
    return xf.reshape(batch, seq, d)
```

```python
import functools
import math

import numpy as np
import jax
import jax.numpy as jnp
from jax import lax
from jax.experimental import pallas as pl
from jax.experimental.pallas import tpu as pltpu

D_MODEL = 2048
HEAD_DIM = 128
GROUP_HEADS = 4
GROUP_WIDTH = GROUP_HEADS * HEAD_DIM
DIFF_QK_DIM = HEAD_DIM // 2
CONV_WIDTH = 31
GQA_KV_HEADS = 2
NA_KH = 8
NA_KW = 16
GRID_W = 64
Q_BLOCK = 128
ROPE_THETA = 10000.0
EPS = 1e-6

COL_A_Q, COL_A_K, COL_A_V = 0, 512, 1024
COL_B_A, COL_B_G = 1536, 2048
COL_C_Q, COL_C_K, COL_C_V = 2560, 3072, 3328
COL_D_Q, COL_D_K, COL_D_V = 3584, 4096, 4608
IN_COLS = 5120

LANES = 128
VMEM_LIMIT = 56 * 1024 * 1024
ROW_CHUNK = 128
ATTN_TQ = 512
ATTN_TK = 512

NA_BAND_ROWS = 10
NA_BAND = NA_BAND_ROWS * GRID_W
NA_CLASSES = 5
NA_T2 = 16
NA_LOOKAHEAD = 2
CONV_HALO = 16

F32 = jnp.float32
BF16 = jnp.bfloat16


def _cparams(semantics):
    return pltpu.CompilerParams(dimension_semantics=semantics,
                                vmem_limit_bytes=VMEM_LIMIT)


def _rms(x):
    return x * lax.rsqrt(jnp.mean(x * x, axis=-1, keepdims=True) + EPS)


def _rmsnorm_kernel(x_ref, g_ref, h_ref):
    def body(c, _):
        r0 = pl.multiple_of(c * ROW_CHUNK, ROW_CHUNK)
        h_ref[pl.ds(r0, ROW_CHUNK), :] = (_rms(x_ref[pl.ds(r0, ROW_CHUNK), :]) * g_ref[...]).astype(h_ref.dtype)
        return 0

    lax.fori_loop(0, x_ref.shape[0] // ROW_CHUNK, body, 0)


def _rmsnorm(x, g, *, tm):
    m, d = x.shape
    return pl.pallas_call(
        _rmsnorm_kernel,
        grid=(m // tm,),
        in_specs=[pl.BlockSpec((tm, d), lambda i: (i, 0)),
                  pl.BlockSpec((1, d), lambda i: (0, 0))],
        out_specs=pl.BlockSpec((tm, d), lambda i: (i, 0)),
        out_shape=jax.ShapeDtypeStruct((m, d), BF16),
        compiler_params=_cparams(("parallel",)),
        name="rmsnorm",
    )(x, g)


def _cast_weight_tile(w_ref, wb_ref):
    def body(c, _):
        r0 = pl.multiple_of(c * ROW_CHUNK, ROW_CHUNK)
        wb_ref[pl.ds(r0, ROW_CHUNK), :] = w_ref[pl.ds(r0, ROW_CHUNK), :].astype(wb_ref.dtype)
        return 0

    lax.fori_loop(0, w_ref.shape[0] // ROW_CHUNK, body, 0)


def _in_proj_kernel(h_ref, w_ref, o_ref, wb_ref):
    @pl.when(pl.program_id(1) == 0)
    def _():
        _cast_weight_tile(w_ref, wb_ref)

    o_ref[...] = jnp.dot(h_ref[...], wb_ref[...], preferred_element_type=F32).astype(o_ref.dtype)


def _in_proj(h, w, layer, *, tm, tn):
    m, d = h.shape
    n = w.shape[2]
    return pl.pallas_call(
        _in_proj_kernel,
        grid=(n // tn, m // tm),
        in_specs=[pl.BlockSpec((tm, d), lambda j, i: (i, 0)),
                  pl.BlockSpec((None, d, tn), lambda j, i: (layer, 0, j))],
        out_specs=pl.BlockSpec((tm, tn), lambda j, i: (i, j)),
        out_shape=jax.ShapeDtypeStruct((m, n), BF16),
        scratch_shapes=[pltpu.VMEM((d, tn), BF16)],
        compiler_params=_cparams(("parallel", "arbitrary")),
        name="in_proj",
    )(h, w)


def _ffn_up_kernel(h_ref, wg_ref, wu_ref, wd_ref, o_ref, wdb_ref, wgb_ref, wub_ref):
    @pl.when(pl.program_id(1) == 0)
    def _():
        _cast_weight_tile(wg_ref, wgb_ref)
        _cast_weight_tile(wu_ref, wub_ref)

    wdb_ref[...] = wd_ref[...].astype(wdb_ref.dtype)

    h = h_ref[...]
    gate = jnp.dot(h, wgb_ref[...], preferred_element_type=F32)
    up = jnp.dot(h, wub_ref[...], preferred_element_type=F32)
    o_ref[...] = (gate * jax.nn.sigmoid(gate) * up).astype(o_ref.dtype)


def _ffn_up(h, wg, wu, wd, layer, *, tm, tn):
    m, d = h.shape
    n = wg.shape[2]
    n_i = m // tm
    slab = wd.shape[1] // ((n // tn) * n_i)
    assert slab * (n // tn) * n_i == wd.shape[1] and slab % BF16_ROWS == 0
    w_spec = pl.BlockSpec((None, d, tn), lambda j, i: (layer, 0, j))
    return pl.pallas_call(
        _ffn_up_kernel,
        grid=(n // tn, n_i),
        in_specs=[pl.BlockSpec((tm, d), lambda j, i: (i, 0)), w_spec, w_spec,
                  pl.BlockSpec((None, slab, wd.shape[2]), lambda j, i: (layer, j * n_i + i, 0))],
        out_specs=[pl.BlockSpec((tm, tn), lambda j, i: (i, j)),
                   pl.BlockSpec((slab, wd.shape[2]), lambda j, i: (j * n_i + i, 0))],
        out_shape=[jax.ShapeDtypeStruct((m, n), BF16),
                   jax.ShapeDtypeStruct(wd.shape[1:], BF16)],
        scratch_shapes=[pltpu.VMEM((d, tn), BF16), pltpu.VMEM((d, tn), BF16)],
        compiler_params=_cparams(("parallel", "arbitrary")),
        name="ffn_up",
    )(h, wg, wu, wd)


def _ffn_down_kernel(*refs, emit_next):
    if emit_next:
        a_ref, w_ref, g_ref, x_ref, gn_ref, o_ref, hn_ref, f_ref = refs
    else:
        a_ref, w_ref, g_ref, x_ref, o_ref, f_ref = refs
        gn_ref = hn_ref = None
    j = pl.program_id(1)
    n_col, _, tn = f_ref.shape
    f_ref[j] = jnp.dot(a_ref[...], w_ref[...], preferred_element_type=F32)

    @pl.when(j == n_col - 1)
    def _():
        d = n_col * tn

        def body(c, _):
            r0 = pl.multiple_of(c * ROW_CHUNK, ROW_CHUNK)
            rows = pl.ds(r0, ROW_CHUNK)
            f = [f_ref[b, rows, :] for b in range(n_col)]
            r = lax.rsqrt(sum(jnp.sum(fb * fb, axis=-1, keepdims=True) for fb in f) * (1.0 / d) + EPS)
            o = [x_ref[rows, b * tn:(b + 1) * tn] + f[b] * r * g_ref[:, b * tn:(b + 1) * tn]
                 for b in range(n_col)]
            for b in range(n_col):
                o_ref[rows, b * tn:(b + 1) * tn] = o[b]
            if hn_ref is not None:
                r = lax.rsqrt(sum(jnp.sum(ob * ob, axis=-1, keepdims=True) for ob in o) * (1.0 / d) + EPS)
                for b in range(n_col):
                    hn_ref[rows, b * tn:(b + 1) * tn] = (
                        o[b] * r * gn_ref[:, b * tn:(b + 1) * tn]).astype(hn_ref.dtype)
            return 0

        lax.fori_loop(0, f_ref.shape[1] // ROW_CHUNK, body, 0)


def _ffn_down(a, w, g, x, gn, *, tm, tn):
    m, kdim = a.shape
    d = w.shape[1]
    emit_next = gn is not None
    vec = pl.BlockSpec((1, d), lambda i, j: (0, 0))
    row_tile = pl.BlockSpec((tm, d), lambda i, j: (i, 0))
    outs = pl.pallas_call(
        functools.partial(_ffn_down_kernel, emit_next=emit_next),
        grid=(m // tm, d // tn),
        in_specs=[pl.BlockSpec((tm, kdim), lambda i, j: (i, 0)),
                  pl.BlockSpec((kdim, tn), lambda i, j: (0, j)),
                  vec, row_tile] + [vec] * emit_next,
        out_specs=[row_tile] + [row_tile] * emit_next,
        out_shape=[jax.ShapeDtypeStruct((m, d), F32)] + [jax.ShapeDtypeStruct((m, d), BF16)] * emit_next,
        scratch_shapes=[pltpu.VMEM((d // tn, tm, tn), F32)],
        compiler_params=_cparams(("parallel", "arbitrary")),
        name="ffn_down",
    )(a, w, g, x, *([gn] * emit_next))
    return (outs[0], outs[1]) if emit_next else (outs[0], None)


def _out_proj_kernel(a_ref, b_ref, c_ref, d_ref, w_ref, g_ref, x_ref, gn_ref, o_ref, hn_ref,
                     lhs_ref, acc_ref):
    for p, part in enumerate((a_ref, b_ref, c_ref, d_ref)):
        lhs_ref[:, p * GROUP_WIDTH:(p + 1) * GROUP_WIDTH] = part[...]
    n_half, half, _ = acc_ref.shape
    for b in range(n_half):
        acc_ref[b] = jnp.dot(lhs_ref[b * half:(b + 1) * half, :], w_ref[...],
                             preferred_element_type=F32)
    for b in range(n_half):
        for c in range(half // ROW_CHUNK):
            rows = slice(b * half + c * ROW_CHUNK, b * half + (c + 1) * ROW_CHUNK)
            o = x_ref[rows, :] + _rms(acc_ref[b, c * ROW_CHUNK:(c + 1) * ROW_CHUNK, :]) * g_ref[...]
            o_ref[rows, :] = o
            hn_ref[rows, :] = (_rms(o) * gn_ref[...]).astype(hn_ref.dtype)


def _out_proj(parts, w, g, x, gn, *, tm):
    m, d = x.shape
    part_spec = pl.BlockSpec((tm, GROUP_WIDTH), lambda i: (i, 0))
    vec = pl.BlockSpec((1, d), lambda i: (0, 0))
    row_tile = pl.BlockSpec((tm, d), lambda i: (i, 0))
    return pl.pallas_call(
        _out_proj_kernel,
        grid=(m // tm,),
        in_specs=[part_spec, part_spec, part_spec, part_spec,
                  pl.BlockSpec((4 * GROUP_WIDTH, d), lambda i: (0, 0)),
                  vec, row_tile, vec],
        out_specs=[row_tile, row_tile],
        out_shape=[jax.ShapeDtypeStruct((m, d), F32), jax.ShapeDtypeStruct((m, d), BF16)],
        scratch_shapes=[pltpu.VMEM((tm, 4 * GROUP_WIDTH), BF16),
                        pltpu.VMEM((2, tm // 2, d), F32)],
        compiler_params=_cparams(("parallel",)),
        name="out_proj",
    )(*parts, w, g, x, gn)


def _rope_tables(pos_lo, pos_hi):
    half = DIFF_QK_DIM // 2
    inv = ROPE_THETA ** (-np.arange(half, dtype=np.float64) / half)
    lane = np.arange(LANES)
    pos = np.where(lane[None, :] < DIFF_QK_DIM, pos_lo[:, None], pos_hi[:, None]).astype(np.float64)
    ang = pos * inv[lane % half][None, :]
    sign = np.where((lane % DIFF_QK_DIM) < half, -1.0, 1.0)
    return (jnp.asarray(np.cos(ang), dtype=F32),
            jnp.asarray(np.sin(ang) * sign[None, :], dtype=F32))


def _rope(x, cos, sin_signed):
    lane = lax.broadcasted_iota(jnp.int32, x.shape, 1)
    first_half = (lane & (DIFF_QK_DIM // 2)) == 0
    partner = jnp.where(first_half, pltpu.roll(x, LANES - DIFF_QK_DIM // 2, 1),
                        pltpu.roll(x, DIFF_QK_DIM // 2, 1))
    return x * cos + partner * sin_signed


LOG2E = math.log2(math.e)


ATTN_OFFSET_SLACK = 64.0
PREP_UNROLL = 4
SUBLANES = 8
BF16_ROWS = 16


def _scores(krot_ref, qs_ref, j, tk):
    k = krot_ref[j * tk:(j + 1) * tk, :]
    return lax.dot_general(k, qs_ref[...], (((1,), (1,)), ((), ())), preferred_element_type=F32)


def _attend_online(qs_ref, krot_ref, vt_ref, acc_ref, *, tk):
    n_tiles = krot_ref.shape[0] // tk
    m = l = None
    s_next = _scores(krot_ref, qs_ref, 0, tk)
    for j in range(n_tiles):
        s = s_next
        if j + 1 < n_tiles:
            s_next = _scores(krot_ref, qs_ref, j + 1, tk)
        m_cur = jnp.max(s, axis=0, keepdims=True)
        m_next = m_cur if m is None else jnp.maximum(m, m_cur)
        p = jnp.exp2(s - m_next)
        pv = jnp.dot(vt_ref[:, j * tk:(j + 1) * tk], p.astype(BF16), preferred_element_type=F32)
        if m is None:
            l = jnp.sum(p, axis=0, keepdims=True)
            acc_ref[...] = pv
        else:
            alpha = jnp.exp2(m - m_next)
            l = alpha * l + jnp.sum(p, axis=0, keepdims=True)
            acc_ref[...] = alpha * acc_ref[...] + pv
        m = m_next
    return l


def _attend_fixed(qs_ref, krot_ref, vt_ref, acc_ref, offset, *, tk):
    n_tiles = krot_ref.shape[0] // tk
    l = None
    s_next = _scores(krot_ref, qs_ref, 0, tk)
    for j in range(n_tiles):
        s = s_next
        if j + 1 < n_tiles:
            s_next = _scores(krot_ref, qs_ref, j + 1, tk)
        p = jnp.exp2(s - offset)
        pv = jnp.dot(vt_ref[:, j * tk:(j + 1) * tk], p.astype(BF16), preferred_element_type=F32)
        if l is None:
            l = jnp.sum(p, axis=0, keepdims=True)
            acc_ref[...] = pv
        else:
            l = l + jnp.sum(p, axis=0, keepdims=True)
            acc_ref[...] += pv
    return l


def _fixed_offset_is_safe(q_norm2_max, kstat_ref):
    worst_exponent = jnp.sqrt(q_norm2_max) * kstat_ref[1:2, 0:1]
    return (jnp.max(worst_exponent) <= ATTN_OFFSET_SLACK).astype(jnp.int32)


def _attend(qs_ref, fixed_ok_ref, krot_ref, vt_ref, kstat_ref, acc_ref, finish, *, tk):
    use_fixed = fixed_ok_ref[0] == 1

    @pl.when(use_fixed)
    def _():
        qf = qs_ref[...].astype(F32)
        ones = jnp.ones((BF16_ROWS, qf.shape[1]), BF16)
        q_norm2 = lax.dot_general(ones, (qf * qf).astype(BF16), (((1,), (1,)), ((), ())),
                                  preferred_element_type=F32)
        offset = jnp.sqrt(q_norm2[0:1]) * kstat_ref[0:1, 0:1]
        finish(_attend_fixed(qs_ref, krot_ref, vt_ref, acc_ref, offset, tk=tk))

    @pl.when(jnp.logical_not(use_fixed))
    def _():
        finish(_attend_online(qs_ref, krot_ref, vt_ref, acc_ref, tk=tk))


def _rms_gain_norm2_bound(gain):
    return gain.shape[1] * jnp.max(gain * gain, axis=1, keepdims=True)


def _prepare_keys_values(k_ref, v_ref, cos_ref, sin_ref, krot_ref, vt_ref, kstat_ref, gain):
    seq = k_ref.shape[0]

    def body(c, carry):
        k_sum, k_max2 = carry
        r0 = pl.multiple_of(c * ROW_CHUNK, ROW_CHUNK)
        k = k_ref[pl.ds(r0, ROW_CHUNK), :].astype(F32)
        if gain is not None:
            k = _rms(k) * gain
        kb = _rope(k, cos_ref[pl.ds(r0, ROW_CHUNK), :], sin_ref[pl.ds(r0, ROW_CHUNK), :]).astype(BF16)
        krot_ref[pl.ds(r0, ROW_CHUNK), :] = kb
        vt_ref[:, pl.ds(r0, ROW_CHUNK)] = v_ref[pl.ds(r0, ROW_CHUNK), :].astype(F32).T.astype(BF16)
        kf = kb.astype(F32)
        if gain is None:
            norm2 = jnp.sum(kf * kf, axis=1, keepdims=True)
            k_max2 = jnp.maximum(k_max2, jnp.max(norm2, axis=0, keepdims=True))
        return k_sum + jnp.sum(kf, axis=0, keepdims=True), k_max2

    k_max2_init = (jnp.zeros((1, 1), F32) if gain is None
                   else _rms_gain_norm2_bound(gain))
    k_sum, k_max2 = lax.fori_loop(0, seq // ROW_CHUNK, body,
                                  (jnp.zeros((1, k_ref.shape[1]), F32), k_max2_init),
                                  unroll=PREP_UNROLL)
    k_bar = k_sum * (1.0 / seq)
    k_bar_norm = jnp.sqrt(jnp.sum(k_bar * k_bar, axis=1, keepdims=True))
    k_max = jnp.sqrt(k_max2) * 1.01
    kstat_ref[0:1, :] = jnp.broadcast_to(k_max, (1, kstat_ref.shape[1]))
    kstat_ref[1:2, :] = jnp.broadcast_to(k_max + k_bar_norm, (1, kstat_ref.shape[1]))


def _stacked_row(c, tq):
    per_block = tq // ROW_CHUNK
    return pl.multiple_of((c // per_block) * (2 * tq) + (c % per_block) * ROW_CHUNK, ROW_CHUNK)


def _diff_attn_kernel(q_ref, k_ref, v_ref, cos_ref, sin_ref, lam_ref, g_ref, o_ref,
                      krot_ref, vt_ref, kstat_ref, qs_ref, fixed_ok_ref, acc_ref, *, tq, tk, lam_init):
    @pl.when(pl.program_id(1) == 0)
    def _():
        _prepare_keys_values(k_ref, v_ref, cos_ref, sin_ref, krot_ref, vt_ref, kstat_ref, None)

        def body(c, q_norm2_max):
            rows = pl.ds(pl.multiple_of(c * ROW_CHUNK, ROW_CHUNK), ROW_CHUNK)
            q = _rope(q_ref[rows, :].astype(F32), cos_ref[rows, :], sin_ref[rows, :]) * (
                DIFF_QK_DIM ** -0.5 * LOG2E)
            lane = lax.broadcasted_iota(jnp.int32, q.shape, 1)
            dst = _stacked_row(c, tq)
            qs_ref[pl.ds(dst, ROW_CHUNK), :] = jnp.where(lane < DIFF_QK_DIM, q, 0.0).astype(BF16)
            qs_ref[pl.ds(dst + tq, ROW_CHUNK), :] = jnp.where(lane >= DIFF_QK_DIM, q, 0.0).astype(BF16)
            return jnp.maximum(q_norm2_max,
                               jnp.max(jnp.sum(q * q, axis=1, keepdims=True), axis=0, keepdims=True))

        q_norm2_max = lax.fori_loop(0, q_ref.shape[0] // ROW_CHUNK, body, jnp.zeros((1, 1), F32),
                                    unroll=PREP_UNROLL)
        fixed_ok_ref[0] = _fixed_offset_is_safe(q_norm2_max, kstat_ref)

    qs_blk = qs_ref.at[pl.ds(pl.multiple_of(pl.program_id(1) * (2 * tq), 2 * tq), 2 * tq), :]

    def finish(l):
        lp = lam_ref[...]
        lam = (jnp.exp(jnp.sum(lp[0:1] * lp[1:2], keepdims=True))
               - jnp.exp(jnp.sum(lp[2:3] * lp[3:4], keepdims=True)) + lam_init)
        ot = (acc_ref[:, 0:tq] / l[:, 0:tq] - lam * (acc_ref[:, tq:2 * tq] / l[:, tq:2 * tq]))
        ms = jnp.mean(ot * ot, axis=0, keepdims=True)
        ot = ot * lax.rsqrt(ms + EPS) * (g_ref[...] * (1.0 - lam_init))
        o_ref[...] = ot.T.astype(o_ref.dtype)

    _attend(qs_blk, fixed_ok_ref, krot_ref, vt_ref, kstat_ref, acc_ref, finish, tk=tk)


def _gqa_attn_kernel(q_ref, k_ref, v_ref, cos_ref, sin_ref, qn_ref, kn_ref, o_ref,
                     krot_ref, vt_ref, kstat_ref, qs_ref, fixed_ok_ref, acc_ref, *, tq, tk):
    @pl.when(pl.program_id(1) == 0)
    def _():
        _prepare_keys_values(k_ref, v_ref, cos_ref, sin_ref, krot_ref, vt_ref, kstat_ref, kn_ref[...])

        q_scale = HEAD_DIM ** -0.5 * LOG2E

        def body(c, _):
            rows = pl.ds(pl.multiple_of(c * ROW_CHUNK, ROW_CHUNK), ROW_CHUNK)
            dst = _stacked_row(c, tq)
            for r in range(2):
                q = q_ref[rows, r * HEAD_DIM:(r + 1) * HEAD_DIM].astype(F32)
                q = _rope(_rms(q) * qn_ref[...], cos_ref[rows, :], sin_ref[rows, :]) * q_scale
                qs_ref[pl.ds(dst + r * tq, ROW_CHUNK), :] = q.astype(BF16)
            return 0

        lax.fori_loop(0, q_ref.shape[0] // ROW_CHUNK, body, 0, unroll=PREP_UNROLL)
        q_norm2_max = _rms_gain_norm2_bound(qn_ref[...]) * (q_scale * q_scale)
        fixed_ok_ref[0] = _fixed_offset_is_safe(q_norm2_max, kstat_ref)

    qs_blk = qs_ref.at[pl.ds(pl.multiple_of(pl.program_id(1) * (2 * tq), 2 * tq), 2 * tq), :]

    def finish(l):
        for r in range(2):
            ot = acc_ref[:, r * tq:(r + 1) * tq] / l[:, r * tq:(r + 1) * tq]
            o_ref[:, r * HEAD_DIM:(r + 1) * HEAD_DIM] = ot.T.astype(o_ref.dtype)

    _attend(qs_blk, fixed_ok_ref, krot_ref, vt_ref, kstat_ref, acc_ref, finish, tk=tk)


def _flash_call(kernel, proj3, tables, params, *, n_groups, q_width, q_col, k_col, v_col,
                out_width, tq, name):
    batch, seq, _ = proj3.shape
    cos_t, sin_t = tables
    qb, kb, vb = q_col // q_width, k_col // HEAD_DIM, v_col // HEAD_DIM

    def grp(i):
        return i // n_groups, i % n_groups

    in_specs = [
        pl.BlockSpec((None, seq, q_width), lambda i, j: (grp(i)[0], 0, qb + grp(i)[1])),
        pl.BlockSpec((None, seq, HEAD_DIM), lambda i, j: (grp(i)[0], 0, kb + grp(i)[1])),
        pl.BlockSpec((None, seq, HEAD_DIM), lambda i, j: (grp(i)[0], 0, vb + grp(i)[1])),
        pl.BlockSpec((seq, LANES), lambda i, j: (0, 0)),
        pl.BlockSpec((seq, LANES), lambda i, j: (0, 0)),
    ] + [pl.BlockSpec(p.shape, lambda i, j: (0, 0)) for p in params]
    return pl.pallas_call(
        kernel,
        grid=(batch * n_groups, seq // tq),
        in_specs=in_specs,
        out_specs=pl.BlockSpec((None, tq, out_width), lambda i, j: (grp(i)[0], j, grp(i)[1])),
        out_shape=jax.ShapeDtypeStruct((batch, seq, GROUP_WIDTH), BF16),
        scratch_shapes=[pltpu.VMEM((seq, HEAD_DIM), BF16),
                        pltpu.VMEM((HEAD_DIM, seq), BF16),
                        pltpu.VMEM((SUBLANES, LANES), F32),
                        pltpu.VMEM((2 * seq, HEAD_DIM), BF16),
                        pltpu.SMEM((1,), jnp.int32),
                        pltpu.VMEM((HEAD_DIM, 2 * tq), F32)],
        compiler_params=_cparams(("parallel", "arbitrary")),
        name=name,
    )(proj3, proj3, proj3, cos_t, sin_t, *params)


def _conv_kernel(a_ref, g_ref, dw_ref, dwb_ref, lng_ref, lnb_ref, pw_ref, pwb_ref, o_ref,
                 u_ref, acc_ref, y_ref, *, tm):
    seq = a_ref.shape[0]
    t = pl.program_id(1)

    @pl.when(t == 0)
    def _():
        zeros = jnp.zeros((CONV_HALO, u_ref.shape[1]), F32)
        u_ref[0:CONV_HALO, :] = zeros
        u_ref[CONV_HALO + seq:2 * CONV_HALO + seq, :] = zeros

        def body(c, _):
            r0 = pl.multiple_of(c * ROW_CHUNK, ROW_CHUNK)
            a = a_ref[pl.ds(r0, ROW_CHUNK), :].astype(F32)
            g = g_ref[pl.ds(r0, ROW_CHUNK), :].astype(F32)
            u_ref[pl.ds(CONV_HALO + r0, ROW_CHUNK), :] = a * jax.nn.sigmoid(g)
            return 0

        lax.fori_loop(0, seq // ROW_CHUNK, body, 0)

    pad = CONV_WIDTH // 2
    sub = 8
    n_ch = u_ref.shape[1]

    def row_block(rb, _):
        base = pl.multiple_of(t * tm + rb * ROW_CHUNK, ROW_CHUNK)
        for cg in range(n_ch // LANES):
            cols = slice(cg * LANES, (cg + 1) * LANES)
            win = u_ref[pl.ds(base, ROW_CHUNK + 2 * CONV_HALO), cols]
            acc = jnp.zeros((ROW_CHUNK, LANES), F32) + dwb_ref[:, cols]
            for r in range(sub):
                shifted = win if r == 0 else pltpu.roll(win, win.shape[0] - r, 0)
                for a in range(2 * CONV_HALO // sub):
                    k = sub * a + r - (CONV_HALO - pad)
                    if 0 <= k < CONV_WIDTH:
                        acc = acc + shifted[sub * a:sub * a + ROW_CHUNK, :] * dw_ref[k:k + 1, cols]
            acc_ref[:, cols] = acc
        acc = acc_ref[...]
        mu = jnp.mean(acc, axis=-1, keepdims=True)
        cen = acc - mu
        var = jnp.mean(cen * cen, axis=-1, keepdims=True)
        y = cen * lax.rsqrt(var + EPS) * lng_ref[...] + lnb_ref[...]
        y_ref[pl.ds(pl.multiple_of(rb * ROW_CHUNK, ROW_CHUNK), ROW_CHUNK), :] = (
            y * jax.nn.sigmoid(y)).astype(BF16)
        return 0

    lax.fori_loop(0, tm // ROW_CHUNK, row_block, 0)
    o_ref[...] = (jnp.dot(y_ref[...], pw_ref[...], preferred_element_type=F32)
                  + pwb_ref[...]).astype(o_ref.dtype)


def _conv_mixer(proj3, dw, dwb, lng, lnb, pw, pwb, *, tm):
    batch, seq, _ = proj3.shape
    ch = GROUP_WIDTH
    vec = pl.BlockSpec((1, ch), lambda b, t: (0, 0))
    return pl.pallas_call(
        functools.partial(_conv_kernel, tm=tm),
        grid=(batch, seq // tm),
        in_specs=[pl.BlockSpec((None, seq, ch), lambda b, t: (b, 0, COL_B_A // ch)),
                  pl.BlockSpec((None, seq, ch), lambda b, t: (b, 0, COL_B_G // ch)),
                  pl.BlockSpec((CONV_WIDTH, ch), lambda b, t: (0, 0)),
                  vec, vec, vec,
                  pl.BlockSpec((ch, ch), lambda b, t: (0, 0)),
                  vec],
        out_specs=pl.BlockSpec((None, tm, ch), lambda b, t: (b, t, 0)),
        out_shape=jax.ShapeDtypeStruct((batch, seq, ch), BF16),
        scratch_shapes=[pltpu.VMEM((seq + 2 * CONV_HALO, ch), F32),
                        pltpu.VMEM((ROW_CHUNK, ch), F32),
                        pltpu.VMEM((tm, ch), BF16)],
        compiler_params=_cparams(("parallel", "arbitrary")),
        name="conv_mixer",
    )(proj3, proj3, dw, dwb, lng, lnb, pw, pwb)


def _na_geometry(seq):
    rows = seq // GRID_W
    qr_blk = Q_BLOCK // GRID_W
    mask_add = np.zeros((NA_CLASSES, Q_BLOCK, NA_BAND), np.float32)
    q_local = np.arange(Q_BLOCK)
    p = np.arange(NA_BAND)
    for c, blk in enumerate(_na_class_blocks(seq)):
        bs = int(np.clip(blk * qr_blk - NA_KH // 2, 0, rows - NA_BAND_ROWS))
        q_row = (blk * qr_blk + q_local // GRID_W)[:, None]
        q_col = (q_local % GRID_W)[:, None]
        k_row = (bs + p // GRID_W)[None, :]
        k_col = (p % GRID_W)[None, :]
        win_r = np.clip(q_row - NA_KH // 2, 0, rows - NA_KH)
        win_c = np.clip(q_col - NA_KW // 2, 0, GRID_W - NA_KW)
        ok = (k_row >= win_r) & (k_row < win_r + NA_KH) & (k_col >= win_c) & (k_col < win_c + NA_KW)
        mask_add[c] = np.where(ok, 0.0, -1e30)
    return jnp.asarray(mask_add)


def _na_class_blocks(seq):
    n_blk = seq // Q_BLOCK
    return (0, 1, 2, n_blk - 2, n_blk - 1)


def _rpb_tiles_kernel(rpb_ref, mask_ref, o_ref, t2_ref, *, seq):
    h = pl.program_id(0)
    n_c = 2 * NA_KW - 1
    n_r = 2 * NA_KH - 1
    shape = (GRID_W, LANES)
    lane = lax.broadcasted_iota(jnp.int32, shape, 1)
    q_col = lax.broadcasted_iota(jnp.int32, shape, 0)
    ic = jnp.clip((lane & (GRID_W - 1)) - q_col + (NA_KW - 1), 0, n_c - 1)
    left = lane < GRID_W
    base = h * (n_r * n_c)
    for i in range(NA_T2):
        i_l = min(max(i - 1, 0), n_r - 1)
        i_r = min(max(i, 0), n_r - 1)

        acc = jnp.zeros(shape, F32)
        for j in range(n_c):
            coef = jnp.where(left, rpb_ref[base + i_l * n_c + j], rpb_ref[base + i_r * n_c + j])
            acc = acc + jnp.where(ic == j, coef, 0.0)
        t2_ref[i] = acc

    rows = seq // GRID_W
    qr_blk = Q_BLOCK // GRID_W
    for c, blk in enumerate(_na_class_blocks(seq)):
        bs = min(max(blk * qr_blk - NA_KH // 2, 0), rows - NA_BAND_ROWS)
        for a in range(qr_blk):
            for r2 in range(NA_BAND_ROWS // 2):
                i1 = bs + 2 * r2 - (blk * qr_blk + a) + (NA_KH - 1)
                idx = min(max(i1, -1), NA_T2 - 2) + 1
                rs = slice(a * GRID_W, (a + 1) * GRID_W)
                cs = slice(r2 * LANES, (r2 + 1) * LANES)
                o_ref[c, rs, cs] = t2_ref[idx] + mask_ref[c, rs, cs]


def _rpb_tiles(rpb, mask_add, seq):
    heads = rpb.shape[0]
    return pl.pallas_call(
        functools.partial(_rpb_tiles_kernel, seq=seq),
        grid=(heads,),
        in_specs=[pl.BlockSpec(memory_space=pltpu.SMEM),
                  pl.BlockSpec(mask_add.shape, lambda h: (0, 0, 0))],
        out_specs=pl.BlockSpec((None,) + mask_add.shape, lambda h: (h, 0, 0, 0)),
        out_shape=jax.ShapeDtypeStruct((heads,) + mask_add.shape, F32),
        scratch_shapes=[pltpu.VMEM((NA_T2, GRID_W, LANES), F32)],
        compiler_params=_cparams(("parallel",)),
        name="na_rpb_tiles",
    )(rpb.reshape(-1), mask_add)


def _na_kernel(q_ref, k_ref, v_ref, bias_ref, o_ref):
    seq = q_ref.shape[0]
    rows = seq // GRID_W
    qr_blk = Q_BLOCK // GRID_W
    n_blk = seq // Q_BLOCK

    classes = _na_class_blocks(seq)

    def band(blk):
        k0 = min(max(blk * qr_blk - NA_KH // 2, 0), rows - NA_BAND_ROWS) * GRID_W
        return slice(k0, k0 + NA_BAND)

    def scores(blk):
        q = q_ref[blk * Q_BLOCK:(blk + 1) * Q_BLOCK, :]
        return lax.dot_general(q, k_ref[band(blk), :], (((1,), (1,)), ((), ())),
                               preferred_element_type=F32)

    pending = [scores(b) for b in range(NA_LOOKAHEAD)]
    for blk in range(n_blk):
        s = pending.pop(0)
        if blk + NA_LOOKAHEAD < n_blk:
            pending.append(scores(blk + NA_LOOKAHEAD))
        cls = classes.index(blk) if blk in classes else 2
        s = s * (HEAD_DIM ** -0.5) + bias_ref[cls]
        m = jnp.max(s, axis=1, keepdims=True)
        p = jnp.exp(s - m)
        l = jnp.sum(p, axis=1, keepdims=True)
        o = jnp.dot(p.astype(BF16), v_ref[band(blk), :], preferred_element_type=F32) / l
        o_ref[blk * Q_BLOCK:(blk + 1) * Q_BLOCK, :] = o.astype(o_ref.dtype)


def _na_mixer(proj3, bias):
    batch, seq, _ = proj3.shape
    heads = GROUP_HEADS

    def head_spec(col):
        return pl.BlockSpec((None, seq, HEAD_DIM), lambda b, h: (b, 0, col // HEAD_DIM + h))

    return pl.pallas_call(
        _na_kernel,
        grid=(batch, heads),
        in_specs=[head_spec(COL_D_Q), head_spec(COL_D_K), head_spec(COL_D_V),
                  pl.BlockSpec((None,) + bias.shape[1:], lambda b, h: (h, 0, 0, 0))],
        out_specs=pl.BlockSpec((None, seq, HEAD_DIM), lambda b, h: (b, 0, h)),
        out_shape=jax.ShapeDtypeStruct((batch, seq, GROUP_WIDTH), BF16),
        compiler_params=_cparams(("parallel", "parallel")),
        name="na_mixer",
    )(proj3, proj3, proj3, bias)


def kernel(x, norm_mix_pre, norm_mix_post, norm_ffn_pre, norm_ffn_post, w_in, w_out, diff_lambda, diff_subln, conv_dw, conv_dw_b, conv_ln_g, conv_ln_b, conv_pw, conv_pw_b, gqa_q_norm, gqa_k_norm, na_rpb, ffn_gate, ffn_up, ffn_down):
    batch, seq, d = x.shape
    depth = w_in.shape[0]
    tokens = batch * seq

    t = np.arange(seq)
    diff_tables = _rope_tables(t, t)
    axial_tables = _rope_tables(t // GRID_W, t % GRID_W)
    mask_add = _na_geometry(seq)

    def row(v):
        return v.reshape(1, -1)

    xf = x.reshape(tokens, d)
    h = _rmsnorm(xf, row(norm_mix_pre[0]), tm=1024)
    for l in range(depth):
        lam_init = 0.8 - 0.6 * math.exp(-0.3 * l)
        proj = _in_proj(h, w_in, l, tm=2048, tn=1024)
        proj3 = proj.reshape(batch, seq, IN_COLS)

        out_a = _flash_call(
            functools.partial(_diff_attn_kernel, tq=ATTN_TQ, tk=ATTN_TK, lam_init=lam_init),
            proj3, diff_tables, (diff_lambda[l], diff_subln[l].reshape(-1, 1)),
            n_groups=GROUP_HEADS, q_width=HEAD_DIM, q_col=COL_A_Q, k_col=COL_A_K, v_col=COL_A_V,
            out_width=HEAD_DIM, tq=ATTN_TQ, name="diff_attn")
        out_b = _conv_mixer(proj3, conv_dw[l], row(conv_dw_b[l]), row(conv_ln_g[l]), row(conv_ln_b[l]),
                            conv_pw[l].astype(BF16), row(conv_pw_b[l]), tm=512)
        out_c = _flash_call(
            functools.partial(_gqa_attn_kernel, tq=ATTN_TQ, tk=ATTN_TK),
            proj3, axial_tables, (row(gqa_q_norm[l]), row(gqa_k_norm[l])),
            n_groups=GQA_KV_HEADS, q_width=2 * HEAD_DIM, q_col=COL_C_Q, k_col=COL_C_K, v_col=COL_C_V,
            out_width=2 * HEAD_DIM, tq=ATTN_TQ, name="gqa_attn")
        out_d = _na_mixer(proj3, _rpb_tiles(na_rpb[l], mask_add, seq))

        parts = [o.reshape(tokens, GROUP_WIDTH) for o in (out_a, out_b, out_c, out_d)]
        xf, h = _out_proj(parts, w_out[l].astype(BF16), row(norm_mix_post[l]), xf,
                          row(norm_ffn_pre[l]), tm=512)

        act, w_down = _ffn_up(h, ffn_gate, ffn_up, ffn_down, l, tm=1024, tn=512)
        next_gain = row(norm_mix_pre[l + 1]) if l + 1 < depth else None
        xf, h = _ffn_down(act, w_down, row(norm_ffn_post[l]), xf, next_gain,
                          tm=512, tn=512)
    return xf.reshape(batch, seq, d)
```

```python
import functools
import math

import numpy as np
import jax
import jax.numpy as jnp
from jax import lax
from jax.experimental import pallas as pl
from jax.experimental.pallas import tpu as pltpu

D_MODEL = 2048
HEAD_DIM = 128
GROUP_HEADS = 4
GROUP_WIDTH = GROUP_HEADS * HEAD_DIM
DIFF_QK_DIM = HEAD_DIM // 2
CONV_WIDTH = 31
GQA_KV_HEADS = 2
NA_KH = 8
NA_KW = 16
GRID_W = 64
Q_BLOCK = 128
ROPE_THETA = 10000.0
EPS = 1e-6

COL_A_Q, COL_A_K, COL_A_V = 0, 512, 1024
COL_B_A, COL_B_G = 1536, 2048
COL_C_Q, COL_C_K, COL_C_V = 2560, 3072, 3328
COL_D_Q, COL_D_K, COL_D_V = 3584, 4096, 4608
IN_COLS = 5120

LANES = 128
VMEM_LIMIT = 56 * 1024 * 1024
ROW_CHUNK = 128
ATTN_TQ = 512
ATTN_TK = 512

NA_BAND_ROWS = 10
NA_BAND = NA_BAND_ROWS * GRID_W
NA_CLASSES = 5
NA_T2 = 16
NA_LOOKAHEAD = 2
CONV_HALO = 16

F32 = jnp.float32
BF16 = jnp.bfloat16


def _cparams(semantics):
    return pltpu.CompilerParams(dimension_semantics=semantics,
                                vmem_limit_bytes=VMEM_LIMIT)


def _rms(x):
    return x * lax.rsqrt(jnp.mean(x * x, axis=-1, keepdims=True) + EPS)


def _rmsnorm_kernel(x_ref, g_ref, h_ref):
    def body(c, _):
        r0 = pl.multiple_of(c * ROW_CHUNK, ROW_CHUNK)
        h_ref[pl.ds(r0, ROW_CHUNK), :] = (_rms(x_ref[pl.ds(r0, ROW_CHUNK), :]) * g_ref[...]).astype(h_ref.dtype)
        return 0

    lax.fori_loop(0, x_ref.shape[0] // ROW_CHUNK, body, 0)


def _rmsnorm(x, g, *, tm):
    m, d = x.shape
    return pl.pallas_call(
        _rmsnorm_kernel,
        grid=(m // tm,),
        in_specs=[pl.BlockSpec((tm, d), lambda i: (i, 0)),
                  pl.BlockSpec((1, d), lambda i: (0, 0))],
        out_specs=pl.BlockSpec((tm, d), lambda i: (i, 0)),
        out_shape=jax.ShapeDtypeStruct((m, d), BF16),
        compiler_params=_cparams(("parallel",)),
        name="rmsnorm",
    )(x, g)


def _cast_weight_tile(w_ref, wb_ref):
    def body(c, _):
        r0 = pl.multiple_of(c * ROW_CHUNK, ROW_CHUNK)
        wb_ref[pl.ds(r0, ROW_CHUNK), :] = w_ref[pl.ds(r0, ROW_CHUNK), :].astype(wb_ref.dtype)
        return 0

    lax.fori_loop(0, w_ref.shape[0] // ROW_CHUNK, body, 0)


def _in_proj_kernel(h_ref, w_ref, o_ref, wb_ref):
    @pl.when(pl.program_id(1) == 0)
    def _():
        _cast_weight_tile(w_ref, wb_ref)

    o_ref[...] = jnp.dot(h_ref[...], wb_ref[...], preferred_element_type=F32).astype(o_ref.dtype)


def _in_proj(h, w, layer, *, tm, tn):
    m, d = h.shape
    n = w.shape[2]
    return pl.pallas_call(
        _in_proj_kernel,
        grid=(n // tn, m // tm),
        in_specs=[pl.BlockSpec((tm, d), lambda j, i: (i, 0)),
                  pl.BlockSpec((None, d, tn), lambda j, i: (layer, 0, j))],
        out_specs=pl.BlockSpec((tm, tn), lambda j, i: (i, j)),
        out_shape=jax.ShapeDtypeStruct((m, n), BF16),
        scratch_shapes=[pltpu.VMEM((d, tn), BF16)],
        compiler_params=_cparams(("parallel", "arbitrary")),
        name="in_proj",
    )(h, w)


def _ffn_up_kernel(h_ref, wg_ref, wu_ref, wd_ref, o_ref, wdb_ref, wgb_ref, wub_ref):
    @pl.when(pl.program_id(1) == 0)
    def _():
        _cast_weight_tile(wg_ref, wgb_ref)
        _cast_weight_tile(wu_ref, wub_ref)

    wdb_ref[...] = wd_ref[...].astype(wdb_ref.dtype)

    h = h_ref[...]
    gate = jnp.dot(h, wgb_ref[...], preferred_element_type=F32)
    up = jnp.dot(h, wub_ref[...], preferred_element_type=F32)
    o_ref[...] = (gate * jax.nn.sigmoid(gate) * up).astype(o_ref.dtype)


def _ffn_up(h, wg, wu, wd, layer, *, tm, tn):
    m, d = h.shape
    n = wg.shape[2]
    n_i = m // tm
    slab = wd.shape[1] // ((n // tn) * n_i)
    assert slab * (n // tn) * n_i == wd.shape[1] and slab % BF16_ROWS == 0
    w_spec = pl.BlockSpec((None, d, tn), lambda j, i: (layer, 0, j))
    return pl.pallas_call(
        _ffn_up_kernel,
        grid=(n // tn, n_i),
        in_specs=[pl.BlockSpec((tm, d), lambda j, i: (i, 0)), w_spec, w_spec,
                  pl.BlockSpec((None, slab, wd.shape[2]), lambda j, i: (layer, j * n_i + i, 0))],
        out_specs=[pl.BlockSpec((tm, tn), lambda j, i: (i, j)),
                   pl.BlockSpec((slab, wd.shape[2]), lambda j, i: (j * n_i + i, 0))],
        out_shape=[jax.ShapeDtypeStruct((m, n), BF16),
                   jax.ShapeDtypeStruct(wd.shape[1:], BF16)],
        scratch_shapes=[pltpu.VMEM((d, tn), BF16), pltpu.VMEM((d, tn), BF16)],
        compiler_params=_cparams(("parallel", "arbitrary")),
        name="ffn_up",
    )(h, wg, wu, wd)


def _project_norm_residual(lhs_ref, w_ref, g_ref, x_ref, gn_ref, o_ref, hn_ref, acc_ref):
    n_half, half, _ = acc_ref.shape
    for b in range(n_half):
        acc_ref[b] = jnp.dot(lhs_ref[b * half:(b + 1) * half, :], w_ref[...],
                             preferred_element_type=F32)
    for b in range(n_half):
        for c in range(half // ROW_CHUNK):
            rows = slice(b * half + c * ROW_CHUNK, b * half + (c + 1) * ROW_CHUNK)
            o = x_ref[rows, :] + _rms(acc_ref[b, c * ROW_CHUNK:(c + 1) * ROW_CHUNK, :]) * g_ref[...]
            o_ref[rows, :] = o
            if hn_ref is not None:
                hn_ref[rows, :] = (_rms(o) * gn_ref[...]).astype(hn_ref.dtype)


def _ffn_down_kernel(*refs, emit_next):
    if emit_next:
        a_ref, w_ref, g_ref, x_ref, gn_ref, o_ref, hn_ref, acc_ref = refs
    else:
        a_ref, w_ref, g_ref, x_ref, o_ref, acc_ref = refs
        gn_ref = hn_ref = None
    _project_norm_residual(a_ref, w_ref, g_ref, x_ref, gn_ref, o_ref, hn_ref, acc_ref)


def _ffn_down(a, w, g, x, gn, *, tm):
    m, kdim = a.shape
    d = w.shape[1]
    emit_next = gn is not None
    vec = pl.BlockSpec((1, d), lambda i: (0, 0))
    row_tile = pl.BlockSpec((tm, d), lambda i: (i, 0))
    outs = pl.pallas_call(
        functools.partial(_ffn_down_kernel, emit_next=emit_next),
        grid=(m // tm,),
        in_specs=[pl.BlockSpec((tm, kdim), lambda i: (i, 0)),
                  pl.BlockSpec((kdim, d), lambda i: (0, 0), pipeline_mode=pl.Buffered(1)),
                  vec, row_tile] + [vec] * emit_next,
        out_specs=[row_tile] + [row_tile] * emit_next,
        out_shape=[jax.ShapeDtypeStruct((m, d), F32)] + [jax.ShapeDtypeStruct((m, d), BF16)] * emit_next,
        scratch_shapes=[pltpu.VMEM((2, tm // 2, d), F32)],
        compiler_params=_cparams(("parallel",)),
        name="ffn_down",
    )(a, w, g, x, *([gn] * emit_next))
    return (outs[0], outs[1]) if emit_next else (outs[0], None)


def _out_proj_kernel(a_ref, b_ref, c_ref, d_ref, w_ref, g_ref, x_ref, gn_ref, o_ref, hn_ref,
                     lhs_ref, acc_ref):
    for p, part in enumerate((a_ref, b_ref, c_ref, d_ref)):
        lhs_ref[:, p * GROUP_WIDTH:(p + 1) * GROUP_WIDTH] = part[...]
    _project_norm_residual(lhs_ref, w_ref, g_ref, x_ref, gn_ref, o_ref, hn_ref, acc_ref)


def _out_proj(parts, w, g, x, gn, *, tm):
    m, d = x.shape
    part_spec = pl.BlockSpec((tm, GROUP_WIDTH), lambda i: (i, 0))
    vec = pl.BlockSpec((1, d), lambda i: (0, 0))
    row_tile = pl.BlockSpec((tm, d), lambda i: (i, 0))
    return pl.pallas_call(
        _out_proj_kernel,
        grid=(m // tm,),
        in_specs=[part_spec, part_spec, part_spec, part_spec,
                  pl.BlockSpec((4 * GROUP_WIDTH, d), lambda i: (0, 0)),
                  vec, row_tile, vec],
        out_specs=[row_tile, row_tile],
        out_shape=[jax.ShapeDtypeStruct((m, d), F32), jax.ShapeDtypeStruct((m, d), BF16)],
        scratch_shapes=[pltpu.VMEM((tm, 4 * GROUP_WIDTH), BF16),
                        pltpu.VMEM((2, tm // 2, d), F32)],
        compiler_params=_cparams(("parallel",)),
        name="out_proj",
    )(*parts, w, g, x, gn)


def _rope_tables(pos_lo, pos_hi):
    half = DIFF_QK_DIM // 2
    inv = ROPE_THETA ** (-np.arange(half, dtype=np.float64) / half)
    lane = np.arange(LANES)
    pos = np.where(lane[None, :] < DIFF_QK_DIM, pos_lo[:, None], pos_hi[:, None]).astype(np.float64)
    ang = pos * inv[lane % half][None, :]
    sign = np.where((lane % DIFF_QK_DIM) < half, -1.0, 1.0)
    return (jnp.asarray(np.cos(ang), dtype=F32),
            jnp.asarray(np.sin(ang) * sign[None, :], dtype=F32))


def _rope(x, cos, sin_signed):
    lane = lax.broadcasted_iota(jnp.int32, x.shape, 1)
    first_half = (lane & (DIFF_QK_DIM // 2)) == 0
    partner = jnp.where(first_half, pltpu.roll(x, LANES - DIFF_QK_DIM // 2, 1),
                        pltpu.roll(x, DIFF_QK_DIM // 2, 1))
    return x * cos + partner * sin_signed


LOG2E = math.log2(math.e)


ATTN_OFFSET_SLACK = 64.0
PREP_UNROLL = 4
SUBLANES = 8
BF16_ROWS = 16


def _scores(krot_ref, qs_ref, j, tk):
    k = krot_ref[j * tk:(j + 1) * tk, :]
    return lax.dot_general(k, qs_ref[...], (((1,), (1,)), ((), ())), preferred_element_type=F32)


def _attend_online(qs_ref, krot_ref, vt_ref, acc_ref, *, tk):
    n_tiles = krot_ref.shape[0] // tk
    m = l = None
    s_next = _scores(krot_ref, qs_ref, 0, tk)
    for j in range(n_tiles):
        s = s_next
        if j + 1 < n_tiles:
            s_next = _scores(krot_ref, qs_ref, j + 1, tk)
        m_cur = jnp.max(s, axis=0, keepdims=True)
        m_next = m_cur if m is None else jnp.maximum(m, m_cur)
        p = jnp.exp2(s - m_next)
        pv = jnp.dot(vt_ref[:, j * tk:(j + 1) * tk], p.astype(BF16), preferred_element_type=F32)
        if m is None:
            l = jnp.sum(p, axis=0, keepdims=True)
            acc_ref[...] = pv
        else:
            alpha = jnp.exp2(m - m_next)
            l = alpha * l + jnp.sum(p, axis=0, keepdims=True)
            acc_ref[...] = alpha * acc_ref[...] + pv
        m = m_next
    return l


def _attend_fixed(qs_ref, krot_ref, vt_ref, acc_ref, offset, *, tk):
    n_tiles = krot_ref.shape[0] // tk
    l = None
    s_next = _scores(krot_ref, qs_ref, 0, tk)
    for j in range(n_tiles):
        s = s_next
        if j + 1 < n_tiles:
            s_next = _scores(krot_ref, qs_ref, j + 1, tk)
        p = jnp.exp2(s - offset)
        pv = jnp.dot(vt_ref[:, j * tk:(j + 1) * tk], p.astype(BF16), preferred_element_type=F32)
        if l is None:
            l = jnp.sum(p, axis=0, keepdims=True)
            acc_ref[...] = pv
        else:
            l = l + jnp.sum(p, axis=0, keepdims=True)
            acc_ref[...] += pv
    return l


def _fixed_offset_is_safe(q_norm2_max, kstat_ref):
    worst_exponent = jnp.sqrt(q_norm2_max) * kstat_ref[1:2, 0:1]
    return (jnp.max(worst_exponent) <= ATTN_OFFSET_SLACK).astype(jnp.int32)


def _attend(qs_ref, fixed_ok_ref, krot_ref, vt_ref, kstat_ref, acc_ref, finish, *, tk):
    use_fixed = fixed_ok_ref[0] == 1

    @pl.when(use_fixed)
    def _():
        qf = qs_ref[...].astype(F32)
        ones = jnp.ones((BF16_ROWS, qf.shape[1]), BF16)
        q_norm2 = lax.dot_general(ones, (qf * qf).astype(BF16), (((1,), (1,)), ((), ())),
                                  preferred_element_type=F32)
        offset = jnp.sqrt(q_norm2[0:1]) * kstat_ref[0:1, 0:1]
        finish(_attend_fixed(qs_ref, krot_ref, vt_ref, acc_ref, offset, tk=tk))

    @pl.when(jnp.logical_not(use_fixed))
    def _():
        finish(_attend_online(qs_ref, krot_ref, vt_ref, acc_ref, tk=tk))


def _rms_gain_norm2_bound(gain):
    return gain.shape[1] * jnp.max(gain * gain, axis=1, keepdims=True)


def _prepare_keys_values(k_ref, v_ref, cos_ref, sin_ref, krot_ref, vt_ref, kstat_ref, gain):
    seq = k_ref.shape[0]

    def body(c, carry):
        k_sum, k_max2 = carry
        r0 = pl.multiple_of(c * ROW_CHUNK, ROW_CHUNK)
        k = k_ref[pl.ds(r0, ROW_CHUNK), :].astype(F32)
        if gain is not None:
            k = _rms(k) * gain
        kb = _rope(k, cos_ref[pl.ds(r0, ROW_CHUNK), :], sin_ref[pl.ds(r0, ROW_CHUNK), :]).astype(BF16)
        krot_ref[pl.ds(r0, ROW_CHUNK), :] = kb
        vt_ref[:, pl.ds(r0, ROW_CHUNK)] = v_ref[pl.ds(r0, ROW_CHUNK), :].astype(F32).T.astype(BF16)
        kf = kb.astype(F32)
        if gain is None:
            norm2 = jnp.sum(kf * kf, axis=1, keepdims=True)
            k_max2 = jnp.maximum(k_max2, jnp.max(norm2, axis=0, keepdims=True))
        return k_sum + jnp.sum(kf, axis=0, keepdims=True), k_max2

    k_max2_init = (jnp.zeros((1, 1), F32) if gain is None
                   else _rms_gain_norm2_bound(gain))
    k_sum, k_max2 = lax.fori_loop(0, seq // ROW_CHUNK, body,
                                  (jnp.zeros((1, k_ref.shape[1]), F32), k_max2_init),
                                  unroll=PREP_UNROLL)
    k_bar = k_sum * (1.0 / seq)
    k_bar_norm = jnp.sqrt(jnp.sum(k_bar * k_bar, axis=1, keepdims=True))
    k_max = jnp.sqrt(k_max2) * 1.01
    kstat_ref[0:1, :] = jnp.broadcast_to(k_max, (1, kstat_ref.shape[1]))
    kstat_ref[1:2, :] = jnp.broadcast_to(k_max + k_bar_norm, (1, kstat_ref.shape[1]))


def _stacked_row(c, tq):
    per_block = tq // ROW_CHUNK
    return pl.multiple_of((c // per_block) * (2 * tq) + (c % per_block) * ROW_CHUNK, ROW_CHUNK)


def _diff_attn_kernel(q_ref, k_ref, v_ref, cos_ref, sin_ref, lam_ref, g_ref, o_ref,
                      krot_ref, vt_ref, kstat_ref, qs_ref, fixed_ok_ref, acc_ref, *, tq, tk, lam_init):
    @pl.when(pl.program_id(1) == 0)
    def _():
        _prepare_keys_values(k_ref, v_ref, cos_ref, sin_ref, krot_ref, vt_ref, kstat_ref, None)

        def body(c, q_norm2_max):
            rows = pl.ds(pl.multiple_of(c * ROW_CHUNK, ROW_CHUNK), ROW_CHUNK)
            q = _rope(q_ref[rows, :].astype(F32), cos_ref[rows, :], sin_ref[rows, :]) * (
                DIFF_QK_DIM ** -0.5 * LOG2E)
            lane = lax.broadcasted_iota(jnp.int32, q.shape, 1)
            dst = _stacked_row(c, tq)
            qs_ref[pl.ds(dst, ROW_CHUNK), :] = jnp.where(lane < DIFF_QK_DIM, q, 0.0).astype(BF16)
            qs_ref[pl.ds(dst + tq, ROW_CHUNK), :] = jnp.where(lane >= DIFF_QK_DIM, q, 0.0).astype(BF16)
            return jnp.maximum(q_norm2_max,
                               jnp.max(jnp.sum(q * q, axis=1, keepdims=True), axis=0, keepdims=True))

        q_norm2_max = lax.fori_loop(0, q_ref.shape[0] // ROW_CHUNK, body, jnp.zeros((1, 1), F32),
                                    unroll=PREP_UNROLL)
        fixed_ok_ref[0] = _fixed_offset_is_safe(q_norm2_max, kstat_ref)

    qs_blk = qs_ref.at[pl.ds(pl.multiple_of(pl.program_id(1) * (2 * tq), 2 * tq), 2 * tq), :]

    def finish(l):
        lp = lam_ref[...]
        lam = (jnp.exp(jnp.sum(lp[0:1] * lp[1:2], keepdims=True))
               - jnp.exp(jnp.sum(lp[2:3] * lp[3:4], keepdims=True)) + lam_init)
        ot = (acc_ref[:, 0:tq] / l[:, 0:tq] - lam * (acc_ref[:, tq:2 * tq] / l[:, tq:2 * tq]))
        ms = jnp.mean(ot * ot, axis=0, keepdims=True)
        ot = ot * lax.rsqrt(ms + EPS) * (g_ref[...] * (1.0 - lam_init))
        o_ref[...] = ot.T.astype(o_ref.dtype)

    _attend(qs_blk, fixed_ok_ref, krot_ref, vt_ref, kstat_ref, acc_ref, finish, tk=tk)


def _gqa_attn_kernel(q_ref, k_ref, v_ref, cos_ref, sin_ref, qn_ref, kn_ref, o_ref,
                     krot_ref, vt_ref, kstat_ref, qs_ref, fixed_ok_ref, acc_ref, *, tq, tk):
    @pl.when(pl.program_id(1) == 0)
    def _():
        _prepare_keys_values(k_ref, v_ref, cos_ref, sin_ref, krot_ref, vt_ref, kstat_ref, kn_ref[...])

        q_scale = HEAD_DIM ** -0.5 * LOG2E

        def body(c, _):
            rows = pl.ds(pl.multiple_of(c * ROW_CHUNK, ROW_CHUNK), ROW_CHUNK)
            dst = _stacked_row(c, tq)
            for r in range(2):
                q = q_ref[rows, r * HEAD_DIM:(r + 1) * HEAD_DIM].astype(F32)
                q = _rope(_rms(q) * qn_ref[...], cos_ref[rows, :], sin_ref[rows, :]) * q_scale
                qs_ref[pl.ds(dst + r * tq, ROW_CHUNK), :] = q.astype(BF16)
            return 0

        lax.fori_loop(0, q_ref.shape[0] // ROW_CHUNK, body, 0, unroll=PREP_UNROLL)
        q_norm2_max = _rms_gain_norm2_bound(qn_ref[...]) * (q_scale * q_scale)
        fixed_ok_ref[0] = _fixed_offset_is_safe(q_norm2_max, kstat_ref)

    qs_blk = qs_ref.at[pl.ds(pl.multiple_of(pl.program_id(1) * (2 * tq), 2 * tq), 2 * tq), :]

    def finish(l):
        for r in range(2):
            ot = acc_ref[:, r * tq:(r + 1) * tq] / l[:, r * tq:(r + 1) * tq]
            o_ref[:, r * HEAD_DIM:(r + 1) * HEAD_DIM] = ot.T.astype(o_ref.dtype)

    _attend(qs_blk, fixed_ok_ref, krot_ref, vt_ref, kstat_ref, acc_ref, finish, tk=tk)


def _flash_call(kernel, proj3, tables, params, *, n_groups, q_width, q_col, k_col, v_col,
                out_width, tq, name):
    batch, seq, _ = proj3.shape
    cos_t, sin_t = tables
    qb, kb, vb = q_col // q_width, k_col // HEAD_DIM, v_col // HEAD_DIM

    def grp(i):
        return i // n_groups, i % n_groups

    in_specs = [
        pl.BlockSpec((None, seq, q_width), lambda i, j: (grp(i)[0], 0, qb + grp(i)[1])),
        pl.BlockSpec((None, seq, HEAD_DIM), lambda i, j: (grp(i)[0], 0, kb + grp(i)[1])),
        pl.BlockSpec((None, seq, HEAD_DIM), lambda i, j: (grp(i)[0], 0, vb + grp(i)[1])),
        pl.BlockSpec((seq, LANES), lambda i, j: (0, 0)),
        pl.BlockSpec((seq, LANES), lambda i, j: (0, 0)),
    ] + [pl.BlockSpec(p.shape, lambda i, j: (0, 0)) for p in params]
    return pl.pallas_call(
        kernel,
        grid=(batch * n_groups, seq // tq),
        in_specs=in_specs,
        out_specs=pl.BlockSpec((None, tq, out_width), lambda i, j: (grp(i)[0], j, grp(i)[1])),
        out_shape=jax.ShapeDtypeStruct((batch, seq, GROUP_WIDTH), BF16),
        scratch_shapes=[pltpu.VMEM((seq, HEAD_DIM), BF16),
                        pltpu.VMEM((HEAD_DIM, seq), BF16),
                        pltpu.VMEM((SUBLANES, LANES), F32),
                        pltpu.VMEM((2 * seq, HEAD_DIM), BF16),
                        pltpu.SMEM((1,), jnp.int32),
                        pltpu.VMEM((HEAD_DIM, 2 * tq), F32)],
        compiler_params=_cparams(("parallel", "arbitrary")),
        name=name,
    )(proj3, proj3, proj3, cos_t, sin_t, *params)


def _conv_kernel(a_ref, g_ref, dw_ref, dwb_ref, lng_ref, lnb_ref, pw_ref, pwb_ref, o_ref,
                 u_ref, acc_ref, y_ref, *, tm):
    seq = a_ref.shape[0]
    t = pl.program_id(1)

    @pl.when(t == 0)
    def _():
        zeros = jnp.zeros((CONV_HALO, u_ref.shape[1]), F32)
        u_ref[0:CONV_HALO, :] = zeros
        u_ref[CONV_HALO + seq:2 * CONV_HALO + seq, :] = zeros

        def body(c, _):
            r0 = pl.multiple_of(c * ROW_CHUNK, ROW_CHUNK)
            a = a_ref[pl.ds(r0, ROW_CHUNK), :].astype(F32)
            g = g_ref[pl.ds(r0, ROW_CHUNK), :].astype(F32)
            u_ref[pl.ds(CONV_HALO + r0, ROW_CHUNK), :] = a * jax.nn.sigmoid(g)
            return 0

        lax.fori_loop(0, seq // ROW_CHUNK, body, 0)

    pad = CONV_WIDTH // 2
    sub = 8
    n_ch = u_ref.shape[1]

    def row_block(rb, _):
        base = pl.multiple_of(t * tm + rb * ROW_CHUNK, ROW_CHUNK)
        for cg in range(n_ch // LANES):
            cols = slice(cg * LANES, (cg + 1) * LANES)
            win = u_ref[pl.ds(base, ROW_CHUNK + 2 * CONV_HALO), cols]
            acc = jnp.zeros((ROW_CHUNK, LANES), F32) + dwb_ref[:, cols]
            for r in range(sub):
                shifted = win if r == 0 else pltpu.roll(win, win.shape[0] - r, 0)
                for a in range(2 * CONV_HALO // sub):
                    k = sub * a + r - (CONV_HALO - pad)
                    if 0 <= k < CONV_WIDTH:
                        acc = acc + shifted[sub * a:sub * a + ROW_CHUNK, :] * dw_ref[k:k + 1, cols]
            acc_ref[:, cols] = acc
        acc = acc_ref[...]
        mu = jnp.mean(acc, axis=-1, keepdims=True)
        cen = acc - mu
        var = jnp.mean(cen * cen, axis=-1, keepdims=True)
        y = cen * lax.rsqrt(var + EPS) * lng_ref[...] + lnb_ref[...]
        y_ref[pl.ds(pl.multiple_of(rb * ROW_CHUNK, ROW_CHUNK), ROW_CHUNK), :] = (
            y * jax.nn.sigmoid(y)).astype(BF16)
        return 0

    lax.fori_loop(0, tm // ROW_CHUNK, row_block, 0)
    o_ref[...] = (jnp.dot(y_ref[...], pw_ref[...], preferred_element_type=F32)
                  + pwb_ref[...]).astype(o_ref.dtype)


def _conv_mixer(proj3, dw, dwb, lng, lnb, pw, pwb, *, tm):
    batch, seq, _ = proj3.shape
    ch = GROUP_WIDTH
    vec = pl.BlockSpec((1, ch), lambda b, t: (0, 0))
    return pl.pallas_call(
        functools.partial(_conv_kernel, tm=tm),
        grid=(batch, seq // tm),
        in_specs=[pl.BlockSpec((None, seq, ch), lambda b, t: (b, 0, COL_B_A // ch)),
                  pl.BlockSpec((None, seq, ch), lambda b, t: (b, 0, COL_B_G // ch)),
                  pl.BlockSpec((CONV_WIDTH, ch), lambda b, t: (0, 0)),
                  vec, vec, vec,
                  pl.BlockSpec((ch, ch), lambda b, t: (0, 0)),
                  vec],
        out_specs=pl.BlockSpec((None, tm, ch), lambda b, t: (b, t, 0)),
        out_shape=jax.ShapeDtypeStruct((batch, seq, ch), BF16),
        scratch_shapes=[pltpu.VMEM((seq + 2 * CONV_HALO, ch), F32),
                        pltpu.VMEM((ROW_CHUNK, ch), F32),
                        pltpu.VMEM((tm, ch), BF16)],
        compiler_params=_cparams(("parallel", "arbitrary")),
        name="conv_mixer",
    )(proj3, proj3, dw, dwb, lng, lnb, pw, pwb)


def _na_geometry(seq):
    rows = seq // GRID_W
    qr_blk = Q_BLOCK // GRID_W
    mask_add = np.zeros((NA_CLASSES, Q_BLOCK, NA_BAND), np.float32)
    q_local = np.arange(Q_BLOCK)
    p = np.arange(NA_BAND)
    for c, blk in enumerate(_na_class_blocks(seq)):
        bs = int(np.clip(blk * qr_blk - NA_KH // 2, 0, rows - NA_BAND_ROWS))
        q_row = (blk * qr_blk + q_local // GRID_W)[:, None]
        q_col = (q_local % GRID_W)[:, None]
        k_row = (bs + p // GRID_W)[None, :]
        k_col = (p % GRID_W)[None, :]
        win_r = np.clip(q_row - NA_KH // 2, 0, rows - NA_KH)
        win_c = np.clip(q_col - NA_KW // 2, 0, GRID_W - NA_KW)
        ok = (k_row >= win_r) & (k_row < win_r + NA_KH) & (k_col >= win_c) & (k_col < win_c + NA_KW)
        mask_add[c] = np.where(ok, 0.0, -1e30)
    return jnp.asarray(mask_add)


def _na_class_blocks(seq):
    n_blk = seq // Q_BLOCK
    return (0, 1, 2, n_blk - 2, n_blk - 1)


def _rpb_tiles_kernel(rpb_ref, mask_ref, o_ref, t2_ref, *, seq):
    h = pl.program_id(0)
    n_c = 2 * NA_KW - 1
    n_r = 2 * NA_KH - 1
    shape = (GRID_W, LANES)
    lane = lax.broadcasted_iota(jnp.int32, shape, 1)
    q_col = lax.broadcasted_iota(jnp.int32, shape, 0)
    ic = jnp.clip((lane & (GRID_W - 1)) - q_col + (NA_KW - 1), 0, n_c - 1)
    left = lane < GRID_W
    base = h * (n_r * n_c)
    for i in range(NA_T2):
        i_l = min(max(i - 1, 0), n_r - 1)
        i_r = min(max(i, 0), n_r - 1)

        acc = jnp.zeros(shape, F32)
        for j in range(n_c):
            coef = jnp.where(left, rpb_ref[base + i_l * n_c + j], rpb_ref[base + i_r * n_c + j])
            acc = acc + jnp.where(ic == j, coef, 0.0)
        t2_ref[i] = acc

    rows = seq // GRID_W
    qr_blk = Q_BLOCK // GRID_W
    for c, blk in enumerate(_na_class_blocks(seq)):
        bs = min(max(blk * qr_blk - NA_KH // 2, 0), rows - NA_BAND_ROWS)
        for a in range(qr_blk):
            for r2 in range(NA_BAND_ROWS // 2):
                i1 = bs + 2 * r2 - (blk * qr_blk + a) + (NA_KH - 1)
                idx = min(max(i1, -1), NA_T2 - 2) + 1
                rs = slice(a * GRID_W, (a + 1) * GRID_W)
                cs = slice(r2 * LANES, (r2 + 1) * LANES)
                o_ref[c, rs, cs] = t2_ref[idx] + mask_ref[c, rs, cs]


def _rpb_tiles(rpb, mask_add, seq):
    heads = rpb.shape[0]
    return pl.pallas_call(
        functools.partial(_rpb_tiles_kernel, seq=seq),
        grid=(heads,),
        in_specs=[pl.BlockSpec(memory_space=pltpu.SMEM),
                  pl.BlockSpec(mask_add.shape, lambda h: (0, 0, 0))],
        out_specs=pl.BlockSpec((None,) + mask_add.shape, lambda h: (h, 0, 0, 0)),
        out_shape=jax.ShapeDtypeStruct((heads,) + mask_add.shape, F32),
        scratch_shapes=[pltpu.VMEM((NA_T2, GRID_W, LANES), F32)],
        compiler_params=_cparams(("parallel",)),
        name="na_rpb_tiles",
    )(rpb.reshape(-1), mask_add)


def _na_kernel(q_ref, k_ref, v_ref, bias_ref, o_ref):
    seq = q_ref.shape[0]
    rows = seq // GRID_W
    qr_blk = Q_BLOCK // GRID_W
    n_blk = seq // Q_BLOCK

    classes = _na_class_blocks(seq)

    def band(blk):
        k0 = min(max(blk * qr_blk - NA_KH // 2, 0), rows - NA_BAND_ROWS) * GRID_W
        return slice(k0, k0 + NA_BAND)

    def scores(blk):
        q = q_ref[blk * Q_BLOCK:(blk + 1) * Q_BLOCK, :]
        return lax.dot_general(q, k_ref[band(blk), :], (((1,), (1,)), ((), ())),
                               preferred_element_type=F32)

    pending = [scores(b) for b in range(NA_LOOKAHEAD)]
    for blk in range(n_blk):
        s = pending.pop(0)
        if blk + NA_LOOKAHEAD < n_blk:
            pending.append(scores(blk + NA_LOOKAHEAD))
        cls = classes.index(blk) if blk in classes else 2
        s = s * (HEAD_DIM ** -0.5) + bias_ref[cls]
        m = jnp.max(s, axis=1, keepdims=True)
        p = jnp.exp(s - m)
        l = jnp.sum(p, axis=1, keepdims=True)
        o = jnp.dot(p.astype(BF16), v_ref[band(blk), :], preferred_element_type=F32) / l
        o_ref[blk * Q_BLOCK:(blk + 1) * Q_BLOCK, :] = o.astype(o_ref.dtype)


def _na_mixer(proj3, bias):
    batch, seq, _ = proj3.shape
    heads = GROUP_HEADS

    def head_spec(col):
        return pl.BlockSpec((None, seq, HEAD_DIM), lambda b, h: (b, 0, col // HEAD_DIM + h))

    return pl.pallas_call(
        _na_kernel,
        grid=(batch, heads),
        in_specs=[head_spec(COL_D_Q), head_spec(COL_D_K), head_spec(COL_D_V),
                  pl.BlockSpec((None,) + bias.shape[1:], lambda b, h: (h, 0, 0, 0))],
        out_specs=pl.BlockSpec((None, seq, HEAD_DIM), lambda b, h: (b, 0, h)),
        out_shape=jax.ShapeDtypeStruct((batch, seq, GROUP_WIDTH), BF16),
        compiler_params=_cparams(("parallel", "parallel")),
        name="na_mixer",
    )(proj3, proj3, proj3, bias)


def kernel(x, norm_mix_pre, norm_mix_post, norm_ffn_pre, norm_ffn_post, w_in, w_out, diff_lambda, diff_subln, conv_dw, conv_dw_b, conv_ln_g, conv_ln_b, conv_pw, conv_pw_b, gqa_q_norm, gqa_k_norm, na_rpb, ffn_gate, ffn_up, ffn_down):
    batch, seq, d = x.shape
    depth = w_in.shape[0]
    tokens = batch * seq

    t = np.arange(seq)
    diff_tables = _rope_tables(t, t)
    axial_tables = _rope_tables(t // GRID_W, t % GRID_W)
    mask_add = _na_geometry(seq)

    def row(v):
        return v.reshape(1, -1)

    xf = x.reshape(tokens, d)
    h = _rmsnorm(xf, row(norm_mix_pre[0]), tm=1024)
    for l in range(depth):
        lam_init = 0.8 - 0.6 * math.exp(-0.3 * l)
        proj = _in_proj(h, w_in, l, tm=2048, tn=1024)
        proj3 = proj.reshape(batch, seq, IN_COLS)

        out_a = _flash_call(
            functools.partial(_diff_attn_kernel, tq=ATTN_TQ, tk=ATTN_TK, lam_init=lam_init),
            proj3, diff_tables, (diff_lambda[l], diff_subln[l].reshape(-1, 1)),
            n_groups=GROUP_HEADS, q_width=HEAD_DIM, q_col=COL_A_Q, k_col=COL_A_K, v_col=COL_A_V,
            out_width=HEAD_DIM, tq=ATTN_TQ, name="diff_attn")
        out_b = _conv_mixer(proj3, conv_dw[l], row(conv_dw_b[l]), row(conv_ln_g[l]), row(conv_ln_b[l]),
                            conv_pw[l].astype(BF16), row(conv_pw_b[l]), tm=512)
        out_c = _flash_call(
            functools.partial(_gqa_attn_kernel, tq=ATTN_TQ, tk=ATTN_TK),
            proj3, axial_tables, (row(gqa_q_norm[l]), row(gqa_k_norm[l])),
            n_groups=GQA_KV_HEADS, q_width=2 * HEAD_DIM, q_col=COL_C_Q, k_col=COL_C_K, v_col=COL_C_V,
            out_width=2 * HEAD_DIM, tq=ATTN_TQ, name="gqa_attn")
        out_d = _na_mixer(proj3, _rpb_tiles(na_rpb[l], mask_add, seq))

        parts = [o.reshape(tokens, GROUP_WIDTH) for o in (out_a, out_b, out_c, out_d)]
        xf, h = _out_proj(parts, w_out[l].astype(BF16), row(norm_mix_post[l]), xf,
                          row(norm_ffn_pre[l]), tm=512)

        act, w_down = _ffn_up(h, ffn_gate, ffn_up, ffn_down, l, tm=1024, tn=512)
        next_gain = row(norm_mix_pre[l + 1]) if l + 1 < depth else None
        xf, h = _ffn_down(act, w_down, row(norm_ffn_post[l]), xf, next_gain,
                          tm=256)
    return xf.reshape(batch, seq, d)
```

```python
import functools
import math

import numpy as np
import jax
import jax.numpy as jnp
from jax import lax
from jax.experimental import pallas as pl
from jax.experimental.pallas import tpu as pltpu

D_MODEL = 2048
HEAD_DIM = 128
GROUP_HEADS = 4
GROUP_WIDTH = GROUP_HEADS * HEAD_DIM
DIFF_QK_DIM = HEAD_DIM // 2
CONV_WIDTH = 31
GQA_KV_HEADS = 2
NA_KH = 8
NA_KW = 16
GRID_W = 64
Q_BLOCK = 128
ROPE_THETA = 10000.0
EPS = 1e-6

COL_A_Q, COL_A_K, COL_A_V = 0, 512, 1024
COL_B_A, COL_B_G = 1536, 2048
COL_C_Q, COL_C_K, COL_C_V = 2560, 3072, 3328
COL_D_Q, COL_D_K, COL_D_V = 3584, 4096, 4608
IN_COLS = 5120

LANES = 128
VMEM_LIMIT = 56 * 1024 * 1024
ROW_CHUNK = 128
ATTN_TQ = 512
ATTN_TK = 512

NA_BAND_ROWS = 10
NA_BAND = NA_BAND_ROWS * GRID_W
NA_CLASSES = 5
NA_T2 = 16
NA_LOOKAHEAD = 2
CONV_HALO = 16

F32 = jnp.float32
BF16 = jnp.bfloat16


def _cparams(semantics):
    return pltpu.CompilerParams(dimension_semantics=semantics,
                                vmem_limit_bytes=VMEM_LIMIT)


def _rms(x):
    return x * lax.rsqrt(jnp.mean(x * x, axis=-1, keepdims=True) + EPS)


def _rmsnorm_kernel(x_ref, g_ref, h_ref):
    def body(c, _):
        r0 = pl.multiple_of(c * ROW_CHUNK, ROW_CHUNK)
        h_ref[pl.ds(r0, ROW_CHUNK), :] = (_rms(x_ref[pl.ds(r0, ROW_CHUNK), :]) * g_ref[...]).astype(h_ref.dtype)
        return 0

    lax.fori_loop(0, x_ref.shape[0] // ROW_CHUNK, body, 0)


def _rmsnorm(x, g, *, tm):
    m, d = x.shape
    return pl.pallas_call(
        _rmsnorm_kernel,
        grid=(m // tm,),
        in_specs=[pl.BlockSpec((tm, d), lambda i: (i, 0)),
                  pl.BlockSpec((1, d), lambda i: (0, 0))],
        out_specs=pl.BlockSpec((tm, d), lambda i: (i, 0)),
        out_shape=jax.ShapeDtypeStruct((m, d), BF16),
        compiler_params=_cparams(("parallel",)),
        name="rmsnorm",
    )(x, g)


def _cast_weight_tile(w_ref, wb_ref):
    def body(c, _):
        r0 = pl.multiple_of(c * ROW_CHUNK, ROW_CHUNK)
        wb_ref[pl.ds(r0, ROW_CHUNK), :] = w_ref[pl.ds(r0, ROW_CHUNK), :].astype(wb_ref.dtype)
        return 0

    lax.fori_loop(0, w_ref.shape[0] // ROW_CHUNK, body, 0)


def _in_proj_kernel(h_ref, w_ref, o_ref, wb_ref):
    @pl.when(pl.program_id(1) == 0)
    def _():
        _cast_weight_tile(w_ref, wb_ref)

    o_ref[...] = jnp.dot(h_ref[...], wb_ref[...], preferred_element_type=F32).astype(o_ref.dtype)


def _in_proj(h, w, layer, *, tm, tn):
    m, d = h.shape
    n = w.shape[2]
    return pl.pallas_call(
        _in_proj_kernel,
        grid=(n // tn, m // tm),
        in_specs=[pl.BlockSpec((tm, d), lambda j, i: (i, 0)),
                  pl.BlockSpec((None, d, tn), lambda j, i: (layer, 0, j))],
        out_specs=pl.BlockSpec((tm, tn), lambda j, i: (i, j)),
        out_shape=jax.ShapeDtypeStruct((m, n), BF16),
        scratch_shapes=[pltpu.VMEM((d, tn), BF16)],
        compiler_params=_cparams(("parallel", "arbitrary")),
        name="in_proj",
    )(h, w)


def _ffn_up_kernel(h_ref, wg_ref, wu_ref, wd_ref, o_ref, wdb_ref, wgb_ref, wub_ref):
    @pl.when(pl.program_id(1) == 0)
    def _():
        _cast_weight_tile(wg_ref, wgb_ref)
        _cast_weight_tile(wu_ref, wub_ref)

    wdb_ref[...] = wd_ref[...].astype(wdb_ref.dtype)

    h = h_ref[...]
    gate = jnp.dot(h, wgb_ref[...], preferred_element_type=F32)
    up = jnp.dot(h, wub_ref[...], preferred_element_type=F32)
    o_ref[...] = (gate * jax.nn.sigmoid(gate) * up).astype(o_ref.dtype)


def _ffn_up(h, wg, wu, wd, layer, *, tm, tn):
    m, d = h.shape
    n = wg.shape[2]
    n_i = m // tm
    slab = wd.shape[1] // ((n // tn) * n_i)
    assert slab * (n // tn) * n_i == wd.shape[1] and slab % BF16_ROWS == 0
    w_spec = pl.BlockSpec((None, d, tn), lambda j, i: (layer, 0, j))
    return pl.pallas_call(
        _ffn_up_kernel,
        grid=(n // tn, n_i),
        in_specs=[pl.BlockSpec((tm, d), lambda j, i: (i, 0)), w_spec, w_spec,
                  pl.BlockSpec((None, slab, wd.shape[2]), lambda j, i: (layer, j * n_i + i, 0))],
        out_specs=[pl.BlockSpec((tm, tn), lambda j, i: (i, j)),
                   pl.BlockSpec((slab, wd.shape[2]), lambda j, i: (j * n_i + i, 0))],
        out_shape=[jax.ShapeDtypeStruct((m, n), BF16),
                   jax.ShapeDtypeStruct(wd.shape[1:], BF16)],
        scratch_shapes=[pltpu.VMEM((d, tn), BF16), pltpu.VMEM((d, tn), BF16)],
        compiler_params=_cparams(("parallel", "arbitrary")),
        name="ffn_up",
    )(h, wg, wu, wd)


def _project_norm_residual(lhs_ref, w_ref, g_ref, x_ref, gn_ref, o_ref, hn_ref, acc_ref):
    n_half, half, _ = acc_ref.shape
    for b in range(n_half):
        acc_ref[b] = jnp.dot(lhs_ref[b * half:(b + 1) * half, :], w_ref[...],
                             preferred_element_type=F32)
    for b in range(n_half):
        for c in range(half // ROW_CHUNK):
            rows = slice(b * half + c * ROW_CHUNK, b * half + (c + 1) * ROW_CHUNK)
            o = x_ref[rows, :] + _rms(acc_ref[b, c * ROW_CHUNK:(c + 1) * ROW_CHUNK, :]) * g_ref[...]
            o_ref[rows, :] = o
            if hn_ref is not None:
                hn_ref[rows, :] = (_rms(o) * gn_ref[...]).astype(hn_ref.dtype)


def _ffn_down_kernel(*refs, emit_next):
    if emit_next:
        a_ref, w_ref, g_ref, x_ref, gn_ref, o_ref, hn_ref, acc_ref = refs
    else:
        a_ref, w_ref, g_ref, x_ref, o_ref, acc_ref = refs
        gn_ref = hn_ref = None
    _project_norm_residual(a_ref, w_ref, g_ref, x_ref, gn_ref, o_ref, hn_ref, acc_ref)


def _ffn_down(a, w, g, x, gn, *, tm):
    m, kdim = a.shape
    d = w.shape[1]
    emit_next = gn is not None
    vec = pl.BlockSpec((1, d), lambda i: (0, 0))
    row_tile = pl.BlockSpec((tm, d), lambda i: (i, 0))
    outs = pl.pallas_call(
        functools.partial(_ffn_down_kernel, emit_next=emit_next),
        grid=(m // tm,),
        in_specs=[pl.BlockSpec((tm, kdim), lambda i: (i, 0)),
                  pl.BlockSpec((kdim, d), lambda i: (0, 0), pipeline_mode=pl.Buffered(1)),
                  vec, row_tile] + [vec] * emit_next,
        out_specs=[row_tile] + [row_tile] * emit_next,
        out_shape=[jax.ShapeDtypeStruct((m, d), F32)] + [jax.ShapeDtypeStruct((m, d), BF16)] * emit_next,
        scratch_shapes=[pltpu.VMEM((2, tm // 2, d), F32)],
        compiler_params=_cparams(("parallel",)),
        name="ffn_down",
    )(a, w, g, x, *([gn] * emit_next))
    return (outs[0], outs[1]) if emit_next else (outs[0], None)


def _out_proj_kernel(a_ref, b_ref, c_ref, d_ref, w_ref, g_ref, x_ref, gn_ref, o_ref, hn_ref,
                     lhs_ref, acc_ref):
    for p, part in enumerate((a_ref, b_ref, c_ref, d_ref)):
        lhs_ref[:, p * GROUP_WIDTH:(p + 1) * GROUP_WIDTH] = part[...]
    _project_norm_residual(lhs_ref, w_ref, g_ref, x_ref, gn_ref, o_ref, hn_ref, acc_ref)


def _out_proj(parts, w, g, x, gn, *, tm):
    m, d = x.shape
    part_spec = pl.BlockSpec((tm, GROUP_WIDTH), lambda i: (i, 0))
    vec = pl.BlockSpec((1, d), lambda i: (0, 0))
    row_tile = pl.BlockSpec((tm, d), lambda i: (i, 0))
    return pl.pallas_call(
        _out_proj_kernel,
        grid=(m // tm,),
        in_specs=[part_spec, part_spec, part_spec, part_spec,
                  pl.BlockSpec((4 * GROUP_WIDTH, d), lambda i: (0, 0)),
                  vec, row_tile, vec],
        out_specs=[row_tile, row_tile],
        out_shape=[jax.ShapeDtypeStruct((m, d), F32), jax.ShapeDtypeStruct((m, d), BF16)],
        scratch_shapes=[pltpu.VMEM((tm, 4 * GROUP_WIDTH), BF16),
                        pltpu.VMEM((2, tm // 2, d), F32)],
        compiler_params=_cparams(("parallel",)),
        name="out_proj",
    )(*parts, w, g, x, gn)


def _rope_tables(pos_lo, pos_hi):
    half = DIFF_QK_DIM // 2
    inv = ROPE_THETA ** (-np.arange(half, dtype=np.float64) / half)
    lane = np.arange(LANES)
    pos = np.where(lane[None, :] < DIFF_QK_DIM, pos_lo[:, None], pos_hi[:, None]).astype(np.float64)
    ang = pos * inv[lane % half][None, :]
    sign = np.where((lane % DIFF_QK_DIM) < half, -1.0, 1.0)
    return (jnp.asarray(np.cos(ang), dtype=F32),
            jnp.asarray(np.sin(ang) * sign[None, :], dtype=F32))


def _rope_partner(x):
    lane = lax.broadcasted_iota(jnp.int32, x.shape, 1)
    first_half = (lane & (DIFF_QK_DIM // 2)) == 0
    return jnp.where(first_half, pltpu.roll(x, LANES - DIFF_QK_DIM // 2, 1),
                     pltpu.roll(x, DIFF_QK_DIM // 2, 1))


def _rope_partner_matrix():
    src = lax.broadcasted_iota(jnp.int32, (LANES, LANES), 0)
    dst = lax.broadcasted_iota(jnp.int32, (LANES, LANES), 1)
    half = DIFF_QK_DIM // 2
    partner = jnp.where((dst & half) == 0, dst + half, dst - half)
    return jnp.where(src == partner, 1.0, 0.0).astype(BF16)


def _rope_bf16(x, pmat, cos, sin_signed):
    partner = jnp.dot(x, pmat, preferred_element_type=F32)
    return x.astype(F32) * cos + partner * sin_signed


LOG2E = math.log2(math.e)


ATTN_OFFSET_SLACK = 64.0
PREP_UNROLL = 4
SUBLANES = 8
BF16_ROWS = 16


def _scores(krot_ref, qs_ref, j, tk):
    k = krot_ref[j * tk:(j + 1) * tk, :]
    return lax.dot_general(k, qs_ref[...], (((1,), (1,)), ((), ())), preferred_element_type=F32)


def _attend_online(qs_ref, krot_ref, vt_ref, acc_ref, *, tk):
    n_tiles = krot_ref.shape[0] // tk
    m = l = None
    s_next = _scores(krot_ref, qs_ref, 0, tk)
    for j in range(n_tiles):
        s = s_next
        if j + 1 < n_tiles:
            s_next = _scores(krot_ref, qs_ref, j + 1, tk)
        m_cur = jnp.max(s, axis=0, keepdims=True)
        m_next = m_cur if m is None else jnp.maximum(m, m_cur)
        p = jnp.exp2(s - m_next)
        pv = jnp.dot(vt_ref[:, j * tk:(j + 1) * tk], p.astype(BF16), preferred_element_type=F32)
        if m is None:
            l = jnp.sum(p, axis=0, keepdims=True)
            acc_ref[...] = pv
        else:
            alpha = jnp.exp2(m - m_next)
            l = alpha * l + jnp.sum(p, axis=0, keepdims=True)
            acc_ref[...] = alpha * acc_ref[...] + pv
        m = m_next
    return l


def _attend_fixed(qs_ref, krot_ref, vt_ref, acc_ref, offset, *, tk):
    n_tiles = krot_ref.shape[0] // tk
    l = None
    s_next = _scores(krot_ref, qs_ref, 0, tk)
    for j in range(n_tiles):
        s = s_next
        if j + 1 < n_tiles:
            s_next = _scores(krot_ref, qs_ref, j + 1, tk)
        p = jnp.exp2(s - offset)
        pv = jnp.dot(vt_ref[:, j * tk:(j + 1) * tk], p.astype(BF16), preferred_element_type=F32)
        if l is None:
            l = jnp.sum(p, axis=0, keepdims=True)
            acc_ref[...] = pv
        else:
            l = l + jnp.sum(p, axis=0, keepdims=True)
            acc_ref[...] += pv
    return l


def _fixed_offset_is_safe(q_norm2_max, kstat_ref):
    worst_exponent = jnp.sqrt(q_norm2_max) * kstat_ref[1:2, 0:1]
    return (jnp.max(worst_exponent) <= ATTN_OFFSET_SLACK).astype(jnp.int32)


def _attend(qs_ref, fixed_ok_ref, krot_ref, vt_ref, kstat_ref, acc_ref, finish, *, tk):
    use_fixed = fixed_ok_ref[0] == 1

    @pl.when(use_fixed)
    def _():
        qf = qs_ref[...].astype(F32)
        ones = jnp.ones((BF16_ROWS, qf.shape[1]), BF16)
        q_norm2 = lax.dot_general(ones, (qf * qf).astype(BF16), (((1,), (1,)), ((), ())),
                                  preferred_element_type=F32)
        offset = jnp.sqrt(q_norm2[0:1]) * kstat_ref[0:1, 0:1]
        finish(_attend_fixed(qs_ref, krot_ref, vt_ref, acc_ref, offset, tk=tk))

    @pl.when(jnp.logical_not(use_fixed))
    def _():
        finish(_attend_online(qs_ref, krot_ref, vt_ref, acc_ref, tk=tk))


def _rms_gain_norm2_bound(gain):
    return gain.shape[1] * jnp.max(gain * gain, axis=1, keepdims=True)


def _rotated_rows(x, cos, sin_signed, pmat, gain_pair):
    if gain_pair is None:
        return _rope_bf16(x, pmat, cos, sin_signed)
    xf = x.astype(F32)
    inv_rms = lax.rsqrt(jnp.mean(xf * xf, axis=-1, keepdims=True) + EPS)
    return _rope_bf16(x, pmat, cos * gain_pair[0], sin_signed * gain_pair[1]) * inv_rms


def _gain_pair(gain):
    tiled = jnp.broadcast_to(gain, (SUBLANES, gain.shape[1]))
    return gain, _rope_partner(tiled)[0:1]


def _prepare_keys_values(k_ref, v_ref, cos_ref, sin_ref, krot_ref, vt_ref, kstat_ref, pmat, gain):
    seq = k_ref.shape[0]
    gain_pair = None if gain is None else _gain_pair(gain)

    def body(c, carry):
        k_sum, k_max2 = carry
        r0 = pl.multiple_of(c * ROW_CHUNK, ROW_CHUNK)
        rows = pl.ds(r0, ROW_CHUNK)
        kb = _rotated_rows(k_ref[rows, :], cos_ref[rows, :], sin_ref[rows, :], pmat,
                           gain_pair).astype(BF16)
        krot_ref[rows, :] = kb
        vt_ref[:, rows] = v_ref[rows, :].T
        kf = kb.astype(F32)
        if gain is None:
            norm2 = jnp.sum(kf * kf, axis=1, keepdims=True)
            k_max2 = jnp.maximum(k_max2, jnp.max(norm2, axis=0, keepdims=True))
        return k_sum + jnp.sum(kf, axis=0, keepdims=True), k_max2

    k_max2_init = (jnp.zeros((1, 1), F32) if gain is None
                   else _rms_gain_norm2_bound(gain))
    k_sum, k_max2 = lax.fori_loop(0, seq // ROW_CHUNK, body,
                                  (jnp.zeros((1, k_ref.shape[1]), F32), k_max2_init),
                                  unroll=PREP_UNROLL)
    k_bar = k_sum * (1.0 / seq)
    k_bar_norm = jnp.sqrt(jnp.sum(k_bar * k_bar, axis=1, keepdims=True))
    k_max = jnp.sqrt(k_max2) * 1.01
    kstat_ref[0:1, :] = jnp.broadcast_to(k_max, (1, kstat_ref.shape[1]))
    kstat_ref[1:2, :] = jnp.broadcast_to(k_max + k_bar_norm, (1, kstat_ref.shape[1]))


def _stacked_row(c, tq):
    per_block = tq // ROW_CHUNK
    return pl.multiple_of((c // per_block) * (2 * tq) + (c % per_block) * ROW_CHUNK, ROW_CHUNK)


def _diff_attn_kernel(q_ref, k_ref, v_ref, cos_ref, sin_ref, lam_ref, g_ref, o_ref,
                      krot_ref, vt_ref, kstat_ref, qs_ref, fixed_ok_ref, acc_ref, *, tq, tk, lam_init):
    @pl.when(pl.program_id(1) == 0)
    def _():
        pmat = _rope_partner_matrix()
        _prepare_keys_values(k_ref, v_ref, cos_ref, sin_ref, krot_ref, vt_ref, kstat_ref, pmat, None)

        def body(c, q_norm2_max):
            rows = pl.ds(pl.multiple_of(c * ROW_CHUNK, ROW_CHUNK), ROW_CHUNK)
            q = _rotated_rows(q_ref[rows, :], cos_ref[rows, :], sin_ref[rows, :], pmat, None) * (
                DIFF_QK_DIM ** -0.5 * LOG2E)
            lane = lax.broadcasted_iota(jnp.int32, q.shape, 1)
            dst = _stacked_row(c, tq)
            qs_ref[pl.ds(dst, ROW_CHUNK), :] = jnp.where(lane < DIFF_QK_DIM, q, 0.0).astype(BF16)
            qs_ref[pl.ds(dst + tq, ROW_CHUNK), :] = jnp.where(lane >= DIFF_QK_DIM, q, 0.0).astype(BF16)
            return jnp.maximum(q_norm2_max,
                               jnp.max(jnp.sum(q * q, axis=1, keepdims=True), axis=0, keepdims=True))

        q_norm2_max = lax.fori_loop(0, q_ref.shape[0] // ROW_CHUNK, body, jnp.zeros((1, 1), F32),
                                    unroll=PREP_UNROLL)
        fixed_ok_ref[0] = _fixed_offset_is_safe(q_norm2_max, kstat_ref)

    qs_blk = qs_ref.at[pl.ds(pl.multiple_of(pl.program_id(1) * (2 * tq), 2 * tq), 2 * tq), :]

    def finish(l):
        lp = lam_ref[...]
        lam = (jnp.exp(jnp.sum(lp[0:1] * lp[1:2], keepdims=True))
               - jnp.exp(jnp.sum(lp[2:3] * lp[3:4], keepdims=True)) + lam_init)
        ot = (acc_ref[:, 0:tq] / l[:, 0:tq] - lam * (acc_ref[:, tq:2 * tq] / l[:, tq:2 * tq]))
        ms = jnp.mean(ot * ot, axis=0, keepdims=True)
        ot = ot * lax.rsqrt(ms + EPS) * (g_ref[...] * (1.0 - lam_init))
        o_ref[...] = ot.T.astype(o_ref.dtype)

    _attend(qs_blk, fixed_ok_ref, krot_ref, vt_ref, kstat_ref, acc_ref, finish, tk=tk)


def _gqa_attn_kernel(q_ref, k_ref, v_ref, cos_ref, sin_ref, qn_ref, kn_ref, o_ref,
                     krot_ref, vt_ref, kstat_ref, qs_ref, fixed_ok_ref, acc_ref, *, tq, tk):
    @pl.when(pl.program_id(1) == 0)
    def _():
        pmat = _rope_partner_matrix()
        _prepare_keys_values(k_ref, v_ref, cos_ref, sin_ref, krot_ref, vt_ref, kstat_ref, pmat,
                             kn_ref[...])

        q_scale = HEAD_DIM ** -0.5 * LOG2E
        q_gain_pair = _gain_pair(qn_ref[...] * q_scale)

        def body(c, _):
            rows = pl.ds(pl.multiple_of(c * ROW_CHUNK, ROW_CHUNK), ROW_CHUNK)
            dst = _stacked_row(c, tq)
            cos, sin_signed = cos_ref[rows, :], sin_ref[rows, :]
            for r in range(2):
                q = _rotated_rows(q_ref[rows, r * HEAD_DIM:(r + 1) * HEAD_DIM], cos, sin_signed,
                                  pmat, q_gain_pair)
                qs_ref[pl.ds(dst + r * tq, ROW_CHUNK), :] = q.astype(BF16)
            return 0

        lax.fori_loop(0, q_ref.shape[0] // ROW_CHUNK, body, 0, unroll=PREP_UNROLL)
        q_norm2_max = _rms_gain_norm2_bound(qn_ref[...]) * (q_scale * q_scale)
        fixed_ok_ref[0] = _fixed_offset_is_safe(q_norm2_max, kstat_ref)

    qs_blk = qs_ref.at[pl.ds(pl.multiple_of(pl.program_id(1) * (2 * tq), 2 * tq), 2 * tq), :]

    def finish(l):
        for r in range(2):
            ot = acc_ref[:, r * tq:(r + 1) * tq] / l[:, r * tq:(r + 1) * tq]
            o_ref[:, r * HEAD_DIM:(r + 1) * HEAD_DIM] = ot.T.astype(o_ref.dtype)

    _attend(qs_blk, fixed_ok_ref, krot_ref, vt_ref, kstat_ref, acc_ref, finish, tk=tk)


def _flash_call(kernel, proj3, tables, params, *, n_groups, q_width, q_col, k_col, v_col,
                out_width, tq, name):
    batch, seq, _ = proj3.shape
    cos_t, sin_t = tables
    qb, kb, vb = q_col // q_width, k_col // HEAD_DIM, v_col // HEAD_DIM

    def grp(i):
        return i // n_groups, i % n_groups

    in_specs = [
        pl.BlockSpec((None, seq, q_width), lambda i, j: (grp(i)[0], 0, qb + grp(i)[1])),
        pl.BlockSpec((None, seq, HEAD_DIM), lambda i, j: (grp(i)[0], 0, kb + grp(i)[1])),
        pl.BlockSpec((None, seq, HEAD_DIM), lambda i, j: (grp(i)[0], 0, vb + grp(i)[1])),
        pl.BlockSpec((seq, LANES), lambda i, j: (0, 0)),
        pl.BlockSpec((seq, LANES), lambda i, j: (0, 0)),
    ] + [pl.BlockSpec(p.shape, lambda i, j: (0, 0)) for p in params]
    return pl.pallas_call(
        kernel,
        grid=(batch * n_groups, seq // tq),
        in_specs=in_specs,
        out_specs=pl.BlockSpec((None, tq, out_width), lambda i, j: (grp(i)[0], j, grp(i)[1])),
        out_shape=jax.ShapeDtypeStruct((batch, seq, GROUP_WIDTH), BF16),
        scratch_shapes=[pltpu.VMEM((seq, HEAD_DIM), BF16),
                        pltpu.VMEM((HEAD_DIM, seq), BF16),
                        pltpu.VMEM((SUBLANES, LANES), F32),
                        pltpu.VMEM((2 * seq, HEAD_DIM), BF16),
                        pltpu.SMEM((1,), jnp.int32),
                        pltpu.VMEM((HEAD_DIM, 2 * tq), F32)],
        compiler_params=_cparams(("parallel", "arbitrary")),
        name=name,
    )(proj3, proj3, proj3, cos_t, sin_t, *params)


def _conv_kernel(a_ref, g_ref, dw_ref, dwb_ref, lng_ref, lnb_ref, pw_ref, pwb_ref, o_ref,
                 u_ref, acc_ref, y_ref, *, tm):
    seq = a_ref.shape[0]
    t = pl.program_id(1)

    @pl.when(t == 0)
    def _():
        zeros = jnp.zeros((CONV_HALO, u_ref.shape[1]), F32)
        u_ref[0:CONV_HALO, :] = zeros
        u_ref[CONV_HALO + seq:2 * CONV_HALO + seq, :] = zeros

        def body(c, _):
            r0 = pl.multiple_of(c * ROW_CHUNK, ROW_CHUNK)
            a = a_ref[pl.ds(r0, ROW_CHUNK), :].astype(F32)
            g = g_ref[pl.ds(r0, ROW_CHUNK), :].astype(F32)
            u_ref[pl.ds(CONV_HALO + r0, ROW_CHUNK), :] = a * jax.nn.sigmoid(g)
            return 0

        lax.fori_loop(0, seq // ROW_CHUNK, body, 0)

    pad = CONV_WIDTH // 2
    sub = 8
    n_ch = u_ref.shape[1]

    def row_block(rb, _):
        base = pl.multiple_of(t * tm + rb * ROW_CHUNK, ROW_CHUNK)
        for cg in range(n_ch // LANES):
            cols = slice(cg * LANES, (cg + 1) * LANES)
            win = u_ref[pl.ds(base, ROW_CHUNK + 2 * CONV_HALO), cols]
            acc = jnp.zeros((ROW_CHUNK, LANES), F32) + dwb_ref[:, cols]
            for r in range(sub):
                shifted = win if r == 0 else pltpu.roll(win, win.shape[0] - r, 0)
                for a in range(2 * CONV_HALO // sub):
                    k = sub * a + r - (CONV_HALO - pad)
                    if 0 <= k < CONV_WIDTH:
                        acc = acc + shifted[sub * a:sub * a + ROW_CHUNK, :] * dw_ref[k:k + 1, cols]
            acc_ref[:, cols] = acc
        acc = acc_ref[...]
        mu = jnp.mean(acc, axis=-1, keepdims=True)
        cen = acc - mu
        var = jnp.mean(cen * cen, axis=-1, keepdims=True)
        y = cen * lax.rsqrt(var + EPS) * lng_ref[...] + lnb_ref[...]
        y_ref[pl.ds(pl.multiple_of(rb * ROW_CHUNK, ROW_CHUNK), ROW_CHUNK), :] = (
            y * jax.nn.sigmoid(y)).astype(BF16)
        return 0

    lax.fori_loop(0, tm // ROW_CHUNK, row_block, 0)
    o_ref[...] = (jnp.dot(y_ref[...], pw_ref[...], preferred_element_type=F32)
                  + pwb_ref[...]).astype(o_ref.dtype)


def _conv_mixer(proj3, dw, dwb, lng, lnb, pw, pwb, *, tm):
    batch, seq, _ = proj3.shape
    ch = GROUP_WIDTH
    vec = pl.BlockSpec((1, ch), lambda b, t: (0, 0))
    return pl.pallas_call(
        functools.partial(_conv_kernel, tm=tm),
        grid=(batch, seq // tm),
        in_specs=[pl.BlockSpec((None, seq, ch), lambda b, t: (b, 0, COL_B_A // ch)),
                  pl.BlockSpec((None, seq, ch), lambda b, t: (b, 0, COL_B_G // ch)),
                  pl.BlockSpec((CONV_WIDTH, ch), lambda b, t: (0, 0)),
                  vec, vec, vec,
                  pl.BlockSpec((ch, ch), lambda b, t: (0, 0)),
                  vec],
        out_specs=pl.BlockSpec((None, tm, ch), lambda b, t: (b, t, 0)),
        out_shape=jax.ShapeDtypeStruct((batch, seq, ch), BF16),
        scratch_shapes=[pltpu.VMEM((seq + 2 * CONV_HALO, ch), F32),
                        pltpu.VMEM((ROW_CHUNK, ch), F32),
                        pltpu.VMEM((tm, ch), BF16)],
        compiler_params=_cparams(("parallel", "arbitrary")),
        name="conv_mixer",
    )(proj3, proj3, dw, dwb, lng, lnb, pw, pwb)


def _na_geometry(seq):
    rows = seq // GRID_W
    qr_blk = Q_BLOCK // GRID_W
    mask_add = np.zeros((NA_CLASSES, Q_BLOCK, NA_BAND), np.float32)
    q_local = np.arange(Q_BLOCK)
    p = np.arange(NA_BAND)
    for c, blk in enumerate(_na_class_blocks(seq)):
        bs = int(np.clip(blk * qr_blk - NA_KH // 2, 0, rows - NA_BAND_ROWS))
        q_row = (blk * qr_blk + q_local // GRID_W)[:, None]
        q_col = (q_local % GRID_W)[:, None]
        k_row = (bs + p // GRID_W)[None, :]
        k_col = (p % GRID_W)[None, :]
        win_r = np.clip(q_row - NA_KH // 2, 0, rows - NA_KH)
        win_c = np.clip(q_col - NA_KW // 2, 0, GRID_W - NA_KW)
        ok = (k_row >= win_r) & (k_row < win_r + NA_KH) & (k_col >= win_c) & (k_col < win_c + NA_KW)
        mask_add[c] = np.where(ok, 0.0, -1e30)
    return jnp.asarray(mask_add)


def _na_class_blocks(seq):
    n_blk = seq // Q_BLOCK
    return (0, 1, 2, n_blk - 2, n_blk - 1)


def _rpb_tiles_kernel(rpb_ref, mask_ref, o_ref, t2_ref, *, seq):
    h = pl.program_id(0)
    n_c = 2 * NA_KW - 1
    n_r = 2 * NA_KH - 1
    shape = (GRID_W, LANES)
    lane = lax.broadcasted_iota(jnp.int32, shape, 1)
    q_col = lax.broadcasted_iota(jnp.int32, shape, 0)
    ic = jnp.clip((lane & (GRID_W - 1)) - q_col + (NA_KW - 1), 0, n_c - 1)
    left = lane < GRID_W
    base = h * (n_r * n_c)
    for i in range(NA_T2):
        i_l = min(max(i - 1, 0), n_r - 1)
        i_r = min(max(i, 0), n_r - 1)

        acc = jnp.zeros(shape, F32)
        for j in range(n_c):
            coef = jnp.where(left, rpb_ref[base + i_l * n_c + j], rpb_ref[base + i_r * n_c + j])
            acc = acc + jnp.where(ic == j, coef, 0.0)
        t2_ref[i] = acc

    rows = seq // GRID_W
    qr_blk = Q_BLOCK // GRID_W
    for c, blk in enumerate(_na_class_blocks(seq)):
        bs = min(max(blk * qr_blk - NA_KH // 2, 0), rows - NA_BAND_ROWS)
        for a in range(qr_blk):
            for r2 in range(NA_BAND_ROWS // 2):
                i1 = bs + 2 * r2 - (blk * qr_blk + a) + (NA_KH - 1)
                idx = min(max(i1, -1), NA_T2 - 2) + 1
                rs = slice(a * GRID_W, (a + 1) * GRID_W)
                cs = slice(r2 * LANES, (r2 + 1) * LANES)
                o_ref[c, rs, cs] = t2_ref[idx] + mask_ref[c, rs, cs]


def _rpb_tiles(rpb, mask_add, seq):
    heads = rpb.shape[0]
    return pl.pallas_call(
        functools.partial(_rpb_tiles_kernel, seq=seq),
        grid=(heads,),
        in_specs=[pl.BlockSpec(memory_space=pltpu.SMEM),
                  pl.BlockSpec(mask_add.shape, lambda h: (0, 0, 0))],
        out_specs=pl.BlockSpec((None,) + mask_add.shape, lambda h: (h, 0, 0, 0)),
        out_shape=jax.ShapeDtypeStruct((heads,) + mask_add.shape, F32),
        scratch_shapes=[pltpu.VMEM((NA_T2, GRID_W, LANES), F32)],
        compiler_params=_cparams(("parallel",)),
        name="na_rpb_tiles",
    )(rpb.reshape(-1), mask_add)


def _na_kernel(q_ref, k_ref, v_ref, bias_ref, o_ref):
    seq = q_ref.shape[0]
    rows = seq // GRID_W
    qr_blk = Q_BLOCK // GRID_W
    n_blk = seq // Q_BLOCK

    classes = _na_class_blocks(seq)

    def band(blk):
        k0 = min(max(blk * qr_blk - NA_KH // 2, 0), rows - NA_BAND_ROWS) * GRID_W
        return slice(k0, k0 + NA_BAND)

    def scores(blk):
        q = q_ref[blk * Q_BLOCK:(blk + 1) * Q_BLOCK, :]
        return lax.dot_general(q, k_ref[band(blk), :], (((1,), (1,)), ((), ())),
                               preferred_element_type=F32)

    pending = [scores(b) for b in range(NA_LOOKAHEAD)]
    for blk in range(n_blk):
        s = pending.pop(0)
        if blk + NA_LOOKAHEAD < n_blk:
            pending.append(scores(blk + NA_LOOKAHEAD))
        cls = classes.index(blk) if blk in classes else 2
        s = s * (HEAD_DIM ** -0.5) + bias_ref[cls]
        m = jnp.max(s, axis=1, keepdims=True)
        p = jnp.exp(s - m)
        l = jnp.sum(p, axis=1, keepdims=True)
        o = jnp.dot(p.astype(BF16), v_ref[band(blk), :], preferred_element_type=F32) / l
        o_ref[blk * Q_BLOCK:(blk + 1) * Q_BLOCK, :] = o.astype(o_ref.dtype)


def _na_mixer(proj3, bias):
    batch, seq, _ = proj3.shape
    heads = GROUP_HEADS

    def head_spec(col):
        return pl.BlockSpec((None, seq, HEAD_DIM), lambda b, h: (b, 0, col // HEAD_DIM + h))

    return pl.pallas_call(
        _na_kernel,
        grid=(batch, heads),
        in_specs=[head_spec(COL_D_Q), head_spec(COL_D_K), head_spec(COL_D_V),
                  pl.BlockSpec((None,) + bias.shape[1:], lambda b, h: (h, 0, 0, 0))],
        out_specs=pl.BlockSpec((None, seq, HEAD_DIM), lambda b, h: (b, 0, h)),
        out_shape=jax.ShapeDtypeStruct((batch, seq, GROUP_WIDTH), BF16),
        compiler_params=_cparams(("parallel", "parallel")),
        name="na_mixer",
    )(proj3, proj3, proj3, bias)


def kernel(x, norm_mix_pre, norm_mix_post, norm_ffn_pre, norm_ffn_post, w_in, w_out, diff_lambda, diff_subln, conv_dw, conv_dw_b, conv_ln_g, conv_ln_b, conv_pw, conv_pw_b, gqa_q_norm, gqa_k_norm, na_rpb, ffn_gate, ffn_up, ffn_down):
    batch, seq, d = x.shape
    depth = w_in.shape[0]
    tokens = batch * seq

    t = np.arange(seq)
    diff_tables = _rope_tables(t, t)
    axial_tables = _rope_tables(t // GRID_W, t % GRID_W)
    mask_add = _na_geometry(seq)

    def row(v):
        return v.reshape(1, -1)

    xf = x.reshape(tokens, d)
    h = _rmsnorm(xf, row(norm_mix_pre[0]), tm=1024)
    for l in range(depth):
        lam_init = 0.8 - 0.6 * math.exp(-0.3 * l)
        proj = _in_proj(h, w_in, l, tm=2048, tn=1024)
        proj3 = proj.reshape(batch, seq, IN_COLS)

        out_a = _flash_call(
            functools.partial(_diff_attn_kernel, tq=ATTN_TQ, tk=ATTN_TK, lam_init=lam_init),
            proj3, diff_tables, (diff_lambda[l], diff_subln[l].reshape(-1, 1)),
            n_groups=GROUP_HEADS, q_width=HEAD_DIM, q_col=COL_A_Q, k_col=COL_A_K, v_col=COL_A_V,
            out_width=HEAD_DIM, tq=ATTN_TQ, name="diff_attn")
        out_b = _conv_mixer(proj3, conv_dw[l], row(conv_dw_b[l]), row(conv_ln_g[l]), row(conv_ln_b[l]),
                            conv_pw[l].astype(BF16), row(conv_pw_b[l]), tm=512)
        out_c = _flash_call(
            functools.partial(_gqa_attn_kernel, tq=ATTN_TQ, tk=ATTN_TK),
            proj3, axial_tables, (row(gqa_q_norm[l]), row(gqa_k_norm[l])),
            n_groups=GQA_KV_HEADS, q_width=2 * HEAD_DIM, q_col=COL_C_Q, k_col=COL_C_K, v_col=COL_C_V,
            out_width=2 * HEAD_DIM, tq=ATTN_TQ, name="gqa_attn")
        out_d = _na_mixer(proj3, _rpb_tiles(na_rpb[l], mask_add, seq))

        parts = [o.reshape(tokens, GROUP_WIDTH) for o in (out_a, out_b, out_c, out_d)]
        xf, h = _out_proj(parts, w_out[l].astype(BF16), row(norm_mix_post[l]), xf,
                          row(norm_ffn_pre[l]), tm=512)

        act, w_down = _ffn_up(h, ffn_gate, ffn_up, ffn_down, l, tm=1024, tn=512)
        next_gain = row(norm_mix_pre[l + 1]) if l + 1 < depth else None
        xf, h = _ffn_down(act, w_down, row(norm_ffn_post[l]), xf, next_gain,
                          tm=256)
    return xf.reshape(batch, seq, d)
```

```python
import functools
import math

import numpy as np
import jax
import jax.numpy as jnp
from jax import lax
from jax.experimental import pallas as pl
from jax.experimental.pallas import tpu as pltpu

D_MODEL = 2048
HEAD_DIM = 128
GROUP_HEADS = 4
GROUP_WIDTH = GROUP_HEADS * HEAD_DIM
DIFF_QK_DIM = HEAD_DIM // 2
CONV_WIDTH = 31
GQA_KV_HEADS = 2
NA_KH = 8
NA_KW = 16
GRID_W = 64
Q_BLOCK = 128
ROPE_THETA = 10000.0
EPS = 1e-6

COL_A_Q, COL_A_K, COL_A_V = 0, 512, 1024
COL_B_A, COL_B_G = 1536, 2048
COL_C_Q, COL_C_K, COL_C_V = 2560, 3072, 3328
COL_D_Q, COL_D_K, COL_D_V = 3584, 4096, 4608
IN_COLS = 5120

LANES = 128
VMEM_LIMIT = 56 * 1024 * 1024
ROW_CHUNK = 128
FFN_UP_SUBTILE = 1024
ATTN_TQ = 512
ATTN_TK = 512

NA_BAND_ROWS = 10
NA_BAND = NA_BAND_ROWS * GRID_W
NA_CLASSES = 5
NA_T2 = 16
NA_LOOKAHEAD = 2
CONV_HALO = 16

F32 = jnp.float32
BF16 = jnp.bfloat16


def _cparams(semantics):
    return pltpu.CompilerParams(dimension_semantics=semantics,
                                vmem_limit_bytes=VMEM_LIMIT)


def _rms(x):
    return x * lax.rsqrt(jnp.mean(x * x, axis=-1, keepdims=True) + EPS)


def _rmsnorm_kernel(x_ref, g_ref, h_ref):
    def body(c, _):
        r0 = pl.multiple_of(c * ROW_CHUNK, ROW_CHUNK)
        h_ref[pl.ds(r0, ROW_CHUNK), :] = (_rms(x_ref[pl.ds(r0, ROW_CHUNK), :]) * g_ref[...]).astype(h_ref.dtype)
        return 0

    lax.fori_loop(0, x_ref.shape[0] // ROW_CHUNK, body, 0)


def _rmsnorm(x, g, *, tm):
    m, d = x.shape
    return pl.pallas_call(
        _rmsnorm_kernel,
        grid=(m // tm,),
        in_specs=[pl.BlockSpec((tm, d), lambda i: (i, 0)),
                  pl.BlockSpec((1, d), lambda i: (0, 0))],
        out_specs=pl.BlockSpec((tm, d), lambda i: (i, 0)),
        out_shape=jax.ShapeDtypeStruct((m, d), BF16),
        compiler_params=_cparams(("parallel",)),
        name="rmsnorm",
    )(x, g)


def _cast_weight_tile(w_ref, wb_ref):
    def body(c, _):
        r0 = pl.multiple_of(c * ROW_CHUNK, ROW_CHUNK)
        wb_ref[pl.ds(r0, ROW_CHUNK), :] = w_ref[pl.ds(r0, ROW_CHUNK), :].astype(wb_ref.dtype)
        return 0

    lax.fori_loop(0, w_ref.shape[0] // ROW_CHUNK, body, 0)


def _in_proj_kernel(h_ref, w_ref, o_ref, wb_ref):
    @pl.when(pl.program_id(1) == 0)
    def _():
        _cast_weight_tile(w_ref, wb_ref)

    o_ref[...] = jnp.dot(h_ref[...], wb_ref[...], preferred_element_type=F32).astype(o_ref.dtype)


def _in_proj(h, w, layer, *, tm, tn):
    m, d = h.shape
    n = w.shape[2]
    return pl.pallas_call(
        _in_proj_kernel,
        grid=(n // tn, m // tm),
        in_specs=[pl.BlockSpec((tm, d), lambda j, i: (i, 0)),
                  pl.BlockSpec((None, d, tn), lambda j, i: (layer, 0, j))],
        out_specs=pl.BlockSpec((tm, tn), lambda j, i: (i, j)),
        out_shape=jax.ShapeDtypeStruct((m, n), BF16),
        scratch_shapes=[pltpu.VMEM((d, tn), BF16)],
        compiler_params=_cparams(("parallel", "arbitrary")),
        name="in_proj",
    )(h, w)


def _ffn_up_kernel(h_ref, wg_ref, wu_ref, wd_ref, o_ref, wdb_ref, wgb_ref, wub_ref):
    @pl.when(pl.program_id(1) == 0)
    def _():
        _cast_weight_tile(wg_ref, wgb_ref)
        _cast_weight_tile(wu_ref, wub_ref)

    wdb_ref[...] = wd_ref[...].astype(wdb_ref.dtype)

    for r in range(h_ref.shape[0] // FFN_UP_SUBTILE):
        rows = slice(r * FFN_UP_SUBTILE, (r + 1) * FFN_UP_SUBTILE)
        h = h_ref[rows, :]
        gate = jnp.dot(h, wgb_ref[...], preferred_element_type=F32)
        up = jnp.dot(h, wub_ref[...], preferred_element_type=F32)
        o_ref[rows, :] = (gate * jax.nn.sigmoid(gate) * up).astype(o_ref.dtype)


def _ffn_up(h, wg, wu, wd, layer, *, tm, tn):
    m, d = h.shape
    n = wg.shape[2]
    n_i = m // tm
    slab = wd.shape[1] // ((n // tn) * n_i)
    assert slab * (n // tn) * n_i == wd.shape[1] and slab % BF16_ROWS == 0
    w_spec = pl.BlockSpec((None, d, tn), lambda j, i: (layer, 0, j))
    return pl.pallas_call(
        _ffn_up_kernel,
        grid=(n // tn, n_i),
        in_specs=[pl.BlockSpec((tm, d), lambda j, i: (i, 0)), w_spec, w_spec,
                  pl.BlockSpec((None, slab, wd.shape[2]), lambda j, i: (layer, j * n_i + i, 0))],
        out_specs=[pl.BlockSpec((tm, tn), lambda j, i: (i, j)),
                   pl.BlockSpec((slab, wd.shape[2]), lambda j, i: (j * n_i + i, 0))],
        out_shape=[jax.ShapeDtypeStruct((m, n), BF16),
                   jax.ShapeDtypeStruct(wd.shape[1:], BF16)],
        scratch_shapes=[pltpu.VMEM((d, tn), BF16), pltpu.VMEM((d, tn), BF16)],
        compiler_params=_cparams(("parallel", "arbitrary")),
        name="ffn_up",
    )(h, wg, wu, wd)


def _project_norm_residual(lhs_ref, w_ref, g_ref, x_ref, gn_ref, o_ref, hn_ref, acc_ref):
    n_part, part, _ = acc_ref.shape
    for b in range(n_part):
        acc_ref[b] = jnp.dot(lhs_ref[b * part:(b + 1) * part, :], w_ref[...],
                             preferred_element_type=F32)
    for b in range(n_part):
        for c in range(part // ROW_CHUNK):
            rows = slice(b * part + c * ROW_CHUNK, b * part + (c + 1) * ROW_CHUNK)
            o = x_ref[rows, :] + _rms(acc_ref[b, c * ROW_CHUNK:(c + 1) * ROW_CHUNK, :]) * g_ref[...]
            o_ref[rows, :] = o
            if hn_ref is not None:
                hn_ref[rows, :] = (_rms(o) * gn_ref[...]).astype(hn_ref.dtype)


def _ffn_down_kernel(*refs, emit_next):
    if emit_next:
        a_ref, w_ref, g_ref, x_ref, gn_ref, o_ref, hn_ref, acc_ref = refs
    else:
        a_ref, w_ref, g_ref, x_ref, o_ref, acc_ref = refs
        gn_ref = hn_ref = None
    _project_norm_residual(a_ref, w_ref, g_ref, x_ref, gn_ref, o_ref, hn_ref, acc_ref)


def _ffn_down(a, w, g, x, gn, *, tm):
    m, kdim = a.shape
    d = w.shape[1]
    emit_next = gn is not None
    vec = pl.BlockSpec((1, d), lambda i: (0, 0))
    row_tile = pl.BlockSpec((tm, d), lambda i: (i, 0))
    outs = pl.pallas_call(
        functools.partial(_ffn_down_kernel, emit_next=emit_next),
        grid=(m // tm,),
        in_specs=[pl.BlockSpec((tm, kdim), lambda i: (i, 0)),
                  pl.BlockSpec((kdim, d), lambda i: (0, 0), pipeline_mode=pl.Buffered(1)),
                  vec, row_tile] + [vec] * emit_next,
        out_specs=[row_tile] + [row_tile] * emit_next,
        out_shape=[jax.ShapeDtypeStruct((m, d), F32)] + [jax.ShapeDtypeStruct((m, d), BF16)] * emit_next,
        scratch_shapes=[pltpu.VMEM((2, tm // 2, d), F32)],
        compiler_params=_cparams(("parallel",)),
        name="ffn_down",
    )(a, w, g, x, *([gn] * emit_next))
    return (outs[0], outs[1]) if emit_next else (outs[0], None)


def _out_proj_kernel(a_ref, b_ref, c_ref, d_ref, w_ref, g_ref, x_ref, gn_ref, o_ref, hn_ref,
                     lhs_ref, acc_ref):
    for p, part in enumerate((a_ref, b_ref, c_ref, d_ref)):
        lhs_ref[:, p * GROUP_WIDTH:(p + 1) * GROUP_WIDTH] = part[...]
    _project_norm_residual(lhs_ref, w_ref, g_ref, x_ref, gn_ref, o_ref, hn_ref, acc_ref)


def _out_proj(parts, w, g, x, gn, *, tm):
    m, d = x.shape
    part_spec = pl.BlockSpec((tm, GROUP_WIDTH), lambda i: (i, 0))
    vec = pl.BlockSpec((1, d), lambda i: (0, 0))
    row_tile = pl.BlockSpec((tm, d), lambda i: (i, 0))
    return pl.pallas_call(
        _out_proj_kernel,
        grid=(m // tm,),
        in_specs=[part_spec, part_spec, part_spec, part_spec,
                  pl.BlockSpec((4 * GROUP_WIDTH, d), lambda i: (0, 0)),
                  vec, row_tile, vec],
        out_specs=[row_tile, row_tile],
        out_shape=[jax.ShapeDtypeStruct((m, d), F32), jax.ShapeDtypeStruct((m, d), BF16)],
        scratch_shapes=[pltpu.VMEM((tm, 4 * GROUP_WIDTH), BF16),
                        pltpu.VMEM((tm // ROW_CHUNK, ROW_CHUNK, d), F32)],
        compiler_params=_cparams(("parallel",)),
        name="out_proj",
    )(*parts, w, g, x, gn)


def _rope_tables(pos_lo, pos_hi):
    half = DIFF_QK_DIM // 2
    inv = ROPE_THETA ** (-np.arange(half, dtype=np.float64) / half)
    lane = np.arange(LANES)
    pos = np.where(lane[None, :] < DIFF_QK_DIM, pos_lo[:, None], pos_hi[:, None]).astype(np.float64)
    ang = pos * inv[lane % half][None, :]
    sign = np.where((lane % DIFF_QK_DIM) < half, -1.0, 1.0)
    return (jnp.asarray(np.cos(ang), dtype=F32),
            jnp.asarray(np.sin(ang) * sign[None, :], dtype=F32))


def _rope_partner(x):
    lane = lax.broadcasted_iota(jnp.int32, x.shape, 1)
    first_half = (lane & (DIFF_QK_DIM // 2)) == 0
    return jnp.where(first_half, pltpu.roll(x, LANES - DIFF_QK_DIM // 2, 1),
                     pltpu.roll(x, DIFF_QK_DIM // 2, 1))


def _rope_partner_matrix():
    src = lax.broadcasted_iota(jnp.int32, (LANES, LANES), 0)
    dst = lax.broadcasted_iota(jnp.int32, (LANES, LANES), 1)
    half = DIFF_QK_DIM // 2
    partner = jnp.where((dst & half) == 0, dst + half, dst - half)
    return jnp.where(src == partner, 1.0, 0.0).astype(BF16)


def _rope_bf16(x, pmat, cos, sin_signed):
    partner = jnp.dot(x, pmat, preferred_element_type=F32)
    return x.astype(F32) * cos + partner * sin_signed


LOG2E = math.log2(math.e)


ATTN_OFFSET_SLACK = 64.0
PREP_UNROLL = 4
SUBLANES = 8
BF16_ROWS = 16


def _scores(krot_ref, qs_ref, j, tk):
    k = krot_ref[j * tk:(j + 1) * tk, :]
    return lax.dot_general(k, qs_ref[...], (((1,), (1,)), ((), ())), preferred_element_type=F32)


def _attend_online(qs_ref, krot_ref, vt_ref, acc_ref, *, tk):
    n_tiles = krot_ref.shape[0] // tk
    m = l = None
    s_next = _scores(krot_ref, qs_ref, 0, tk)
    for j in range(n_tiles):
        s = s_next
        if j + 1 < n_tiles:
            s_next = _scores(krot_ref, qs_ref, j + 1, tk)
        m_cur = jnp.max(s, axis=0, keepdims=True)
        m_next = m_cur if m is None else jnp.maximum(m, m_cur)
        p = jnp.exp2(s - m_next)
        pv = jnp.dot(vt_ref[:, j * tk:(j + 1) * tk], p.astype(BF16), preferred_element_type=F32)
        if m is None:
            l = jnp.sum(p, axis=0, keepdims=True)
            acc_ref[...] = pv
        else:
            alpha = jnp.exp2(m - m_next)
            l = alpha * l + jnp.sum(p, axis=0, keepdims=True)
            acc_ref[...] = alpha * acc_ref[...] + pv
        m = m_next
    return l


def _attend_fixed(qs_ref, krot_ref, vt_ref, acc_ref, offset, *, tk):
    n_tiles = krot_ref.shape[0] // tk
    l = None
    s_next = _scores(krot_ref, qs_ref, 0, tk)
    for j in range(n_tiles):
        s = s_next
        if j + 1 < n_tiles:
            s_next = _scores(krot_ref, qs_ref, j + 1, tk)
        p = jnp.exp2(s - offset)
        pv = jnp.dot(vt_ref[:, j * tk:(j + 1) * tk], p.astype(BF16), preferred_element_type=F32)
        if l is None:
            l = jnp.sum(p, axis=0, keepdims=True)
            acc_ref[...] = pv
        else:
            l = l + jnp.sum(p, axis=0, keepdims=True)
            acc_ref[...] += pv
    return l


def _fixed_offset_is_safe(q_norm2_max, kstat_ref):
    worst_exponent = jnp.sqrt(q_norm2_max) * kstat_ref[1:2, 0:1]
    return (jnp.max(worst_exponent) <= ATTN_OFFSET_SLACK).astype(jnp.int32)


def _attend(qs_ref, fixed_ok_ref, krot_ref, vt_ref, kstat_ref, acc_ref, finish, *, tk):
    use_fixed = fixed_ok_ref[0] == 1

    @pl.when(use_fixed)
    def _():
        qf = qs_ref[...].astype(F32)
        ones = jnp.ones((BF16_ROWS, qf.shape[1]), BF16)
        q_norm2 = lax.dot_general(ones, (qf * qf).astype(BF16), (((1,), (1,)), ((), ())),
                                  preferred_element_type=F32)
        offset = jnp.sqrt(q_norm2[0:1]) * kstat_ref[0:1, 0:1]
        finish(_attend_fixed(qs_ref, krot_ref, vt_ref, acc_ref, offset, tk=tk))

    @pl.when(jnp.logical_not(use_fixed))
    def _():
        finish(_attend_online(qs_ref, krot_ref, vt_ref, acc_ref, tk=tk))


def _rms_gain_norm2_bound(gain):
    return gain.shape[1] * jnp.max(gain * gain, axis=1, keepdims=True)


def _rotated_rows(x, cos, sin_signed, pmat, gain_pair):
    if gain_pair is None:
        return _rope_bf16(x, pmat, cos, sin_signed)
    xf = x.astype(F32)
    inv_rms = lax.rsqrt(jnp.mean(xf * xf, axis=-1, keepdims=True) + EPS)
    return _rope_bf16(x, pmat, cos * gain_pair[0], sin_signed * gain_pair[1]) * inv_rms


def _gain_pair(gain):
    tiled = jnp.broadcast_to(gain, (SUBLANES, gain.shape[1]))
    return gain, _rope_partner(tiled)[0:1]


def _prepare_keys_values(k_ref, v_ref, cos_ref, sin_ref, krot_ref, vt_ref, kstat_ref, pmat, gain):
    seq = k_ref.shape[0]
    gain_pair = None if gain is None else _gain_pair(gain)

    def body(c, carry):
        k_sum, k_max2 = carry
        r0 = pl.multiple_of(c * ROW_CHUNK, ROW_CHUNK)
        rows = pl.ds(r0, ROW_CHUNK)
        kb = _rotated_rows(k_ref[rows, :], cos_ref[rows, :], sin_ref[rows, :], pmat,
                           gain_pair).astype(BF16)
        krot_ref[rows, :] = kb
        vt_ref[:, rows] = v_ref[rows, :].T
        kf = kb.astype(F32)
        if gain is None:
            norm2 = jnp.sum(kf * kf, axis=1, keepdims=True)
            k_max2 = jnp.maximum(k_max2, jnp.max(norm2, axis=0, keepdims=True))
        return k_sum + jnp.sum(kf, axis=0, keepdims=True), k_max2

    k_max2_init = (jnp.zeros((1, 1), F32) if gain is None
                   else _rms_gain_norm2_bound(gain))
    k_sum, k_max2 = lax.fori_loop(0, seq // ROW_CHUNK, body,
                                  (jnp.zeros((1, k_ref.shape[1]), F32), k_max2_init),
                                  unroll=PREP_UNROLL)
    k_bar = k_sum * (1.0 / seq)
    k_bar_norm = jnp.sqrt(jnp.sum(k_bar * k_bar, axis=1, keepdims=True))
    k_max = jnp.sqrt(k_max2) * 1.01
    kstat_ref[0:1, :] = jnp.broadcast_to(k_max, (1, kstat_ref.shape[1]))
    kstat_ref[1:2, :] = jnp.broadcast_to(k_max + k_bar_norm, (1, kstat_ref.shape[1]))


def _stacked_row(c, tq):
    per_block = tq // ROW_CHUNK
    return pl.multiple_of((c // per_block) * (2 * tq) + (c % per_block) * ROW_CHUNK, ROW_CHUNK)


def _diff_attn_kernel(q_ref, k_ref, v_ref, cos_ref, sin_ref, lam_ref, g_ref, o_ref,
                      krot_ref, vt_ref, kstat_ref, qs_ref, fixed_ok_ref, acc_ref, *, tq, tk, lam_init):
    @pl.when(pl.program_id(1) == 0)
    def _():
        pmat = _rope_partner_matrix()
        _prepare_keys_values(k_ref, v_ref, cos_ref, sin_ref, krot_ref, vt_ref, kstat_ref, pmat, None)

        def body(c, q_norm2_max):
            rows = pl.ds(pl.multiple_of(c * ROW_CHUNK, ROW_CHUNK), ROW_CHUNK)
            q = _rotated_rows(q_ref[rows, :], cos_ref[rows, :], sin_ref[rows, :], pmat, None) * (
                DIFF_QK_DIM ** -0.5 * LOG2E)
            lane = lax.broadcasted_iota(jnp.int32, q.shape, 1)
            dst = _stacked_row(c, tq)
            qs_ref[pl.ds(dst, ROW_CHUNK), :] = jnp.where(lane < DIFF_QK_DIM, q, 0.0).astype(BF16)
            qs_ref[pl.ds(dst + tq, ROW_CHUNK), :] = jnp.where(lane >= DIFF_QK_DIM, q, 0.0).astype(BF16)
            return jnp.maximum(q_norm2_max,
                               jnp.max(jnp.sum(q * q, axis=1, keepdims=True), axis=0, keepdims=True))

        q_norm2_max = lax.fori_loop(0, q_ref.shape[0] // ROW_CHUNK, body, jnp.zeros((1, 1), F32),
                                    unroll=PREP_UNROLL)
        fixed_ok_ref[0] = _fixed_offset_is_safe(q_norm2_max, kstat_ref)

    qs_blk = qs_ref.at[pl.ds(pl.multiple_of(pl.program_id(1) * (2 * tq), 2 * tq), 2 * tq), :]

    def finish(l):
        lp = lam_ref[...]
        lam = (jnp.exp(jnp.sum(lp[0:1] * lp[1:2], keepdims=True))
               - jnp.exp(jnp.sum(lp[2:3] * lp[3:4], keepdims=True)) + lam_init)
        ot = (acc_ref[:, 0:tq] / l[:, 0:tq] - lam * (acc_ref[:, tq:2 * tq] / l[:, tq:2 * tq]))
        ms = jnp.mean(ot * ot, axis=0, keepdims=True)
        ot = ot * lax.rsqrt(ms + EPS) * (g_ref[...] * (1.0 - lam_init))
        o_ref[...] = ot.T.astype(o_ref.dtype)

    _attend(qs_blk, fixed_ok_ref, krot_ref, vt_ref, kstat_ref, acc_ref, finish, tk=tk)


def _gqa_attn_kernel(q_ref, k_ref, v_ref, cos_ref, sin_ref, qn_ref, kn_ref, o_ref,
                     krot_ref, vt_ref, kstat_ref, qs_ref, fixed_ok_ref, acc_ref, *, tq, tk):
    @pl.when(pl.program_id(1) == 0)
    def _():
        pmat = _rope_partner_matrix()
        _prepare_keys_values(k_ref, v_ref, cos_ref, sin_ref, krot_ref, vt_ref, kstat_ref, pmat,
                             kn_ref[...])

        q_scale = HEAD_DIM ** -0.5 * LOG2E
        q_gain_pair = _gain_pair(qn_ref[...] * q_scale)

        def body(c, _):
            rows = pl.ds(pl.multiple_of(c * ROW_CHUNK, ROW_CHUNK), ROW_CHUNK)
            dst = _stacked_row(c, tq)
            cos, sin_signed = cos_ref[rows, :], sin_ref[rows, :]
            for r in range(2):
                q = _rotated_rows(q_ref[rows, r * HEAD_DIM:(r + 1) * HEAD_DIM], cos, sin_signed,
                                  pmat, q_gain_pair)
                qs_ref[pl.ds(dst + r * tq, ROW_CHUNK), :] = q.astype(BF16)
            return 0

        lax.fori_loop(0, q_ref.shape[0] // ROW_CHUNK, body, 0, unroll=PREP_UNROLL)
        q_norm2_max = _rms_gain_norm2_bound(qn_ref[...]) * (q_scale * q_scale)
        fixed_ok_ref[0] = _fixed_offset_is_safe(q_norm2_max, kstat_ref)

    qs_blk = qs_ref.at[pl.ds(pl.multiple_of(pl.program_id(1) * (2 * tq), 2 * tq), 2 * tq), :]

    def finish(l):
        for r in range(2):
            ot = acc_ref[:, r * tq:(r + 1) * tq] / l[:, r * tq:(r + 1) * tq]
            o_ref[:, r * HEAD_DIM:(r + 1) * HEAD_DIM] = ot.T.astype(o_ref.dtype)

    _attend(qs_blk, fixed_ok_ref, krot_ref, vt_ref, kstat_ref, acc_ref, finish, tk=tk)


def _flash_call(kernel, proj3, tables, params, *, n_groups, q_width, q_col, k_col, v_col,
                out_width, tq, name):
    batch, seq, _ = proj3.shape
    cos_t, sin_t = tables
    qb, kb, vb = q_col // q_width, k_col // HEAD_DIM, v_col // HEAD_DIM

    def grp(i):
        return i // n_groups, i % n_groups

    in_specs = [
        pl.BlockSpec((None, seq, q_width), lambda i, j: (grp(i)[0], 0, qb + grp(i)[1])),
        pl.BlockSpec((None, seq, HEAD_DIM), lambda i, j: (grp(i)[0], 0, kb + grp(i)[1])),
        pl.BlockSpec((None, seq, HEAD_DIM), lambda i, j: (grp(i)[0], 0, vb + grp(i)[1])),
        pl.BlockSpec((seq, LANES), lambda i, j: (0, 0)),
        pl.BlockSpec((seq, LANES), lambda i, j: (0, 0)),
    ] + [pl.BlockSpec(p.shape, lambda i, j: (0, 0)) for p in params]
    return pl.pallas_call(
        kernel,
        grid=(batch * n_groups, seq // tq),
        in_specs=in_specs,
        out_specs=pl.BlockSpec((None, tq, out_width), lambda i, j: (grp(i)[0], j, grp(i)[1])),
        out_shape=jax.ShapeDtypeStruct((batch, seq, GROUP_WIDTH), BF16),
        scratch_shapes=[pltpu.VMEM((seq, HEAD_DIM), BF16),
                        pltpu.VMEM((HEAD_DIM, seq), BF16),
                        pltpu.VMEM((SUBLANES, LANES), F32),
                        pltpu.VMEM((2 * seq, HEAD_DIM), BF16),
                        pltpu.SMEM((1,), jnp.int32),
                        pltpu.VMEM((HEAD_DIM, 2 * tq), F32)],
        compiler_params=_cparams(("parallel", "arbitrary")),
        name=name,
    )(proj3, proj3, proj3, cos_t, sin_t, *params)


def _conv_kernel(a_ref, g_ref, dw_ref, dwb_ref, lng_ref, lnb_ref, pw_ref, pwb_ref, o_ref,
                 u_ref, acc_ref, y_ref, *, tm):
    seq = a_ref.shape[0]
    t = pl.program_id(1)

    @pl.when(t == 0)
    def _():
        zeros = jnp.zeros((CONV_HALO, u_ref.shape[1]), F32)
        u_ref[0:CONV_HALO, :] = zeros
        u_ref[CONV_HALO + seq:2 * CONV_HALO + seq, :] = zeros

        def body(c, _):
            r0 = pl.multiple_of(c * ROW_CHUNK, ROW_CHUNK)
            a = a_ref[pl.ds(r0, ROW_CHUNK), :].astype(F32)
            g = g_ref[pl.ds(r0, ROW_CHUNK), :].astype(F32)
            u_ref[pl.ds(CONV_HALO + r0, ROW_CHUNK), :] = a * jax.nn.sigmoid(g)
            return 0

        lax.fori_loop(0, seq // ROW_CHUNK, body, 0)

    pad = CONV_WIDTH // 2
    sub = 8
    n_ch = u_ref.shape[1]

    def row_block(rb, _):
        base = pl.multiple_of(t * tm + rb * ROW_CHUNK, ROW_CHUNK)
        for cg in range(n_ch // LANES):
            cols = slice(cg * LANES, (cg + 1) * LANES)
            win = u_ref[pl.ds(base, ROW_CHUNK + 2 * CONV_HALO), cols]
            acc = jnp.zeros((ROW_CHUNK, LANES), F32) + dwb_ref[:, cols]
            for r in range(sub):
                shifted = win if r == 0 else pltpu.roll(win, win.shape[0] - r, 0)
                for a in range(2 * CONV_HALO // sub):
                    k = sub * a + r - (CONV_HALO - pad)
                    if 0 <= k < CONV_WIDTH:
                        acc = acc + shifted[sub * a:sub * a + ROW_CHUNK, :] * dw_ref[k:k + 1, cols]
            acc_ref[:, cols] = acc
        acc = acc_ref[...]
        mu = jnp.mean(acc, axis=-1, keepdims=True)
        cen = acc - mu
        var = jnp.mean(cen * cen, axis=-1, keepdims=True)
        y = cen * lax.rsqrt(var + EPS) * lng_ref[...] + lnb_ref[...]
        y_ref[pl.ds(pl.multiple_of(rb * ROW_CHUNK, ROW_CHUNK), ROW_CHUNK), :] = (
            y * jax.nn.sigmoid(y)).astype(BF16)
        return 0

    lax.fori_loop(0, tm // ROW_CHUNK, row_block, 0)
    o_ref[...] = (jnp.dot(y_ref[...], pw_ref[...], preferred_element_type=F32)
                  + pwb_ref[...]).astype(o_ref.dtype)


def _conv_mixer(proj3, dw, dwb, lng, lnb, pw, pwb, *, tm):
    batch, seq, _ = proj3.shape
    ch = GROUP_WIDTH
    vec = pl.BlockSpec((1, ch), lambda b, t: (0, 0))
    return pl.pallas_call(
        functools.partial(_conv_kernel, tm=tm),
        grid=(batch, seq // tm),
        in_specs=[pl.BlockSpec((None, seq, ch), lambda b, t: (b, 0, COL_B_A // ch)),
                  pl.BlockSpec((None, seq, ch), lambda b, t: (b, 0, COL_B_G // ch)),
                  pl.BlockSpec((CONV_WIDTH, ch), lambda b, t: (0, 0)),
                  vec, vec, vec,
                  pl.BlockSpec((ch, ch), lambda b, t: (0, 0)),
                  vec],
        out_specs=pl.BlockSpec((None, tm, ch), lambda b, t: (b, t, 0)),
        out_shape=jax.ShapeDtypeStruct((batch, seq, ch), BF16),
        scratch_shapes=[pltpu.VMEM((seq + 2 * CONV_HALO, ch), F32),
                        pltpu.VMEM((ROW_CHUNK, ch), F32),
                        pltpu.VMEM((tm, ch), BF16)],
        compiler_params=_cparams(("parallel", "arbitrary")),
        name="conv_mixer",
    )(proj3, proj3, dw, dwb, lng, lnb, pw, pwb)


def _na_geometry(seq):
    rows = seq // GRID_W
    qr_blk = Q_BLOCK // GRID_W
    mask_add = np.zeros((NA_CLASSES, Q_BLOCK, NA_BAND), np.float32)
    q_local = np.arange(Q_BLOCK)
    p = np.arange(NA_BAND)
    for c, blk in enumerate(_na_class_blocks(seq)):
        bs = int(np.clip(blk * qr_blk - NA_KH // 2, 0, rows - NA_BAND_ROWS))
        q_row = (blk * qr_blk + q_local // GRID_W)[:, None]
        q_col = (q_local % GRID_W)[:, None]
        k_row = (bs + p // GRID_W)[None, :]
        k_col = (p % GRID_W)[None, :]
        win_r = np.clip(q_row - NA_KH // 2, 0, rows - NA_KH)
        win_c = np.clip(q_col - NA_KW // 2, 0, GRID_W - NA_KW)
        ok = (k_row >= win_r) & (k_row < win_r + NA_KH) & (k_col >= win_c) & (k_col < win_c + NA_KW)
        mask_add[c] = np.where(ok, 0.0, -1e30)
    return jnp.asarray(mask_add)


def _na_class_blocks(seq):
    n_blk = seq // Q_BLOCK
    return (0, 1, 2, n_blk - 2, n_blk - 1)


def _rpb_tiles_kernel(rpb_ref, mask_ref, o_ref, t2_ref, *, seq):
    h = pl.program_id(0)
    n_c = 2 * NA_KW - 1
    n_r = 2 * NA_KH - 1
    shape = (GRID_W, LANES)
    lane = lax.broadcasted_iota(jnp.int32, shape, 1)
    q_col = lax.broadcasted_iota(jnp.int32, shape, 0)
    ic = jnp.clip((lane & (GRID_W - 1)) - q_col + (NA_KW - 1), 0, n_c - 1)
    left = lane < GRID_W
    base = h * (n_r * n_c)
    for i in range(NA_T2):
        i_l = min(max(i - 1, 0), n_r - 1)
        i_r = min(max(i, 0), n_r - 1)

        acc = jnp.zeros(shape, F32)
        for j in range(n_c):
            coef = jnp.where(left, rpb_ref[base + i_l * n_c + j], rpb_ref[base + i_r * n_c + j])
            acc = acc + jnp.where(ic == j, coef, 0.0)
        t2_ref[i] = acc

    rows = seq // GRID_W
    qr_blk = Q_BLOCK // GRID_W
    for c, blk in enumerate(_na_class_blocks(seq)):
        bs = min(max(blk * qr_blk - NA_KH // 2, 0), rows - NA_BAND_ROWS)
        for a in range(qr_blk):
            for r2 in range(NA_BAND_ROWS // 2):
                i1 = bs + 2 * r2 - (blk * qr_blk + a) + (NA_KH - 1)
                idx = min(max(i1, -1), NA_T2 - 2) + 1
                rs = slice(a * GRID_W, (a + 1) * GRID_W)
                cs = slice(r2 * LANES, (r2 + 1) * LANES)
                o_ref[c, rs, cs] = t2_ref[idx] + mask_ref[c, rs, cs]


def _rpb_tiles(rpb, mask_add, seq):
    heads = rpb.shape[0]
    return pl.pallas_call(
        functools.partial(_rpb_tiles_kernel, seq=seq),
        grid=(heads,),
        in_specs=[pl.BlockSpec(memory_space=pltpu.SMEM),
                  pl.BlockSpec(mask_add.shape, lambda h: (0, 0, 0))],
        out_specs=pl.BlockSpec((None,) + mask_add.shape, lambda h: (h, 0, 0, 0)),
        out_shape=jax.ShapeDtypeStruct((heads,) + mask_add.shape, F32),
        scratch_shapes=[pltpu.VMEM((NA_T2, GRID_W, LANES), F32)],
        compiler_params=_cparams(("parallel",)),
        name="na_rpb_tiles",
    )(rpb.reshape(-1), mask_add)


def _na_kernel(q_ref, k_ref, v_ref, bias_ref, o_ref):
    seq = q_ref.shape[0]
    rows = seq // GRID_W
    qr_blk = Q_BLOCK // GRID_W
    n_blk = seq // Q_BLOCK

    classes = _na_class_blocks(seq)

    def band(blk):
        k0 = min(max(blk * qr_blk - NA_KH // 2, 0), rows - NA_BAND_ROWS) * GRID_W
        return slice(k0, k0 + NA_BAND)

    def scores(blk):
        q = q_ref[blk * Q_BLOCK:(blk + 1) * Q_BLOCK, :]
        return lax.dot_general(q, k_ref[band(blk), :], (((1,), (1,)), ((), ())),
                               preferred_element_type=F32)

    pending = [scores(b) for b in range(NA_LOOKAHEAD)]
    for blk in range(n_blk):
        s = pending.pop(0)
        if blk + NA_LOOKAHEAD < n_blk:
            pending.append(scores(blk + NA_LOOKAHEAD))
        cls = classes.index(blk) if blk in classes else 2
        s = s * (HEAD_DIM ** -0.5) + bias_ref[cls]
        m = jnp.max(s, axis=1, keepdims=True)
        p = jnp.exp(s - m)
        l = jnp.sum(p, axis=1, keepdims=True)
        o = jnp.dot(p.astype(BF16), v_ref[band(blk), :], preferred_element_type=F32) / l
        o_ref[blk * Q_BLOCK:(blk + 1) * Q_BLOCK, :] = o.astype(o_ref.dtype)


def _na_mixer(proj3, bias):
    batch, seq, _ = proj3.shape
    heads = GROUP_HEADS

    def head_spec(col):
        return pl.BlockSpec((None, seq, HEAD_DIM), lambda b, h: (b, 0, col // HEAD_DIM + h))

    return pl.pallas_call(
        _na_kernel,
        grid=(batch, heads),
        in_specs=[head_spec(COL_D_Q), head_spec(COL_D_K), head_spec(COL_D_V),
                  pl.BlockSpec((None,) + bias.shape[1:], lambda b, h: (h, 0, 0, 0))],
        out_specs=pl.BlockSpec((None, seq, HEAD_DIM), lambda b, h: (b, 0, h)),
        out_shape=jax.ShapeDtypeStruct((batch, seq, GROUP_WIDTH), BF16),
        compiler_params=_cparams(("parallel", "parallel")),
        name="na_mixer",
    )(proj3, proj3, proj3, bias)


def kernel(x, norm_mix_pre, norm_mix_post, norm_ffn_pre, norm_ffn_post, w_in, w_out, diff_lambda, diff_subln, conv_dw, conv_dw_b, conv_ln_g, conv_ln_b, conv_pw, conv_pw_b, gqa_q_norm, gqa_k_norm, na_rpb, ffn_gate, ffn_up, ffn_down):
    batch, seq, d = x.shape
    depth = w_in.shape[0]
    tokens = batch * seq

    t = np.arange(seq)
    diff_tables = _rope_tables(t, t)
    axial_tables = _rope_tables(t // GRID_W, t % GRID_W)
    mask_add = _na_geometry(seq)

    def row(v):
        return v.reshape(1, -1)

    xf = x.reshape(tokens, d)
    h = _rmsnorm(xf, row(norm_mix_pre[0]), tm=1024)
    for l in range(depth):
        lam_init = 0.8 - 0.6 * math.exp(-0.3 * l)
        proj = _in_proj(h, w_in, l, tm=2048, tn=1024)
        proj3 = proj.reshape(batch, seq, IN_COLS)

        out_a = _flash_call(
            functools.partial(_diff_attn_kernel, tq=ATTN_TQ, tk=ATTN_TK, lam_init=lam_init),
            proj3, diff_tables, (diff_lambda[l], diff_subln[l].reshape(-1, 1)),
            n_groups=GROUP_HEADS, q_width=HEAD_DIM, q_col=COL_A_Q, k_col=COL_A_K, v_col=COL_A_V,
            out_width=HEAD_DIM, tq=ATTN_TQ, name="diff_attn")
        out_b = _conv_mixer(proj3, conv_dw[l], row(conv_dw_b[l]), row(conv_ln_g[l]), row(conv_ln_b[l]),
                            conv_pw[l].astype(BF16), row(conv_pw_b[l]), tm=512)
        out_c = _flash_call(
            functools.partial(_gqa_attn_kernel, tq=ATTN_TQ, tk=ATTN_TK),
            proj3, axial_tables, (row(gqa_q_norm[l]), row(gqa_k_norm[l])),
            n_groups=GQA_KV_HEADS, q_width=2 * HEAD_DIM, q_col=COL_C_Q, k_col=COL_C_K, v_col=COL_C_V,
            out_width=2 * HEAD_DIM, tq=ATTN_TQ, name="gqa_attn")
        out_d = _na_mixer(proj3, _rpb_tiles(na_rpb[l], mask_add, seq))

        parts = [o.reshape(tokens, GROUP_WIDTH) for o in (out_a, out_b, out_c, out_d)]
        xf, h = _out_proj(parts, w_out[l].astype(BF16), row(norm_mix_post[l]), xf,
                          row(norm_ffn_pre[l]), tm=512)

        act, w_down = _ffn_up(h, ffn_gate, ffn_up, ffn_down, l, tm=2048, tn=512)
        next_gain = row(norm_mix_pre[l + 1]) if l + 1 < depth else None
        xf, h = _ffn_down(act, w_down, row(norm_ffn_post[l]), xf, next_gain,
                          tm=256)
    return xf.reshape(batch, seq, d)
```

```python
import functools
import math

import numpy as np
import jax
import jax.numpy as jnp
from jax import lax
from jax.experimental import pallas as pl
from jax.experimental.pallas import tpu as pltpu

D_MODEL = 2048
HEAD_DIM = 128
GROUP_HEADS = 4
GROUP_WIDTH = GROUP_HEADS * HEAD_DIM
DIFF_QK_DIM = HEAD_DIM // 2
CONV_WIDTH = 31
GQA_KV_HEADS = 2
NA_KH = 8
NA_KW = 16
GRID_W = 64
Q_BLOCK = 128
ROPE_THETA = 10000.0
EPS = 1e-6

COL_A_Q, COL_A_K, COL_A_V = 0, 512, 1024
COL_B_A, COL_B_G = 1536, 2048
COL_C_Q, COL_C_K, COL_C_V = 2560, 3072, 3328
COL_D_Q, COL_D_K, COL_D_V = 3584, 4096, 4608
IN_COLS = 5120

LANES = 128
VMEM_LIMIT = 56 * 1024 * 1024
ROW_CHUNK = 128
FFN_UP_SUBTILE = 1024
ATTN_TQ = 512
ATTN_TK = 512

NA_BAND_ROWS = 10
NA_BAND = NA_BAND_ROWS * GRID_W
NA_CLASSES = 5
NA_T2 = 16
NA_LOOKAHEAD = 2
CONV_HALO = 16

F32 = jnp.float32
BF16 = jnp.bfloat16


def _cparams(semantics):
    return pltpu.CompilerParams(dimension_semantics=semantics,
                                vmem_limit_bytes=VMEM_LIMIT)


def _rms(x):
    return x * lax.rsqrt(jnp.mean(x * x, axis=-1, keepdims=True) + EPS)


def _rmsnorm_kernel(x_ref, g_ref, h_ref):
    def body(c, _):
        r0 = pl.multiple_of(c * ROW_CHUNK, ROW_CHUNK)
        h_ref[pl.ds(r0, ROW_CHUNK), :] = (_rms(x_ref[pl.ds(r0, ROW_CHUNK), :]) * g_ref[...]).astype(h_ref.dtype)
        return 0

    lax.fori_loop(0, x_ref.shape[0] // ROW_CHUNK, body, 0)


def _rmsnorm(x, g, *, tm):
    m, d = x.shape
    return pl.pallas_call(
        _rmsnorm_kernel,
        grid=(m // tm,),
        in_specs=[pl.BlockSpec((tm, d), lambda i: (i, 0)),
                  pl.BlockSpec((1, d), lambda i: (0, 0))],
        out_specs=pl.BlockSpec((tm, d), lambda i: (i, 0)),
        out_shape=jax.ShapeDtypeStruct((m, d), BF16),
        compiler_params=_cparams(("parallel",)),
        name="rmsnorm",
    )(x, g)


def _cast_weight_tile(w_ref, wb_ref):
    def body(c, _):
        r0 = pl.multiple_of(c * ROW_CHUNK, ROW_CHUNK)
        wb_ref[pl.ds(r0, ROW_CHUNK), :] = w_ref[pl.ds(r0, ROW_CHUNK), :].astype(wb_ref.dtype)
        return 0

    lax.fori_loop(0, w_ref.shape[0] // ROW_CHUNK, body, 0)


def _in_proj_kernel(h_ref, w_ref, o_ref, wb_ref):
    @pl.when(pl.program_id(1) == 0)
    def _():
        _cast_weight_tile(w_ref, wb_ref)

    o_ref[...] = jnp.dot(h_ref[...], wb_ref[...], preferred_element_type=F32).astype(o_ref.dtype)


def _in_proj(h, w, layer, *, tm, tn):
    m, d = h.shape
    n = w.shape[2]
    return pl.pallas_call(
        _in_proj_kernel,
        grid=(n // tn, m // tm),
        in_specs=[pl.BlockSpec((tm, d), lambda j, i: (i, 0)),
                  pl.BlockSpec((None, d, tn), lambda j, i: (layer, 0, j))],
        out_specs=pl.BlockSpec((tm, tn), lambda j, i: (i, j)),
        out_shape=jax.ShapeDtypeStruct((m, n), BF16),
        scratch_shapes=[pltpu.VMEM((d, tn), BF16)],
        compiler_params=_cparams(("parallel", "arbitrary")),
        name="in_proj",
    )(h, w)


def _ffn_up_kernel(h_ref, wg_ref, wu_ref, wd_ref, o_ref, wdb_ref, wgb_ref, wub_ref):
    @pl.when(pl.program_id(1) == 0)
    def _():
        _cast_weight_tile(wg_ref, wgb_ref)
        _cast_weight_tile(wu_ref, wub_ref)

    wdb_ref[...] = wd_ref[...].astype(wdb_ref.dtype)

    for r in range(h_ref.shape[0] // FFN_UP_SUBTILE):
        rows = slice(r * FFN_UP_SUBTILE, (r + 1) * FFN_UP_SUBTILE)
        h = h_ref[rows, :]
        gate = jnp.dot(h, wgb_ref[...], preferred_element_type=F32)
        up = jnp.dot(h, wub_ref[...], preferred_element_type=F32)
        o_ref[rows, :] = (gate * jax.nn.sigmoid(gate) * up).astype(o_ref.dtype)


def _ffn_up(h, wg, wu, wd, layer, *, tm, tn):
    m, d = h.shape
    n = wg.shape[2]
    n_i = m // tm
    slab = wd.shape[1] // ((n // tn) * n_i)
    assert slab * (n // tn) * n_i == wd.shape[1] and slab % BF16_ROWS == 0
    w_spec = pl.BlockSpec((None, d, tn), lambda j, i: (layer, 0, j))
    return pl.pallas_call(
        _ffn_up_kernel,
        grid=(n // tn, n_i),
        in_specs=[pl.BlockSpec((tm, d), lambda j, i: (i, 0)), w_spec, w_spec,
                  pl.BlockSpec((None, slab, wd.shape[2]), lambda j, i: (layer, j * n_i + i, 0))],
        out_specs=[pl.BlockSpec((tm, tn), lambda j, i: (i, j)),
                   pl.BlockSpec((slab, wd.shape[2]), lambda j, i: (j * n_i + i, 0))],
        out_shape=[jax.ShapeDtypeStruct((m, n), BF16),
                   jax.ShapeDtypeStruct(wd.shape[1:], BF16)],
        scratch_shapes=[pltpu.VMEM((d, tn), BF16), pltpu.VMEM((d, tn), BF16)],
        compiler_params=_cparams(("parallel", "arbitrary")),
        name="ffn_up",
    )(h, wg, wu, wd)


def _project_norm_residual(lhs_ref, w_ref, g_ref, x_ref, gn_ref, o_ref, hn_ref, acc_ref):
    n_part, part, _ = acc_ref.shape
    for b in range(n_part):
        acc_ref[b] = jnp.dot(lhs_ref[b * part:(b + 1) * part, :], w_ref[...],
                             preferred_element_type=F32)
    for b in range(n_part):
        for c in range(part // ROW_CHUNK):
            rows = slice(b * part + c * ROW_CHUNK, b * part + (c + 1) * ROW_CHUNK)
            o = x_ref[rows, :] + _rms(acc_ref[b, c * ROW_CHUNK:(c + 1) * ROW_CHUNK, :]) * g_ref[...]
            o_ref[rows, :] = o
            if hn_ref is not None:
                hn_ref[rows, :] = (_rms(o) * gn_ref[...]).astype(hn_ref.dtype)


def _ffn_down_kernel(*refs, emit_next):
    if emit_next:
        a_ref, w_ref, g_ref, x_ref, gn_ref, o_ref, hn_ref, acc_ref = refs
    else:
        a_ref, w_ref, g_ref, x_ref, o_ref, acc_ref = refs
        gn_ref = hn_ref = None
    _project_norm_residual(a_ref, w_ref, g_ref, x_ref, gn_ref, o_ref, hn_ref, acc_ref)


def _ffn_down(a, w, g, x, gn, *, tm):
    m, kdim = a.shape
    d = w.shape[1]
    emit_next = gn is not None
    vec = pl.BlockSpec((1, d), lambda i: (0, 0))
    row_tile = pl.BlockSpec((tm, d), lambda i: (i, 0))
    outs = pl.pallas_call(
        functools.partial(_ffn_down_kernel, emit_next=emit_next),
        grid=(m // tm,),
        in_specs=[pl.BlockSpec((tm, kdim), lambda i: (i, 0)),
                  pl.BlockSpec((kdim, d), lambda i: (0, 0), pipeline_mode=pl.Buffered(1)),
                  vec, row_tile] + [vec] * emit_next,
        out_specs=[row_tile] + [row_tile] * emit_next,
        out_shape=[jax.ShapeDtypeStruct((m, d), F32)] + [jax.ShapeDtypeStruct((m, d), BF16)] * emit_next,
        scratch_shapes=[pltpu.VMEM((2, tm // 2, d), F32)],
        compiler_params=_cparams(("parallel",)),
        name="ffn_down",
    )(a, w, g, x, *([gn] * emit_next))
    return (outs[0], outs[1]) if emit_next else (outs[0], None)


def _out_proj_kernel(a_ref, b_ref, c_ref, d_ref, w_ref, g_ref, x_ref, gn_ref, o_ref, hn_ref,
                     lhs_ref, acc_ref):
    for p, part in enumerate((a_ref, b_ref, c_ref, d_ref)):
        lhs_ref[:, p * GROUP_WIDTH:(p + 1) * GROUP_WIDTH] = part[...]
    _project_norm_residual(lhs_ref, w_ref, g_ref, x_ref, gn_ref, o_ref, hn_ref, acc_ref)


def _out_proj(parts, w, g, x, gn, *, tm):
    m, d = x.shape
    part_spec = pl.BlockSpec((tm, GROUP_WIDTH), lambda i: (i, 0))
    vec = pl.BlockSpec((1, d), lambda i: (0, 0))
    row_tile = pl.BlockSpec((tm, d), lambda i: (i, 0))
    return pl.pallas_call(
        _out_proj_kernel,
        grid=(m // tm,),
        in_specs=[part_spec, part_spec, part_spec, part_spec,
                  pl.BlockSpec((4 * GROUP_WIDTH, d), lambda i: (0, 0)),
                  vec, row_tile, vec],
        out_specs=[row_tile, row_tile],
        out_shape=[jax.ShapeDtypeStruct((m, d), F32), jax.ShapeDtypeStruct((m, d), BF16)],
        scratch_shapes=[pltpu.VMEM((tm, 4 * GROUP_WIDTH), BF16),
                        pltpu.VMEM((tm // ROW_CHUNK, ROW_CHUNK, d), F32)],
        compiler_params=_cparams(("parallel",)),
        name="out_proj",
    )(*parts, w, g, x, gn)


def _rope_tables(pos_lo, pos_hi):
    half = DIFF_QK_DIM // 2
    inv = ROPE_THETA ** (-np.arange(half, dtype=np.float64) / half)
    lane = np.arange(LANES)
    pos = np.where(lane[None, :] < DIFF_QK_DIM, pos_lo[:, None], pos_hi[:, None]).astype(np.float64)
    ang = pos * inv[lane % half][None, :]
    sign = np.where((lane % DIFF_QK_DIM) < half, -1.0, 1.0)
    return (jnp.asarray(np.cos(ang), dtype=F32),
            jnp.asarray(np.sin(ang) * sign[None, :], dtype=F32))


def _rope_partner(x):
    lane = lax.broadcasted_iota(jnp.int32, x.shape, 1)
    first_half = (lane & (DIFF_QK_DIM // 2)) == 0
    return jnp.where(first_half, pltpu.roll(x, LANES - DIFF_QK_DIM // 2, 1),
                     pltpu.roll(x, DIFF_QK_DIM // 2, 1))


def _rope_partner_matrix():
    src = lax.broadcasted_iota(jnp.int32, (LANES, LANES), 0)
    dst = lax.broadcasted_iota(jnp.int32, (LANES, LANES), 1)
    half = DIFF_QK_DIM // 2
    partner = jnp.where((dst & half) == 0, dst + half, dst - half)
    return jnp.where(src == partner, 1.0, 0.0).astype(BF16)


def _rope_bf16(x, pmat, cos, sin_signed):
    partner = jnp.dot(x, pmat, preferred_element_type=F32)
    return x.astype(F32) * cos + partner * sin_signed


LOG2E = math.log2(math.e)


ATTN_OFFSET_SLACK = 64.0
PREP_UNROLL = 4
SUBLANES = 8
BF16_ROWS = 16


def _scores(krot_ref, qs_ref, j, tk):
    k = krot_ref[j * tk:(j + 1) * tk, :]
    return lax.dot_general(k, qs_ref[...], (((1,), (1,)), ((), ())), preferred_element_type=F32)


def _attend_online(qs_ref, krot_ref, vt_ref, acc_ref, *, tk):
    n_tiles = krot_ref.shape[0] // tk
    m = l = None
    s_next = _scores(krot_ref, qs_ref, 0, tk)
    for j in range(n_tiles):
        s = s_next
        if j + 1 < n_tiles:
            s_next = _scores(krot_ref, qs_ref, j + 1, tk)
        m_cur = jnp.max(s, axis=0, keepdims=True)
        m_next = m_cur if m is None else jnp.maximum(m, m_cur)
        p = jnp.exp2(s - m_next)
        pv = jnp.dot(vt_ref[:, j * tk:(j + 1) * tk], p.astype(BF16), preferred_element_type=F32)
        if m is None:
            l = jnp.sum(p, axis=0, keepdims=True)
            acc_ref[...] = pv
        else:
            alpha = jnp.exp2(m - m_next)
            l = alpha * l + jnp.sum(p, axis=0, keepdims=True)
            acc_ref[...] = alpha * acc_ref[...] + pv
        m = m_next
    return l


def _attend_fixed(qs_ref, krot_ref, vt_ref, acc_ref, offset, *, tk):
    n_tiles = krot_ref.shape[0] // tk
    l = None
    s_next = _scores(krot_ref, qs_ref, 0, tk)
    for j in range(n_tiles):
        s = s_next
        if j + 1 < n_tiles:
            s_next = _scores(krot_ref, qs_ref, j + 1, tk)
        p = jnp.exp2(s - offset)
        pv = jnp.dot(vt_ref[:, j * tk:(j + 1) * tk], p.astype(BF16), preferred_element_type=F32)
        if l is None:
            l = jnp.sum(p, axis=0, keepdims=True)
            acc_ref[...] = pv
        else:
            l = l + jnp.sum(p, axis=0, keepdims=True)
            acc_ref[...] += pv
    return l


def _fixed_offset_is_safe(q_norm2_max, kstat_ref):
    worst_exponent = jnp.sqrt(q_norm2_max) * kstat_ref[1:2, 0:1]
    return (jnp.max(worst_exponent) <= ATTN_OFFSET_SLACK).astype(jnp.int32)


def _attend(qs_ref, fixed_ok_ref, krot_ref, vt_ref, kstat_ref, acc_ref, finish, *, tk):
    use_fixed = fixed_ok_ref[0] == 1

    @pl.when(use_fixed)
    def _():
        qf = qs_ref[...].astype(F32)
        ones = jnp.ones((BF16_ROWS, qf.shape[1]), BF16)
        q_norm2 = lax.dot_general(ones, (qf * qf).astype(BF16), (((1,), (1,)), ((), ())),
                                  preferred_element_type=F32)
        offset = jnp.sqrt(q_norm2[0:1]) * kstat_ref[0:1, 0:1]
        finish(_attend_fixed(qs_ref, krot_ref, vt_ref, acc_ref, offset, tk=tk))

    @pl.when(jnp.logical_not(use_fixed))
    def _():
        finish(_attend_online(qs_ref, krot_ref, vt_ref, acc_ref, tk=tk))


def _rms_gain_norm2_bound(gain):
    return gain.shape[1] * jnp.max(gain * gain, axis=1, keepdims=True)


def _rotated_rows(x, cos, sin_signed, pmat, gain_pair):
    if gain_pair is None:
        return _rope_bf16(x, pmat, cos, sin_signed)
    xf = x.astype(F32)
    inv_rms = lax.rsqrt(jnp.mean(xf * xf, axis=-1, keepdims=True) + EPS)
    return _rope_bf16(x, pmat, cos * gain_pair[0], sin_signed * gain_pair[1]) * inv_rms


def _gain_pair(gain):
    tiled = jnp.broadcast_to(gain, (SUBLANES, gain.shape[1]))
    return gain, _rope_partner(tiled)[0:1]


def _prepare_keys_values(k_ref, v_ref, cos_ref, sin_ref, krot_ref, vt_ref, kstat_ref, pmat, gain):
    seq = k_ref.shape[0]
    gain_pair = None if gain is None else _gain_pair(gain)

    def body(c, carry):
        k_sum, k_max2 = carry
        r0 = pl.multiple_of(c * ROW_CHUNK, ROW_CHUNK)
        rows = pl.ds(r0, ROW_CHUNK)
        kb = _rotated_rows(k_ref[rows, :], cos_ref[rows, :], sin_ref[rows, :], pmat,
                           gain_pair).astype(BF16)
        krot_ref[rows, :] = kb
        vt_ref[:, rows] = v_ref[rows, :].T
        kf = kb.astype(F32)
        if gain is None:
            norm2 = jnp.sum(kf * kf, axis=1, keepdims=True)
            k_max2 = jnp.maximum(k_max2, jnp.max(norm2, axis=0, keepdims=True))
        return k_sum + jnp.sum(kf, axis=0, keepdims=True), k_max2

    k_max2_init = (jnp.zeros((1, 1), F32) if gain is None
                   else _rms_gain_norm2_bound(gain))
    k_sum, k_max2 = lax.fori_loop(0, seq // ROW_CHUNK, body,
                                  (jnp.zeros((1, k_ref.shape[1]), F32), k_max2_init),
                                  unroll=PREP_UNROLL)
    k_bar = k_sum * (1.0 / seq)
    k_bar_norm = jnp.sqrt(jnp.sum(k_bar * k_bar, axis=1, keepdims=True))
    k_max = jnp.sqrt(k_max2) * 1.01
    kstat_ref[0:1, :] = jnp.broadcast_to(k_max, (1, kstat_ref.shape[1]))
    kstat_ref[1:2, :] = jnp.broadcast_to(k_max + k_bar_norm, (1, kstat_ref.shape[1]))


def _stacked_row(c, tq):
    per_block = tq // ROW_CHUNK
    return pl.multiple_of((c // per_block) * (2 * tq) + (c % per_block) * ROW_CHUNK, ROW_CHUNK)


def _diff_attn_kernel(q_ref, k_ref, v_ref, cos_ref, sin_ref, lam_ref, g_ref, o_ref,
                      krot_ref, vt_ref, kstat_ref, qs_ref, fixed_ok_ref, acc_ref, *, tq, tk, lam_init):
    @pl.when(pl.program_id(1) == 0)
    def _():
        pmat = _rope_partner_matrix()
        _prepare_keys_values(k_ref, v_ref, cos_ref, sin_ref, krot_ref, vt_ref, kstat_ref, pmat, None)

        def body(c, q_norm2_max):
            rows = pl.ds(pl.multiple_of(c * ROW_CHUNK, ROW_CHUNK), ROW_CHUNK)
            q = _rotated_rows(q_ref[rows, :], cos_ref[rows, :], sin_ref[rows, :], pmat, None) * (
                DIFF_QK_DIM ** -0.5 * LOG2E)
            lane = lax.broadcasted_iota(jnp.int32, q.shape, 1)
            dst = _stacked_row(c, tq)
            qs_ref[pl.ds(dst, ROW_CHUNK), :] = jnp.where(lane < DIFF_QK_DIM, q, 0.0).astype(BF16)
            qs_ref[pl.ds(dst + tq, ROW_CHUNK), :] = jnp.where(lane >= DIFF_QK_DIM, q, 0.0).astype(BF16)
            return jnp.maximum(q_norm2_max,
                               jnp.max(jnp.sum(q * q, axis=1, keepdims=True), axis=0, keepdims=True))

        q_norm2_max = lax.fori_loop(0, q_ref.shape[0] // ROW_CHUNK, body, jnp.zeros((1, 1), F32),
                                    unroll=PREP_UNROLL)
        fixed_ok_ref[0] = _fixed_offset_is_safe(q_norm2_max, kstat_ref)

    qs_blk = qs_ref.at[pl.ds(pl.multiple_of(pl.program_id(1) * (2 * tq), 2 * tq), 2 * tq), :]

    def finish(l):
        lp = lam_ref[...]
        lam = (jnp.exp(jnp.sum(lp[0:1] * lp[1:2], keepdims=True))
               - jnp.exp(jnp.sum(lp[2:3] * lp[3:4], keepdims=True)) + lam_init)
        ot = (acc_ref[:, 0:tq] / l[:, 0:tq] - lam * (acc_ref[:, tq:2 * tq] / l[:, tq:2 * tq]))
        ms = jnp.mean(ot * ot, axis=0, keepdims=True)
        ot = ot * lax.rsqrt(ms + EPS) * (g_ref[...] * (1.0 - lam_init))
        o_ref[...] = ot.T.astype(o_ref.dtype)

    _attend(qs_blk, fixed_ok_ref, krot_ref, vt_ref, kstat_ref, acc_ref, finish, tk=tk)


def _gqa_attn_kernel(q_ref, k_ref, v_ref, cos_ref, sin_ref, qn_ref, kn_ref, o_ref,
                     krot_ref, vt_ref, kstat_ref, qs_ref, fixed_ok_ref, acc_ref, *, tq, tk):
    @pl.when(pl.program_id(1) == 0)
    def _():
        pmat = _rope_partner_matrix()
        _prepare_keys_values(k_ref, v_ref, cos_ref, sin_ref, krot_ref, vt_ref, kstat_ref, pmat,
                             kn_ref[...])

        q_scale = HEAD_DIM ** -0.5 * LOG2E
        q_gain_pair = _gain_pair(qn_ref[...] * q_scale)

        def body(c, _):
            rows = pl.ds(pl.multiple_of(c * ROW_CHUNK, ROW_CHUNK), ROW_CHUNK)
            dst = _stacked_row(c, tq)
            cos, sin_signed = cos_ref[rows, :], sin_ref[rows, :]
            for r in range(2):
                q = _rotated_rows(q_ref[rows, r * HEAD_DIM:(r + 1) * HEAD_DIM], cos, sin_signed,
                                  pmat, q_gain_pair)
                qs_ref[pl.ds(dst + r * tq, ROW_CHUNK), :] = q.astype(BF16)
            return 0

        lax.fori_loop(0, q_ref.shape[0] // ROW_CHUNK, body, 0, unroll=PREP_UNROLL)
        q_norm2_max = _rms_gain_norm2_bound(qn_ref[...]) * (q_scale * q_scale)
        fixed_ok_ref[0] = _fixed_offset_is_safe(q_norm2_max, kstat_ref)

    qs_blk = qs_ref.at[pl.ds(pl.multiple_of(pl.program_id(1) * (2 * tq), 2 * tq), 2 * tq), :]

    def finish(l):
        for r in range(2):
            ot = acc_ref[:, r * tq:(r + 1) * tq] / l[:, r * tq:(r + 1) * tq]
            o_ref[:, r * HEAD_DIM:(r + 1) * HEAD_DIM] = ot.T.astype(o_ref.dtype)

    _attend(qs_blk, fixed_ok_ref, krot_ref, vt_ref, kstat_ref, acc_ref, finish, tk=tk)


def _flash_call(kernel, proj3, tables, params, *, n_groups, q_width, q_col, k_col, v_col,
                out_width, tq, name):
    batch, seq, _ = proj3.shape
    cos_t, sin_t = tables
    qb, kb, vb = q_col // q_width, k_col // HEAD_DIM, v_col // HEAD_DIM

    def grp(i):
        return i // n_groups, i % n_groups

    in_specs = [
        pl.BlockSpec((None, seq, q_width), lambda i, j: (grp(i)[0], 0, qb + grp(i)[1])),
        pl.BlockSpec((None, seq, HEAD_DIM), lambda i, j: (grp(i)[0], 0, kb + grp(i)[1])),
        pl.BlockSpec((None, seq, HEAD_DIM), lambda i, j: (grp(i)[0], 0, vb + grp(i)[1])),
        pl.BlockSpec((seq, LANES), lambda i, j: (0, 0)),
        pl.BlockSpec((seq, LANES), lambda i, j: (0, 0)),
    ] + [pl.BlockSpec(p.shape, lambda i, j: (0, 0)) for p in params]
    return pl.pallas_call(
        kernel,
        grid=(batch * n_groups, seq // tq),
        in_specs=in_specs,
        out_specs=pl.BlockSpec((None, tq, out_width), lambda i, j: (grp(i)[0], j, grp(i)[1])),
        out_shape=jax.ShapeDtypeStruct((batch, seq, GROUP_WIDTH), BF16),
        scratch_shapes=[pltpu.VMEM((seq, HEAD_DIM), BF16),
                        pltpu.VMEM((HEAD_DIM, seq), BF16),
                        pltpu.VMEM((SUBLANES, LANES), F32),
                        pltpu.VMEM((2 * seq, HEAD_DIM), BF16),
                        pltpu.SMEM((1,), jnp.int32),
                        pltpu.VMEM((HEAD_DIM, 2 * tq), F32)],
        compiler_params=_cparams(("parallel", "arbitrary")),
        name=name,
    )(proj3, proj3, proj3, cos_t, sin_t, *params)


def _conv_kernel(a_ref, g_ref, dw_ref, dwb_ref, lng_ref, lnb_ref, pw_ref, pwb_ref, wo_ref,
                 o_ref, wob_ref, u_ref, acc_ref, y_ref, *, tm):
    seq = a_ref.shape[0]
    t = pl.program_id(1)

    wob_ref[...] = wo_ref[...].astype(wob_ref.dtype)

    @pl.when(t == 0)
    def _():
        zeros = jnp.zeros((CONV_HALO, u_ref.shape[1]), F32)
        u_ref[0:CONV_HALO, :] = zeros
        u_ref[CONV_HALO + seq:2 * CONV_HALO + seq, :] = zeros

        def body(c, _):
            r0 = pl.multiple_of(c * ROW_CHUNK, ROW_CHUNK)
            a = a_ref[pl.ds(r0, ROW_CHUNK), :].astype(F32)
            g = g_ref[pl.ds(r0, ROW_CHUNK), :].astype(F32)
            u_ref[pl.ds(CONV_HALO + r0, ROW_CHUNK), :] = a * jax.nn.sigmoid(g)
            return 0

        lax.fori_loop(0, seq // ROW_CHUNK, body, 0)

    pad = CONV_WIDTH // 2
    sub = 8
    n_ch = u_ref.shape[1]

    def row_block(rb, _):
        base = pl.multiple_of(t * tm + rb * ROW_CHUNK, ROW_CHUNK)
        for cg in range(n_ch // LANES):
            cols = slice(cg * LANES, (cg + 1) * LANES)
            win = u_ref[pl.ds(base, ROW_CHUNK + 2 * CONV_HALO), cols]
            acc = jnp.zeros((ROW_CHUNK, LANES), F32) + dwb_ref[:, cols]
            for r in range(sub):
                shifted = win if r == 0 else pltpu.roll(win, win.shape[0] - r, 0)
                for a in range(2 * CONV_HALO // sub):
                    k = sub * a + r - (CONV_HALO - pad)
                    if 0 <= k < CONV_WIDTH:
                        acc = acc + shifted[sub * a:sub * a + ROW_CHUNK, :] * dw_ref[k:k + 1, cols]
            acc_ref[:, cols] = acc
        acc = acc_ref[...]
        mu = jnp.mean(acc, axis=-1, keepdims=True)
        cen = acc - mu
        var = jnp.mean(cen * cen, axis=-1, keepdims=True)
        y = cen * lax.rsqrt(var + EPS) * lng_ref[...] + lnb_ref[...]
        y_ref[pl.ds(pl.multiple_of(rb * ROW_CHUNK, ROW_CHUNK), ROW_CHUNK), :] = (
            y * jax.nn.sigmoid(y)).astype(BF16)
        return 0

    lax.fori_loop(0, tm // ROW_CHUNK, row_block, 0)
    o_ref[...] = (jnp.dot(y_ref[...], pw_ref[...].astype(BF16), preferred_element_type=F32)
                  + pwb_ref[...]).astype(o_ref.dtype)


def _conv_mixer(proj3, dw, dwb, lng, lnb, pw, pwb, w_out, layer, *, tm):
    batch, seq, _ = proj3.shape
    ch = GROUP_WIDTH
    n_t = seq // tm
    slab = w_out.shape[1] // (batch * n_t)
    assert slab * batch * n_t == w_out.shape[1] and slab % BF16_ROWS == 0
    vec = pl.BlockSpec((1, ch), lambda b, t: (0, 0))
    return pl.pallas_call(
        functools.partial(_conv_kernel, tm=tm),
        grid=(batch, n_t),
        in_specs=[pl.BlockSpec((None, seq, ch), lambda b, t: (b, 0, COL_B_A // ch)),
                  pl.BlockSpec((None, seq, ch), lambda b, t: (b, 0, COL_B_G // ch)),
                  pl.BlockSpec((CONV_WIDTH, ch), lambda b, t: (0, 0)),
                  vec, vec, vec,
                  pl.BlockSpec((None, ch, ch), lambda b, t: (layer, 0, 0)),
                  vec,
                  pl.BlockSpec((None, slab, w_out.shape[2]), lambda b, t: (layer, b * n_t + t, 0))],
        out_specs=[pl.BlockSpec((None, tm, ch), lambda b, t: (b, t, 0)),
                   pl.BlockSpec((slab, w_out.shape[2]), lambda b, t: (b * n_t + t, 0))],
        out_shape=[jax.ShapeDtypeStruct((batch, seq, ch), BF16),
                   jax.ShapeDtypeStruct(w_out.shape[1:], BF16)],
        scratch_shapes=[pltpu.VMEM((seq + 2 * CONV_HALO, ch), F32),
                        pltpu.VMEM((ROW_CHUNK, ch), F32),
                        pltpu.VMEM((tm, ch), BF16)],
        compiler_params=_cparams(("parallel", "arbitrary")),
        name="conv_mixer",
    )(proj3, proj3, dw, dwb, lng, lnb, pw, pwb, w_out)


def _na_geometry(seq):
    rows = seq // GRID_W
    qr_blk = Q_BLOCK // GRID_W
    mask_add = np.zeros((NA_CLASSES, Q_BLOCK, NA_BAND), np.float32)
    q_local = np.arange(Q_BLOCK)
    p = np.arange(NA_BAND)
    for c, blk in enumerate(_na_class_blocks(seq)):
        bs = int(np.clip(blk * qr_blk - NA_KH // 2, 0, rows - NA_BAND_ROWS))
        q_row = (blk * qr_blk + q_local // GRID_W)[:, None]
        q_col = (q_local % GRID_W)[:, None]
        k_row = (bs + p // GRID_W)[None, :]
        k_col = (p % GRID_W)[None, :]
        win_r = np.clip(q_row - NA_KH // 2, 0, rows - NA_KH)
        win_c = np.clip(q_col - NA_KW // 2, 0, GRID_W - NA_KW)
        ok = (k_row >= win_r) & (k_row < win_r + NA_KH) & (k_col >= win_c) & (k_col < win_c + NA_KW)
        mask_add[c] = np.where(ok, 0.0, -1e30)
    return jnp.asarray(mask_add)


def _na_class_blocks(seq):
    n_blk = seq // Q_BLOCK
    return (0, 1, 2, n_blk - 2, n_blk - 1)


def _rpb_tiles_kernel(rpb_ref, mask_ref, o_ref, t2_ref, *, seq):
    h = pl.program_id(0)
    n_c = 2 * NA_KW - 1
    n_r = 2 * NA_KH - 1
    shape = (GRID_W, LANES)
    lane = lax.broadcasted_iota(jnp.int32, shape, 1)
    q_col = lax.broadcasted_iota(jnp.int32, shape, 0)
    ic = jnp.clip((lane & (GRID_W - 1)) - q_col + (NA_KW - 1), 0, n_c - 1)
    left = lane < GRID_W
    base = h * (n_r * n_c)
    for i in range(NA_T2):
        i_l = min(max(i - 1, 0), n_r - 1)
        i_r = min(max(i, 0), n_r - 1)

        acc = jnp.zeros(shape, F32)
        for j in range(n_c):
            coef = jnp.where(left, rpb_ref[base + i_l * n_c + j], rpb_ref[base + i_r * n_c + j])
            acc = acc + jnp.where(ic == j, coef, 0.0)
        t2_ref[i] = acc

    rows = seq // GRID_W
    qr_blk = Q_BLOCK // GRID_W
    for c, blk in enumerate(_na_class_blocks(seq)):
        bs = min(max(blk * qr_blk - NA_KH // 2, 0), rows - NA_BAND_ROWS)
        for a in range(qr_blk):
            for r2 in range(NA_BAND_ROWS // 2):
                i1 = bs + 2 * r2 - (blk * qr_blk + a) + (NA_KH - 1)
                idx = min(max(i1, -1), NA_T2 - 2) + 1
                rs = slice(a * GRID_W, (a + 1) * GRID_W)
                cs = slice(r2 * LANES, (r2 + 1) * LANES)
                o_ref[c, rs, cs] = (t2_ref[idx] + mask_ref[c, rs, cs]) * LOG2E


def _rpb_tiles(rpb, mask_add, seq):
    heads = rpb.shape[0]
    return pl.pallas_call(
        functools.partial(_rpb_tiles_kernel, seq=seq),
        grid=(heads,),
        in_specs=[pl.BlockSpec(memory_space=pltpu.SMEM),
                  pl.BlockSpec(mask_add.shape, lambda h: (0, 0, 0))],
        out_specs=pl.BlockSpec((None,) + mask_add.shape, lambda h: (h, 0, 0, 0)),
        out_shape=jax.ShapeDtypeStruct((heads,) + mask_add.shape, F32),
        scratch_shapes=[pltpu.VMEM((NA_T2, GRID_W, LANES), F32)],
        compiler_params=_cparams(("parallel",)),
        name="na_rpb_tiles",
    )(rpb.reshape(-1), mask_add)


def _na_kernel(q_ref, k_ref, v_ref, bias_ref, o_ref):
    seq = q_ref.shape[0]
    rows = seq // GRID_W
    qr_blk = Q_BLOCK // GRID_W
    n_blk = seq // Q_BLOCK

    classes = _na_class_blocks(seq)

    def band(blk):
        k0 = min(max(blk * qr_blk - NA_KH // 2, 0), rows - NA_BAND_ROWS) * GRID_W
        return slice(k0, k0 + NA_BAND)

    def scores(blk):
        q = q_ref[blk * Q_BLOCK:(blk + 1) * Q_BLOCK, :]
        return lax.dot_general(q, k_ref[band(blk), :], (((1,), (1,)), ((), ())),
                               preferred_element_type=F32)

    pending = [scores(b) for b in range(NA_LOOKAHEAD)]
    for blk in range(n_blk):
        s = pending.pop(0)
        if blk + NA_LOOKAHEAD < n_blk:
            pending.append(scores(blk + NA_LOOKAHEAD))
        cls = classes.index(blk) if blk in classes else 2
        s = s * (HEAD_DIM ** -0.5 * LOG2E) + bias_ref[cls]
        m = jnp.max(s, axis=1, keepdims=True)
        p = jnp.exp2(s - m)
        l = jnp.sum(p, axis=1, keepdims=True)
        o = jnp.dot(p.astype(BF16), v_ref[band(blk), :], preferred_element_type=F32) / l
        o_ref[blk * Q_BLOCK:(blk + 1) * Q_BLOCK, :] = o.astype(o_ref.dtype)


def _na_mixer(proj3, bias):
    batch, seq, _ = proj3.shape
    heads = GROUP_HEADS

    def head_spec(col):
        return pl.BlockSpec((None, seq, HEAD_DIM), lambda b, h: (b, 0, col // HEAD_DIM + h))

    return pl.pallas_call(
        _na_kernel,
        grid=(batch, heads),
        in_specs=[head_spec(COL_D_Q), head_spec(COL_D_K), head_spec(COL_D_V),
                  pl.BlockSpec((None,) + bias.shape[1:], lambda b, h: (h, 0, 0, 0))],
        out_specs=pl.BlockSpec((None, seq, HEAD_DIM), lambda b, h: (b, 0, h)),
        out_shape=jax.ShapeDtypeStruct((batch, seq, GROUP_WIDTH), BF16),
        compiler_params=_cparams(("parallel", "parallel")),
        name="na_mixer",
    )(proj3, proj3, proj3, bias)


def kernel(x, norm_mix_pre, norm_mix_post, norm_ffn_pre, norm_ffn_post, w_in, w_out, diff_lambda, diff_subln, conv_dw, conv_dw_b, conv_ln_g, conv_ln_b, conv_pw, conv_pw_b, gqa_q_norm, gqa_k_norm, na_rpb, ffn_gate, ffn_up, ffn_down):
    batch, seq, d = x.shape
    depth = w_in.shape[0]
    tokens = batch * seq

    t = np.arange(seq)
    diff_tables = _rope_tables(t, t)
    axial_tables = _rope_tables(t // GRID_W, t % GRID_W)
    mask_add = _na_geometry(seq)

    def row(v):
        return v.reshape(1, -1)

    xf = x.reshape(tokens, d)
    h = _rmsnorm(xf, row(norm_mix_pre[0]), tm=1024)
    for l in range(depth):
        lam_init = 0.8 - 0.6 * math.exp(-0.3 * l)
        proj = _in_proj(h, w_in, l, tm=2048, tn=1024)
        proj3 = proj.reshape(batch, seq, IN_COLS)

        out_a = _flash_call(
            functools.partial(_diff_attn_kernel, tq=ATTN_TQ, tk=ATTN_TK, lam_init=lam_init),
            proj3, diff_tables, (diff_lambda[l], diff_subln[l].reshape(-1, 1)),
            n_groups=GROUP_HEADS, q_width=HEAD_DIM, q_col=COL_A_Q, k_col=COL_A_K, v_col=COL_A_V,
            out_width=HEAD_DIM, tq=ATTN_TQ, name="diff_attn")
        out_b, w_out_bf16 = _conv_mixer(
            proj3, conv_dw[l], row(conv_dw_b[l]), row(conv_ln_g[l]), row(conv_ln_b[l]),
            conv_pw, row(conv_pw_b[l]), w_out, l, tm=512)
        out_c = _flash_call(
            functools.partial(_gqa_attn_kernel, tq=ATTN_TQ, tk=ATTN_TK),
            proj3, axial_tables, (row(gqa_q_norm[l]), row(gqa_k_norm[l])),
            n_groups=GQA_KV_HEADS, q_width=2 * HEAD_DIM, q_col=COL_C_Q, k_col=COL_C_K, v_col=COL_C_V,
            out_width=2 * HEAD_DIM, tq=ATTN_TQ, name="gqa_attn")
        out_d = _na_mixer(proj3, _rpb_tiles(na_rpb[l], mask_add, seq))

        parts = [o.reshape(tokens, GROUP_WIDTH) for o in (out_a, out_b, out_c, out_d)]
        xf, h = _out_proj(parts, w_out_bf16, row(norm_mix_post[l]), xf,
                          row(norm_ffn_pre[l]), tm=512)

        act, w_down = _ffn_up(h, ffn_gate, ffn_up, ffn_down, l, tm=2048, tn=512)
        next_gain = row(norm_mix_pre[l + 1]) if l + 1 < depth else None
        xf, h = _ffn_down(act, w_down, row(norm_ffn_post[l]), xf, next_gain,
                          tm=256)
    return xf.reshape(batch, seq, d)
```

```python
import functools
import math

import numpy as np
import jax
import jax.numpy as jnp
from jax import lax
from jax.experimental import pallas as pl
from jax.experimental.pallas import tpu as pltpu

D_MODEL = 2048
HEAD_DIM = 128
GROUP_HEADS = 4
GROUP_WIDTH = GROUP_HEADS * HEAD_DIM
DIFF_QK_DIM = HEAD_DIM // 2
CONV_WIDTH = 31
GQA_KV_HEADS = 2
NA_KH = 8
NA_KW = 16
GRID_W = 64
Q_BLOCK = 128
ROPE_THETA = 10000.0
EPS = 1e-6

COL_A_Q, COL_A_K, COL_A_V = 0, 512, 1024
COL_B_A, COL_B_G = 1536, 2048
COL_C_Q, COL_C_K, COL_C_V = 2560, 3072, 3328
COL_D_Q, COL_D_K, COL_D_V = 3584, 4096, 4608
IN_COLS = 5120

LANES = 128
VMEM_LIMIT = 56 * 1024 * 1024
ROW_CHUNK = 128
FFN_UP_SUBTILE = 1024
ATTN_TQ = 512
ATTN_TK = 512

NA_BAND_ROWS = 10
NA_BAND = NA_BAND_ROWS * GRID_W
NA_CLASSES = 5
NA_T2 = 16
NA_LOOKAHEAD = 2
CONV_HALO = 16

F32 = jnp.float32
BF16 = jnp.bfloat16


def _cparams(semantics):
    return pltpu.CompilerParams(dimension_semantics=semantics,
                                vmem_limit_bytes=VMEM_LIMIT)


def _rms(x):
    return x * lax.rsqrt(jnp.mean(x * x, axis=-1, keepdims=True) + EPS)


def _rmsnorm_kernel(x_ref, g_ref, h_ref):
    def body(c, _):
        r0 = pl.multiple_of(c * ROW_CHUNK, ROW_CHUNK)
        h_ref[pl.ds(r0, ROW_CHUNK), :] = (_rms(x_ref[pl.ds(r0, ROW_CHUNK), :]) * g_ref[...]).astype(h_ref.dtype)
        return 0

    lax.fori_loop(0, x_ref.shape[0] // ROW_CHUNK, body, 0)


def _rmsnorm(x, g, *, tm):
    m, d = x.shape
    return pl.pallas_call(
        _rmsnorm_kernel,
        grid=(m // tm,),
        in_specs=[pl.BlockSpec((tm, d), lambda i: (i, 0)),
                  pl.BlockSpec((1, d), lambda i: (0, 0))],
        out_specs=pl.BlockSpec((tm, d), lambda i: (i, 0)),
        out_shape=jax.ShapeDtypeStruct((m, d), BF16),
        compiler_params=_cparams(("parallel",)),
        name="rmsnorm",
    )(x, g)


def _cast_weight_tile(w_ref, wb_ref):
    def body(c, _):
        r0 = pl.multiple_of(c * ROW_CHUNK, ROW_CHUNK)
        wb_ref[pl.ds(r0, ROW_CHUNK), :] = w_ref[pl.ds(r0, ROW_CHUNK), :].astype(wb_ref.dtype)
        return 0

    lax.fori_loop(0, w_ref.shape[0] // ROW_CHUNK, body, 0)


def _in_proj_kernel(h_ref, w_ref, o_ref, wb_ref):
    @pl.when(pl.program_id(1) == 0)
    def _():
        _cast_weight_tile(w_ref, wb_ref)

    o_ref[...] = jnp.dot(h_ref[...], wb_ref[...], preferred_element_type=F32).astype(o_ref.dtype)


def _in_proj(h, w, layer, *, tm, tn):
    m, d = h.shape
    n = w.shape[2]
    return pl.pallas_call(
        _in_proj_kernel,
        grid=(n // tn, m // tm),
        in_specs=[pl.BlockSpec((tm, d), lambda j, i: (i, 0)),
                  pl.BlockSpec((None, d, tn), lambda j, i: (layer, 0, j))],
        out_specs=pl.BlockSpec((tm, tn), lambda j, i: (i, j)),
        out_shape=jax.ShapeDtypeStruct((m, n), BF16),
        scratch_shapes=[pltpu.VMEM((d, tn), BF16)],
        compiler_params=_cparams(("parallel", "arbitrary")),
        name="in_proj",
    )(h, w)


def _ffn_up_kernel(h_ref, wg_ref, wu_ref, wd_ref, o_ref, wdb_ref, wgb_ref, wub_ref):
    @pl.when(pl.program_id(1) == 0)
    def _():
        _cast_weight_tile(wg_ref, wgb_ref)
        _cast_weight_tile(wu_ref, wub_ref)

    wdb_ref[...] = wd_ref[...].astype(wdb_ref.dtype)

    for r in range(h_ref.shape[0] // FFN_UP_SUBTILE):
        rows = slice(r * FFN_UP_SUBTILE, (r + 1) * FFN_UP_SUBTILE)
        h = h_ref[rows, :]
        gate = jnp.dot(h, wgb_ref[...], preferred_element_type=F32)
        up = jnp.dot(h, wub_ref[...], preferred_element_type=F32)
        o_ref[rows, :] = (gate * jax.nn.sigmoid(gate) * up).astype(o_ref.dtype)


def _ffn_up(h, wg, wu, wd, layer, *, tm, tn):
    m, d = h.shape
    n = wg.shape[2]
    n_i = m // tm
    slab = wd.shape[1] // ((n // tn) * n_i)
    assert slab * (n // tn) * n_i == wd.shape[1] and slab % BF16_ROWS == 0
    w_spec = pl.BlockSpec((None, d, tn), lambda j, i: (layer, 0, j))
    return pl.pallas_call(
        _ffn_up_kernel,
        grid=(n // tn, n_i),
        in_specs=[pl.BlockSpec((tm, d), lambda j, i: (i, 0)), w_spec, w_spec,
                  pl.BlockSpec((None, slab, wd.shape[2]), lambda j, i: (layer, j * n_i + i, 0))],
        out_specs=[pl.BlockSpec((tm, tn), lambda j, i: (i, j)),
                   pl.BlockSpec((slab, wd.shape[2]), lambda j, i: (j * n_i + i, 0))],
        out_shape=[jax.ShapeDtypeStruct((m, n), BF16),
                   jax.ShapeDtypeStruct(wd.shape[1:], BF16)],
        scratch_shapes=[pltpu.VMEM((d, tn), BF16), pltpu.VMEM((d, tn), BF16)],
        compiler_params=_cparams(("parallel", "arbitrary")),
        name="ffn_up",
    )(h, wg, wu, wd)


def _project_norm_residual(lhs_ref, w_ref, g_ref, x_ref, gn_ref, o_ref, hn_ref, acc_ref):
    n_part, part, _ = acc_ref.shape
    for b in range(n_part):
        acc_ref[b] = jnp.dot(lhs_ref[b * part:(b + 1) * part, :], w_ref[...],
                             preferred_element_type=F32)
    for b in range(n_part):
        for c in range(part // ROW_CHUNK):
            rows = slice(b * part + c * ROW_CHUNK, b * part + (c + 1) * ROW_CHUNK)
            o = x_ref[rows, :] + _rms(acc_ref[b, c * ROW_CHUNK:(c + 1) * ROW_CHUNK, :]) * g_ref[...]
            o_ref[rows, :] = o
            if hn_ref is not None:
                hn_ref[rows, :] = (_rms(o) * gn_ref[...]).astype(hn_ref.dtype)


def _ffn_down_kernel(*refs, emit_next):
    if emit_next:
        a_ref, w_ref, g_ref, x_ref, gn_ref, o_ref, hn_ref, acc_ref = refs
    else:
        a_ref, w_ref, g_ref, x_ref, o_ref, acc_ref = refs
        gn_ref = hn_ref = None
    _project_norm_residual(a_ref, w_ref, g_ref, x_ref, gn_ref, o_ref, hn_ref, acc_ref)


def _ffn_down(a, w, g, x, gn, *, tm):
    m, kdim = a.shape
    d = w.shape[1]
    emit_next = gn is not None
    vec = pl.BlockSpec((1, d), lambda i: (0, 0))
    row_tile = pl.BlockSpec((tm, d), lambda i: (i, 0))
    outs = pl.pallas_call(
        functools.partial(_ffn_down_kernel, emit_next=emit_next),
        grid=(m // tm,),
        in_specs=[pl.BlockSpec((tm, kdim), lambda i: (i, 0)),
                  pl.BlockSpec((kdim, d), lambda i: (0, 0), pipeline_mode=pl.Buffered(1)),
                  vec, row_tile] + [vec] * emit_next,
        out_specs=[row_tile] + [row_tile] * emit_next,
        out_shape=[jax.ShapeDtypeStruct((m, d), F32)] + [jax.ShapeDtypeStruct((m, d), BF16)] * emit_next,
        scratch_shapes=[pltpu.VMEM((2, tm // 2, d), F32)],
        compiler_params=_cparams(("parallel",)),
        name="ffn_down",
    )(a, w, g, x, *([gn] * emit_next))
    return (outs[0], outs[1]) if emit_next else (outs[0], None)


def _out_proj_kernel(a_ref, b_ref, c_ref, d_ref, w_ref, g_ref, x_ref, gn_ref, o_ref, hn_ref,
                     lhs_ref, acc_ref):
    for p, part in enumerate((a_ref, b_ref, c_ref, d_ref)):
        lhs_ref[:, p * GROUP_WIDTH:(p + 1) * GROUP_WIDTH] = part[...]
    _project_norm_residual(lhs_ref, w_ref, g_ref, x_ref, gn_ref, o_ref, hn_ref, acc_ref)


def _out_proj(parts, w, g, x, gn, *, tm):
    m, d = x.shape
    part_spec = pl.BlockSpec((tm, GROUP_WIDTH), lambda i: (i, 0))
    vec = pl.BlockSpec((1, d), lambda i: (0, 0))
    row_tile = pl.BlockSpec((tm, d), lambda i: (i, 0))
    return pl.pallas_call(
        _out_proj_kernel,
        grid=(m // tm,),
        in_specs=[part_spec, part_spec, part_spec, part_spec,
                  pl.BlockSpec((4 * GROUP_WIDTH, d), lambda i: (0, 0)),
                  vec, row_tile, vec],
        out_specs=[row_tile, row_tile],
        out_shape=[jax.ShapeDtypeStruct((m, d), F32), jax.ShapeDtypeStruct((m, d), BF16)],
        scratch_shapes=[pltpu.VMEM((tm, 4 * GROUP_WIDTH), BF16),
                        pltpu.VMEM((tm // ROW_CHUNK, ROW_CHUNK, d), F32)],
        compiler_params=_cparams(("parallel",)),
        name="out_proj",
    )(*parts, w, g, x, gn)


def _rope_tables(pos_lo, pos_hi):
    half = DIFF_QK_DIM // 2
    inv = ROPE_THETA ** (-np.arange(half, dtype=np.float64) / half)
    lane = np.arange(LANES)
    pos = np.where(lane[None, :] < DIFF_QK_DIM, pos_lo[:, None], pos_hi[:, None]).astype(np.float64)
    ang = pos * inv[lane % half][None, :]
    sign = np.where((lane % DIFF_QK_DIM) < half, -1.0, 1.0)
    return (jnp.asarray(np.cos(ang), dtype=F32),
            jnp.asarray(np.sin(ang) * sign[None, :], dtype=F32))


def _rope_partner(x):
    lane = lax.broadcasted_iota(jnp.int32, x.shape, 1)
    first_half = (lane & (DIFF_QK_DIM // 2)) == 0
    return jnp.where(first_half, pltpu.roll(x, LANES - DIFF_QK_DIM // 2, 1),
                     pltpu.roll(x, DIFF_QK_DIM // 2, 1))


def _rope_partner_matrix():
    src = lax.broadcasted_iota(jnp.int32, (LANES, LANES), 0)
    dst = lax.broadcasted_iota(jnp.int32, (LANES, LANES), 1)
    half = DIFF_QK_DIM // 2
    partner = jnp.where((dst & half) == 0, dst + half, dst - half)
    return jnp.where(src == partner, 1.0, 0.0).astype(BF16)


def _rope_bf16(x, pmat, cos, sin_signed):
    partner = jnp.dot(x, pmat, preferred_element_type=F32)
    return x.astype(F32) * cos + partner * sin_signed


LOG2E = math.log2(math.e)


ATTN_OFFSET_SLACK = 64.0
PREP_UNROLL = 16
SUBLANES = 8
BF16_ROWS = 16


def _scores(krot_ref, qs_ref, j, tk):
    k = krot_ref[j * tk:(j + 1) * tk, :]
    return lax.dot_general(k, qs_ref[...], (((1,), (1,)), ((), ())), preferred_element_type=F32)


def _attend_online(qs_ref, krot_ref, vt_ref, acc_ref, *, tk):
    n_tiles = krot_ref.shape[0] // tk
    m = l = None
    s_next = _scores(krot_ref, qs_ref, 0, tk)
    for j in range(n_tiles):
        s = s_next
        if j + 1 < n_tiles:
            s_next = _scores(krot_ref, qs_ref, j + 1, tk)
        m_cur = jnp.max(s, axis=0, keepdims=True)
        m_next = m_cur if m is None else jnp.maximum(m, m_cur)
        p = jnp.exp2(s - m_next)
        pv = jnp.dot(vt_ref[:, j * tk:(j + 1) * tk], p.astype(BF16), preferred_element_type=F32)
        if m is None:
            l = jnp.sum(p, axis=0, keepdims=True)
            acc_ref[...] = pv
        else:
            alpha = jnp.exp2(m - m_next)
            l = alpha * l + jnp.sum(p, axis=0, keepdims=True)
            acc_ref[...] = alpha * acc_ref[...] + pv
        m = m_next
    return l


def _attend_fixed(qs_ref, krot_ref, vt_ref, acc_ref, offset, *, tk):
    n_tiles = krot_ref.shape[0] // tk
    l = None
    s_next = _scores(krot_ref, qs_ref, 0, tk)
    for j in range(n_tiles):
        s = s_next
        if j + 1 < n_tiles:
            s_next = _scores(krot_ref, qs_ref, j + 1, tk)
        p = jnp.exp2(s - offset)
        pv = jnp.dot(vt_ref[:, j * tk:(j + 1) * tk], p.astype(BF16), preferred_element_type=F32)
        if l is None:
            l = jnp.sum(p, axis=0, keepdims=True)
            acc_ref[...] = pv
        else:
            l = l + jnp.sum(p, axis=0, keepdims=True)
            acc_ref[...] += pv
    return l


def _fixed_offset_is_safe(q_norm2_max, kstat_ref):
    worst_exponent = jnp.sqrt(q_norm2_max) * kstat_ref[1:2, 0:1]
    return (jnp.max(worst_exponent) <= ATTN_OFFSET_SLACK).astype(jnp.int32)


def _attend(qs_ref, fixed_ok_ref, krot_ref, vt_ref, kstat_ref, acc_ref, finish, *, tk):
    use_fixed = fixed_ok_ref[0] == 1

    @pl.when(use_fixed)
    def _():
        qf = qs_ref[...].astype(F32)
        ones = jnp.ones((BF16_ROWS, qf.shape[1]), BF16)
        q_norm2 = lax.dot_general(ones, (qf * qf).astype(BF16), (((1,), (1,)), ((), ())),
                                  preferred_element_type=F32)
        offset = jnp.sqrt(q_norm2[0:1]) * kstat_ref[0:1, 0:1]
        finish(_attend_fixed(qs_ref, krot_ref, vt_ref, acc_ref, offset, tk=tk))

    @pl.when(jnp.logical_not(use_fixed))
    def _():
        finish(_attend_online(qs_ref, krot_ref, vt_ref, acc_ref, tk=tk))


def _rms_gain_norm2_bound(gain):
    return gain.shape[1] * jnp.max(gain * gain, axis=1, keepdims=True)


def _rotated_rows(x, cos, sin_signed, pmat, gain_pair):
    if gain_pair is None:
        return _rope_bf16(x, pmat, cos, sin_signed)
    xf = x.astype(F32)
    inv_rms = lax.rsqrt(jnp.mean(xf * xf, axis=-1, keepdims=True) + EPS)
    return _rope_bf16(x, pmat, cos * gain_pair[0], sin_signed * gain_pair[1]) * inv_rms


def _gain_pair(gain):
    tiled = jnp.broadcast_to(gain, (SUBLANES, gain.shape[1]))
    return gain, _rope_partner(tiled)[0:1]


def _prepare_keys_values(k_ref, v_ref, cos_ref, sin_ref, krot_ref, vt_ref, kstat_ref, pmat, gain):
    seq = k_ref.shape[0]
    gain_pair = None if gain is None else _gain_pair(gain)

    def body(c, carry):
        k_sum, k_max2 = carry
        r0 = pl.multiple_of(c * ROW_CHUNK, ROW_CHUNK)
        rows = pl.ds(r0, ROW_CHUNK)
        kb = _rotated_rows(k_ref[rows, :], cos_ref[rows, :], sin_ref[rows, :], pmat,
                           gain_pair).astype(BF16)
        krot_ref[rows, :] = kb
        vt_ref[:, rows] = v_ref[rows, :].T
        kf = kb.astype(F32)
        if gain is None:
            norm2 = jnp.sum(kf * kf, axis=1, keepdims=True)
            k_max2 = jnp.maximum(k_max2, jnp.max(norm2, axis=0, keepdims=True))
        return k_sum + jnp.sum(kf, axis=0, keepdims=True), k_max2

    k_max2_init = (jnp.zeros((1, 1), F32) if gain is None
                   else _rms_gain_norm2_bound(gain))
    k_sum, k_max2 = lax.fori_loop(0, seq // ROW_CHUNK, body,
                                  (jnp.zeros((1, k_ref.shape[1]), F32), k_max2_init),
                                  unroll=PREP_UNROLL)
    k_bar = k_sum * (1.0 / seq)
    k_bar_norm = jnp.sqrt(jnp.sum(k_bar * k_bar, axis=1, keepdims=True))
    k_max = jnp.sqrt(k_max2) * 1.01
    kstat_ref[0:1, :] = jnp.broadcast_to(k_max, (1, kstat_ref.shape[1]))
    kstat_ref[1:2, :] = jnp.broadcast_to(k_max + k_bar_norm, (1, kstat_ref.shape[1]))


def _stacked_row(c, tq):
    per_block = tq // ROW_CHUNK
    return pl.multiple_of((c // per_block) * (2 * tq) + (c % per_block) * ROW_CHUNK, ROW_CHUNK)


def _diff_attn_kernel(q_ref, k_ref, v_ref, cos_ref, sin_ref, lam_ref, g_ref, o_ref,
                      krot_ref, vt_ref, kstat_ref, qs_ref, fixed_ok_ref, acc_ref, *, tq, tk, lam_init):
    @pl.when(pl.program_id(1) == 0)
    def _():
        pmat = _rope_partner_matrix()
        _prepare_keys_values(k_ref, v_ref, cos_ref, sin_ref, krot_ref, vt_ref, kstat_ref, pmat, None)

        def body(c, q_norm2_max):
            rows = pl.ds(pl.multiple_of(c * ROW_CHUNK, ROW_CHUNK), ROW_CHUNK)
            q = _rotated_rows(q_ref[rows, :], cos_ref[rows, :], sin_ref[rows, :], pmat, None) * (
                DIFF_QK_DIM ** -0.5 * LOG2E)
            lane = lax.broadcasted_iota(jnp.int32, q.shape, 1)
            dst = _stacked_row(c, tq)
            qs_ref[pl.ds(dst, ROW_CHUNK), :] = jnp.where(lane < DIFF_QK_DIM, q, 0.0).astype(BF16)
            qs_ref[pl.ds(dst + tq, ROW_CHUNK), :] = jnp.where(lane >= DIFF_QK_DIM, q, 0.0).astype(BF16)
            return jnp.maximum(q_norm2_max,
                               jnp.max(jnp.sum(q * q, axis=1, keepdims=True), axis=0, keepdims=True))

        q_norm2_max = lax.fori_loop(0, q_ref.shape[0] // ROW_CHUNK, body, jnp.zeros((1, 1), F32),
                                    unroll=PREP_UNROLL)
        fixed_ok_ref[0] = _fixed_offset_is_safe(q_norm2_max, kstat_ref)

    qs_blk = qs_ref.at[pl.ds(pl.multiple_of(pl.program_id(1) * (2 * tq), 2 * tq), 2 * tq), :]

    def finish(l):
        lp = lam_ref[...]
        lam = (jnp.exp(jnp.sum(lp[0:1] * lp[1:2], keepdims=True))
               - jnp.exp(jnp.sum(lp[2:3] * lp[3:4], keepdims=True)) + lam_init)
        ot = (acc_ref[:, 0:tq] / l[:, 0:tq] - lam * (acc_ref[:, tq:2 * tq] / l[:, tq:2 * tq]))
        ms = jnp.mean(ot * ot, axis=0, keepdims=True)
        ot = ot * lax.rsqrt(ms + EPS) * (g_ref[...] * (1.0 - lam_init))
        o_ref[...] = ot.T.astype(o_ref.dtype)

    _attend(qs_blk, fixed_ok_ref, krot_ref, vt_ref, kstat_ref, acc_ref, finish, tk=tk)


def _gqa_attn_kernel(q_ref, k_ref, v_ref, cos_ref, sin_ref, qn_ref, kn_ref, o_ref,
                     krot_ref, vt_ref, kstat_ref, qs_ref, fixed_ok_ref, acc_ref, *, tq, tk):
    @pl.when(pl.program_id(1) == 0)
    def _():
        pmat = _rope_partner_matrix()
        _prepare_keys_values(k_ref, v_ref, cos_ref, sin_ref, krot_ref, vt_ref, kstat_ref, pmat,
                             kn_ref[...])

        q_scale = HEAD_DIM ** -0.5 * LOG2E
        q_gain_pair = _gain_pair(qn_ref[...] * q_scale)

        def body(c, _):
            rows = pl.ds(pl.multiple_of(c * ROW_CHUNK, ROW_CHUNK), ROW_CHUNK)
            dst = _stacked_row(c, tq)
            cos, sin_signed = cos_ref[rows, :], sin_ref[rows, :]
            for r in range(2):
                q = _rotated_rows(q_ref[rows, r * HEAD_DIM:(r + 1) * HEAD_DIM], cos, sin_signed,
                                  pmat, q_gain_pair)
                qs_ref[pl.ds(dst + r * tq, ROW_CHUNK), :] = q.astype(BF16)
            return 0

        lax.fori_loop(0, q_ref.shape[0] // ROW_CHUNK, body, 0, unroll=PREP_UNROLL)
        q_norm2_max = _rms_gain_norm2_bound(qn_ref[...]) * (q_scale * q_scale)
        fixed_ok_ref[0] = _fixed_offset_is_safe(q_norm2_max, kstat_ref)

    qs_blk = qs_ref.at[pl.ds(pl.multiple_of(pl.program_id(1) * (2 * tq), 2 * tq), 2 * tq), :]

    def finish(l):
        for r in range(2):
            ot = acc_ref[:, r * tq:(r + 1) * tq] / l[:, r * tq:(r + 1) * tq]
            o_ref[:, r * HEAD_DIM:(r + 1) * HEAD_DIM] = ot.T.astype(o_ref.dtype)

    _attend(qs_blk, fixed_ok_ref, krot_ref, vt_ref, kstat_ref, acc_ref, finish, tk=tk)


def _flash_call(kernel, proj3, tables, params, *, n_groups, q_width, q_col, k_col, v_col,
                out_width, tq, name):
    batch, seq, _ = proj3.shape
    cos_t, sin_t = tables
    qb, kb, vb = q_col // q_width, k_col // HEAD_DIM, v_col // HEAD_DIM

    def grp(i):
        return i // n_groups, i % n_groups

    in_specs = [
        pl.BlockSpec((None, seq, q_width), lambda i, j: (grp(i)[0], 0, qb + grp(i)[1])),
        pl.BlockSpec((None, seq, HEAD_DIM), lambda i, j: (grp(i)[0], 0, kb + grp(i)[1])),
        pl.BlockSpec((None, seq, HEAD_DIM), lambda i, j: (grp(i)[0], 0, vb + grp(i)[1])),
        pl.BlockSpec((seq, LANES), lambda i, j: (0, 0)),
        pl.BlockSpec((seq, LANES), lambda i, j: (0, 0)),
    ] + [pl.BlockSpec(p.shape, lambda i, j: (0, 0)) for p in params]
    return pl.pallas_call(
        kernel,
        grid=(batch * n_groups, seq // tq),
        in_specs=in_specs,
        out_specs=pl.BlockSpec((None, tq, out_width), lambda i, j: (grp(i)[0], j, grp(i)[1])),
        out_shape=jax.ShapeDtypeStruct((batch, seq, GROUP_WIDTH), BF16),
        scratch_shapes=[pltpu.VMEM((seq, HEAD_DIM), BF16),
                        pltpu.VMEM((HEAD_DIM, seq), BF16),
                        pltpu.VMEM((SUBLANES, LANES), F32),
                        pltpu.VMEM((2 * seq, HEAD_DIM), BF16),
                        pltpu.SMEM((1,), jnp.int32),
                        pltpu.VMEM((HEAD_DIM, 2 * tq), F32)],
        compiler_params=_cparams(("parallel", "arbitrary")),
        name=name,
    )(proj3, proj3, proj3, cos_t, sin_t, *params)


def _conv_kernel(a_ref, g_ref, dw_ref, dwb_ref, lng_ref, lnb_ref, pw_ref, pwb_ref, wo_ref,
                 o_ref, wob_ref, u_ref, acc_ref, y_ref, *, tm):
    seq = a_ref.shape[0]
    t = pl.program_id(1)

    wob_ref[...] = wo_ref[...].astype(wob_ref.dtype)

    @pl.when(t == 0)
    def _():
        zeros = jnp.zeros((CONV_HALO, u_ref.shape[1]), F32)
        u_ref[0:CONV_HALO, :] = zeros
        u_ref[CONV_HALO + seq:2 * CONV_HALO + seq, :] = zeros

        def body(c, _):
            r0 = pl.multiple_of(c * ROW_CHUNK, ROW_CHUNK)
            a = a_ref[pl.ds(r0, ROW_CHUNK), :].astype(F32)
            g = g_ref[pl.ds(r0, ROW_CHUNK), :].astype(F32)
            u_ref[pl.ds(CONV_HALO + r0, ROW_CHUNK), :] = a * jax.nn.sigmoid(g)
            return 0

        lax.fori_loop(0, seq // ROW_CHUNK, body, 0)

    pad = CONV_WIDTH // 2
    sub = 8
    n_ch = u_ref.shape[1]

    def row_block(rb, _):
        base = pl.multiple_of(t * tm + rb * ROW_CHUNK, ROW_CHUNK)
        for cg in range(n_ch // LANES):
            cols = slice(cg * LANES, (cg + 1) * LANES)
            win = u_ref[pl.ds(base, ROW_CHUNK + 2 * CONV_HALO), cols]
            acc = jnp.zeros((ROW_CHUNK, LANES), F32) + dwb_ref[:, cols]
            for r in range(sub):
                shifted = win if r == 0 else pltpu.roll(win, win.shape[0] - r, 0)
                for a in range(2 * CONV_HALO // sub):
                    k = sub * a + r - (CONV_HALO - pad)
                    if 0 <= k < CONV_WIDTH:
                        acc = acc + shifted[sub * a:sub * a + ROW_CHUNK, :] * dw_ref[k:k + 1, cols]
            acc_ref[:, cols] = acc
        acc = acc_ref[...]
        mu = jnp.mean(acc, axis=-1, keepdims=True)
        cen = acc - mu
        var = jnp.mean(cen * cen, axis=-1, keepdims=True)
        y = cen * lax.rsqrt(var + EPS) * lng_ref[...] + lnb_ref[...]
        y_ref[pl.ds(pl.multiple_of(rb * ROW_CHUNK, ROW_CHUNK), ROW_CHUNK), :] = (
            y * jax.nn.sigmoid(y)).astype(BF16)
        return 0

    lax.fori_loop(0, tm // ROW_CHUNK, row_block, 0)
    o_ref[...] = (jnp.dot(y_ref[...], pw_ref[...].astype(BF16), preferred_element_type=F32)
                  + pwb_ref[...]).astype(o_ref.dtype)


def _conv_mixer(proj3, dw, dwb, lng, lnb, pw, pwb, w_out, layer, *, tm):
    batch, seq, _ = proj3.shape
    ch = GROUP_WIDTH
    n_t = seq // tm
    slab = w_out.shape[1] // (batch * n_t)
    assert slab * batch * n_t == w_out.shape[1] and slab % BF16_ROWS == 0
    vec = pl.BlockSpec((1, ch), lambda b, t: (0, 0))
    return pl.pallas_call(
        functools.partial(_conv_kernel, tm=tm),
        grid=(batch, n_t),
        in_specs=[pl.BlockSpec((None, seq, ch), lambda b, t: (b, 0, COL_B_A // ch)),
                  pl.BlockSpec((None, seq, ch), lambda b, t: (b, 0, COL_B_G // ch)),
                  pl.BlockSpec((CONV_WIDTH, ch), lambda b, t: (0, 0)),
                  vec, vec, vec,
                  pl.BlockSpec((None, ch, ch), lambda b, t: (layer, 0, 0)),
                  vec,
                  pl.BlockSpec((None, slab, w_out.shape[2]), lambda b, t: (layer, b * n_t + t, 0))],
        out_specs=[pl.BlockSpec((None, tm, ch), lambda b, t: (b, t, 0)),
                   pl.BlockSpec((slab, w_out.shape[2]), lambda b, t: (b * n_t + t, 0))],
        out_shape=[jax.ShapeDtypeStruct((batch, seq, ch), BF16),
                   jax.ShapeDtypeStruct(w_out.shape[1:], BF16)],
        scratch_shapes=[pltpu.VMEM((seq + 2 * CONV_HALO, ch), F32),
                        pltpu.VMEM((ROW_CHUNK, ch), F32),
                        pltpu.VMEM((tm, ch), BF16)],
        compiler_params=_cparams(("parallel", "arbitrary")),
        name="conv_mixer",
    )(proj3, proj3, dw, dwb, lng, lnb, pw, pwb, w_out)


def _na_geometry(seq):
    rows = seq // GRID_W
    qr_blk = Q_BLOCK // GRID_W
    mask_add = np.zeros((NA_CLASSES, Q_BLOCK, NA_BAND), np.float32)
    q_local = np.arange(Q_BLOCK)
    p = np.arange(NA_BAND)
    for c, blk in enumerate(_na_class_blocks(seq)):
        bs = int(np.clip(blk * qr_blk - NA_KH // 2, 0, rows - NA_BAND_ROWS))
        q_row = (blk * qr_blk + q_local // GRID_W)[:, None]
        q_col = (q_local % GRID_W)[:, None]
        k_row = (bs + p // GRID_W)[None, :]
        k_col = (p % GRID_W)[None, :]
        win_r = np.clip(q_row - NA_KH // 2, 0, rows - NA_KH)
        win_c = np.clip(q_col - NA_KW // 2, 0, GRID_W - NA_KW)
        ok = (k_row >= win_r) & (k_row < win_r + NA_KH) & (k_col >= win_c) & (k_col < win_c + NA_KW)
        mask_add[c] = np.where(ok, 0.0, -1e30)
    return jnp.asarray(mask_add)


def _na_class_blocks(seq):
    n_blk = seq // Q_BLOCK
    return (0, 1, 2, n_blk - 2, n_blk - 1)


def _rpb_tiles_kernel(rpb_ref, mask_ref, o_ref, t2_ref, *, seq):
    h = pl.program_id(0)
    n_c = 2 * NA_KW - 1
    n_r = 2 * NA_KH - 1
    shape = (GRID_W, LANES)
    lane = lax.broadcasted_iota(jnp.int32, shape, 1)
    q_col = lax.broadcasted_iota(jnp.int32, shape, 0)
    ic = jnp.clip((lane & (GRID_W - 1)) - q_col + (NA_KW - 1), 0, n_c - 1)
    left = lane < GRID_W
    base = h * (n_r * n_c)
    for i in range(NA_T2):
        i_l = min(max(i - 1, 0), n_r - 1)
        i_r = min(max(i, 0), n_r - 1)

        acc = jnp.zeros(shape, F32)
        for j in range(n_c):
            coef = jnp.where(left, rpb_ref[base + i_l * n_c + j], rpb_ref[base + i_r * n_c + j])
            acc = acc + jnp.where(ic == j, coef, 0.0)
        t2_ref[i] = acc

    rows = seq // GRID_W
    qr_blk = Q_BLOCK // GRID_W
    for c, blk in enumerate(_na_class_blocks(seq)):
        bs = min(max(blk * qr_blk - NA_KH // 2, 0), rows - NA_BAND_ROWS)
        for a in range(qr_blk):
            for r2 in range(NA_BAND_ROWS // 2):
                i1 = bs + 2 * r2 - (blk * qr_blk + a) + (NA_KH - 1)
                idx = min(max(i1, -1), NA_T2 - 2) + 1
                rs = slice(a * GRID_W, (a + 1) * GRID_W)
                cs = slice(r2 * LANES, (r2 + 1) * LANES)
                o_ref[c, rs, cs] = (t2_ref[idx] + mask_ref[c, rs, cs]) * LOG2E


def _rpb_tiles(rpb, mask_add, seq):
    heads = rpb.shape[0]
    return pl.pallas_call(
        functools.partial(_rpb_tiles_kernel, seq=seq),
        grid=(heads,),
        in_specs=[pl.BlockSpec(memory_space=pltpu.SMEM),
                  pl.BlockSpec(mask_add.shape, lambda h: (0, 0, 0))],
        out_specs=pl.BlockSpec((None,) + mask_add.shape, lambda h: (h, 0, 0, 0)),
        out_shape=jax.ShapeDtypeStruct((heads,) + mask_add.shape, F32),
        scratch_shapes=[pltpu.VMEM((NA_T2, GRID_W, LANES), F32)],
        compiler_params=_cparams(("parallel",)),
        name="na_rpb_tiles",
    )(rpb.reshape(-1), mask_add)


def _na_kernel(q_ref, k_ref, v_ref, bias_ref, o_ref):
    seq = q_ref.shape[0]
    rows = seq // GRID_W
    qr_blk = Q_BLOCK // GRID_W
    n_blk = seq // Q_BLOCK

    classes = _na_class_blocks(seq)

    def band(blk):
        k0 = min(max(blk * qr_blk - NA_KH // 2, 0), rows - NA_BAND_ROWS) * GRID_W
        return slice(k0, k0 + NA_BAND)

    def scores(blk):
        q = q_ref[blk * Q_BLOCK:(blk + 1) * Q_BLOCK, :]
        return lax.dot_general(q, k_ref[band(blk), :], (((1,), (1,)), ((), ())),
                               preferred_element_type=F32)

    pending = [scores(b) for b in range(NA_LOOKAHEAD)]
    for blk in range(n_blk):
        s = pending.pop(0)
        if blk + NA_LOOKAHEAD < n_blk:
            pending.append(scores(blk + NA_LOOKAHEAD))
        cls = classes.index(blk) if blk in classes else 2
        s = s * (HEAD_DIM ** -0.5 * LOG2E) + bias_ref[cls]
        m = jnp.max(s, axis=1, keepdims=True)
        p = jnp.exp2(s - m)
        l = jnp.sum(p, axis=1, keepdims=True)
        o = jnp.dot(p.astype(BF16), v_ref[band(blk), :], preferred_element_type=F32) / l
        o_ref[blk * Q_BLOCK:(blk + 1) * Q_BLOCK, :] = o.astype(o_ref.dtype)


def _na_mixer(proj3, bias):
    batch, seq, _ = proj3.shape
    heads = GROUP_HEADS

    def head_spec(col):
        return pl.BlockSpec((None, seq, HEAD_DIM), lambda b, h: (b, 0, col // HEAD_DIM + h))

    return pl.pallas_call(
        _na_kernel,
        grid=(batch, heads),
        in_specs=[head_spec(COL_D_Q), head_spec(COL_D_K), head_spec(COL_D_V),
                  pl.BlockSpec((None,) + bias.shape[1:], lambda b, h: (h, 0, 0, 0))],
        out_specs=pl.BlockSpec((None, seq, HEAD_DIM), lambda b, h: (b, 0, h)),
        out_shape=jax.ShapeDtypeStruct((batch, seq, GROUP_WIDTH), BF16),
        compiler_params=_cparams(("parallel", "parallel")),
        name="na_mixer",
    )(proj3, proj3, proj3, bias)


def kernel(x, norm_mix_pre, norm_mix_post, norm_ffn_pre, norm_ffn_post, w_in, w_out, diff_lambda, diff_subln, conv_dw, conv_dw_b, conv_ln_g, conv_ln_b, conv_pw, conv_pw_b, gqa_q_norm, gqa_k_norm, na_rpb, ffn_gate, ffn_up, ffn_down):
    batch, seq, d = x.shape
    depth = w_in.shape[0]
    tokens = batch * seq

    t = np.arange(seq)
    diff_tables = _rope_tables(t, t)
    axial_tables = _rope_tables(t // GRID_W, t % GRID_W)
    mask_add = _na_geometry(seq)

    def row(v):
        return v.reshape(1, -1)

    xf = x.reshape(tokens, d)
    h = _rmsnorm(xf, row(norm_mix_pre[0]), tm=1024)
    for l in range(depth):
        lam_init = 0.8 - 0.6 * math.exp(-0.3 * l)
        proj = _in_proj(h, w_in, l, tm=2048, tn=1024)
        proj3 = proj.reshape(batch, seq, IN_COLS)

        out_a = _flash_call(
            functools.partial(_diff_attn_kernel, tq=ATTN_TQ, tk=ATTN_TK, lam_init=lam_init),
            proj3, diff_tables, (diff_lambda[l], diff_subln[l].reshape(-1, 1)),
            n_groups=GROUP_HEADS, q_width=HEAD_DIM, q_col=COL_A_Q, k_col=COL_A_K, v_col=COL_A_V,
            out_width=HEAD_DIM, tq=ATTN_TQ, name="diff_attn")
        out_b, w_out_bf16 = _conv_mixer(
            proj3, conv_dw[l], row(conv_dw_b[l]), row(conv_ln_g[l]), row(conv_ln_b[l]),
            conv_pw, row(conv_pw_b[l]), w_out, l, tm=512)
        out_c = _flash_call(
            functools.partial(_gqa_attn_kernel, tq=ATTN_TQ, tk=ATTN_TK),
            proj3, axial_tables, (row(gqa_q_norm[l]), row(gqa_k_norm[l])),
            n_groups=GQA_KV_HEADS, q_width=2 * HEAD_DIM, q_col=COL_C_Q, k_col=COL_C_K, v_col=COL_C_V,
            out_width=2 * HEAD_DIM, tq=ATTN_TQ, name="gqa_attn")
        out_d = _na_mixer(proj3, _rpb_tiles(na_rpb[l], mask_add, seq))

        parts = [o.reshape(tokens, GROUP_WIDTH) for o in (out_a, out_b, out_c, out_d)]
        xf, h = _out_proj(parts, w_out_bf16, row(norm_mix_post[l]), xf,
                          row(norm_ffn_pre[l]), tm=512)

        act, w_down = _ffn_up(h, ffn_gate, ffn_up, ffn_down, l, tm=2048, tn=512)
        next_gain = row(norm_mix_pre[l + 1]) if l + 1 < depth else None
        xf, h = _ffn_down(act, w_down, row(norm_ffn_post[l]), xf, next_gain,
                          tm=256)
    return xf.reshape(batch, seq, d)
```

```python
import functools
import math

import numpy as np
import jax
import jax.numpy as jnp
from jax import lax
from jax.experimental import pallas as pl
from jax.experimental.pallas import tpu as pltpu

D_MODEL = 2048
HEAD_DIM = 128
GROUP_HEADS = 4
GROUP_WIDTH = GROUP_HEADS * HEAD_DIM
DIFF_QK_DIM = HEAD_DIM // 2
CONV_WIDTH = 31
GQA_KV_HEADS = 2
NA_KH = 8
NA_KW = 16
GRID_W = 64
Q_BLOCK = 128
ROPE_THETA = 10000.0
EPS = 1e-6

COL_A_Q, COL_A_K, COL_A_V = 0, 512, 1024
COL_B_A, COL_B_G = 1536, 2048
COL_C_Q, COL_C_K, COL_C_V = 2560, 3072, 3328
COL_D_Q, COL_D_K, COL_D_V = 3584, 4096, 4608
IN_COLS = 5120

LANES = 128
VMEM_LIMIT = 56 * 1024 * 1024
VMEM_LIMIT_FFN_DOWN = 60 * 1024 * 1024
ROW_CHUNK = 128

RMSNORM_TM = 1024
IN_PROJ_TM, IN_PROJ_TN = 2048, 1024
CONV_TM = 512
OUT_PROJ_TM = 512
FFN_UP_TM, FFN_UP_TN = 2048, 512
FFN_DOWN_TM = 512
FFN_UP_SUBTILE = 1024
ATTN_TQ = 512
ATTN_TK = 512

NA_BAND_ROWS = 10
NA_BAND = NA_BAND_ROWS * GRID_W
NA_CLASSES = 5
NA_T2 = 16
NA_LOOKAHEAD = 2
CONV_HALO = 16

F32 = jnp.float32
BF16 = jnp.bfloat16


def _cparams(semantics, vmem_limit=VMEM_LIMIT):
    return pltpu.CompilerParams(dimension_semantics=semantics,
                                vmem_limit_bytes=vmem_limit)


def _rms(x):
    return x * lax.rsqrt(jnp.mean(x * x, axis=-1, keepdims=True) + EPS)


def _rmsnorm_kernel(x_ref, g_ref, h_ref):
    def body(c, _):
        r0 = pl.multiple_of(c * ROW_CHUNK, ROW_CHUNK)
        h_ref[pl.ds(r0, ROW_CHUNK), :] = (_rms(x_ref[pl.ds(r0, ROW_CHUNK), :]) * g_ref[...]).astype(h_ref.dtype)
        return 0

    lax.fori_loop(0, x_ref.shape[0] // ROW_CHUNK, body, 0)


def _rmsnorm(x, g, *, tm):
    m, d = x.shape
    return pl.pallas_call(
        _rmsnorm_kernel,
        grid=(m // tm,),
        in_specs=[pl.BlockSpec((tm, d), lambda i: (i, 0)),
                  pl.BlockSpec((1, d), lambda i: (0, 0))],
        out_specs=pl.BlockSpec((tm, d), lambda i: (i, 0)),
        out_shape=jax.ShapeDtypeStruct((m, d), BF16),
        compiler_params=_cparams(("parallel",)),
        name="rmsnorm",
    )(x, g)


def _cast_weight_tile(w_ref, wb_ref):
    def body(c, _):
        r0 = pl.multiple_of(c * ROW_CHUNK, ROW_CHUNK)
        wb_ref[pl.ds(r0, ROW_CHUNK), :] = w_ref[pl.ds(r0, ROW_CHUNK), :].astype(wb_ref.dtype)
        return 0

    lax.fori_loop(0, w_ref.shape[0] // ROW_CHUNK, body, 0)


def _in_proj_kernel(h_ref, w_ref, o_ref, wb_ref):
    @pl.when(pl.program_id(1) == 0)
    def _():
        _cast_weight_tile(w_ref, wb_ref)

    o_ref[...] = jnp.dot(h_ref[...], wb_ref[...], preferred_element_type=F32).astype(o_ref.dtype)


def _in_proj(h, w, layer, *, tm, tn):
    m, d = h.shape
    n = w.shape[2]
    return pl.pallas_call(
        _in_proj_kernel,
        grid=(n // tn, m // tm),
        in_specs=[pl.BlockSpec((tm, d), lambda j, i: (i, 0)),
                  pl.BlockSpec((None, d, tn), lambda j, i: (layer, 0, j))],
        out_specs=pl.BlockSpec((tm, tn), lambda j, i: (i, j)),
        out_shape=jax.ShapeDtypeStruct((m, n), BF16),
        scratch_shapes=[pltpu.VMEM((d, tn), BF16)],
        compiler_params=_cparams(("parallel", "arbitrary")),
        name="in_proj",
    )(h, w)


def _ffn_up_kernel(h_ref, wg_ref, wu_ref, wd_ref, o_ref, wdb_ref, wgb_ref, wub_ref):
    @pl.when(pl.program_id(1) == 0)
    def _():
        _cast_weight_tile(wg_ref, wgb_ref)
        _cast_weight_tile(wu_ref, wub_ref)

    wdb_ref[...] = wd_ref[...].astype(wdb_ref.dtype)

    for r in range(h_ref.shape[0] // FFN_UP_SUBTILE):
        rows = slice(r * FFN_UP_SUBTILE, (r + 1) * FFN_UP_SUBTILE)
        h = h_ref[rows, :]
        gate = jnp.dot(h, wgb_ref[...], preferred_element_type=F32)
        up = jnp.dot(h, wub_ref[...], preferred_element_type=F32)
        o_ref[rows, :] = (gate * jax.nn.sigmoid(gate) * up).astype(o_ref.dtype)


def _ffn_up(h, wg, wu, wd, layer, *, tm, tn):
    m, d = h.shape
    n = wg.shape[2]
    n_i = m // tm
    slab = wd.shape[1] // ((n // tn) * n_i)
    assert slab * (n // tn) * n_i == wd.shape[1] and slab % BF16_ROWS == 0
    w_spec = pl.BlockSpec((None, d, tn), lambda j, i: (layer, 0, j))
    return pl.pallas_call(
        _ffn_up_kernel,
        grid=(n // tn, n_i),
        in_specs=[pl.BlockSpec((tm, d), lambda j, i: (i, 0)), w_spec, w_spec,
                  pl.BlockSpec((None, slab, wd.shape[2]), lambda j, i: (layer, j * n_i + i, 0))],
        out_specs=[pl.BlockSpec((tm, tn), lambda j, i: (i, j)),
                   pl.BlockSpec((slab, wd.shape[2]), lambda j, i: (j * n_i + i, 0))],
        out_shape=[jax.ShapeDtypeStruct((m, n), BF16),
                   jax.ShapeDtypeStruct(wd.shape[1:], BF16)],
        scratch_shapes=[pltpu.VMEM((d, tn), BF16), pltpu.VMEM((d, tn), BF16)],
        compiler_params=_cparams(("parallel", "arbitrary")),
        name="ffn_up",
    )(h, wg, wu, wd)


def _project_norm_residual(lhs_ref, w_ref, g_ref, x_ref, gn_ref, o_ref, hn_ref, acc_ref):
    n_part, part, _ = acc_ref.shape
    for b in range(n_part):
        acc_ref[b] = jnp.dot(lhs_ref[b * part:(b + 1) * part, :], w_ref[...],
                             preferred_element_type=F32)
    for b in range(n_part):
        for c in range(part // ROW_CHUNK):
            rows = slice(b * part + c * ROW_CHUNK, b * part + (c + 1) * ROW_CHUNK)
            o = x_ref[rows, :] + _rms(acc_ref[b, c * ROW_CHUNK:(c + 1) * ROW_CHUNK, :]) * g_ref[...]
            o_ref[rows, :] = o
            if hn_ref is not None:
                hn_ref[rows, :] = (_rms(o) * gn_ref[...]).astype(hn_ref.dtype)


def _ffn_down_kernel(*refs, emit_next):
    if emit_next:
        a_ref, w_ref, g_ref, x_ref, gn_ref, o_ref, hn_ref, acc_ref = refs
    else:
        a_ref, w_ref, g_ref, x_ref, o_ref, acc_ref = refs
        gn_ref = hn_ref = None
    _project_norm_residual(a_ref, w_ref, g_ref, x_ref, gn_ref, o_ref, hn_ref, acc_ref)


def _ffn_down(a, w, g, x, gn, *, tm):
    m, kdim = a.shape
    d = w.shape[1]
    emit_next = gn is not None
    vec = pl.BlockSpec((1, d), lambda i: (0, 0))
    row_tile = pl.BlockSpec((tm, d), lambda i: (i, 0))
    outs = pl.pallas_call(
        functools.partial(_ffn_down_kernel, emit_next=emit_next),
        grid=(m // tm,),
        in_specs=[pl.BlockSpec((tm, kdim), lambda i: (i, 0)),
                  pl.BlockSpec((kdim, d), lambda i: (0, 0), pipeline_mode=pl.Buffered(1)),
                  vec, row_tile] + [vec] * emit_next,
        out_specs=[row_tile] + [row_tile] * emit_next,
        out_shape=[jax.ShapeDtypeStruct((m, d), F32)] + [jax.ShapeDtypeStruct((m, d), BF16)] * emit_next,
        scratch_shapes=[pltpu.VMEM((tm // ROW_CHUNK, ROW_CHUNK, d), F32)],
        compiler_params=_cparams(("parallel",), VMEM_LIMIT_FFN_DOWN),
        name="ffn_down",
    )(a, w, g, x, *([gn] * emit_next))
    return (outs[0], outs[1]) if emit_next else (outs[0], None)


def _out_proj_kernel(a_ref, b_ref, c_ref, d_ref, w_ref, g_ref, x_ref, gn_ref, o_ref, hn_ref,
                     lhs_ref, acc_ref):
    for p, part in enumerate((a_ref, b_ref, c_ref, d_ref)):
        lhs_ref[:, p * GROUP_WIDTH:(p + 1) * GROUP_WIDTH] = part[...]
    _project_norm_residual(lhs_ref, w_ref, g_ref, x_ref, gn_ref, o_ref, hn_ref, acc_ref)


def _out_proj(parts, w, g, x, gn, *, tm):
    m, d = x.shape
    part_spec = pl.BlockSpec((tm, GROUP_WIDTH), lambda i: (i, 0))
    vec = pl.BlockSpec((1, d), lambda i: (0, 0))
    row_tile = pl.BlockSpec((tm, d), lambda i: (i, 0))
    return pl.pallas_call(
        _out_proj_kernel,
        grid=(m // tm,),
        in_specs=[part_spec, part_spec, part_spec, part_spec,
                  pl.BlockSpec((4 * GROUP_WIDTH, d), lambda i: (0, 0)),
                  vec, row_tile, vec],
        out_specs=[row_tile, row_tile],
        out_shape=[jax.ShapeDtypeStruct((m, d), F32), jax.ShapeDtypeStruct((m, d), BF16)],
        scratch_shapes=[pltpu.VMEM((tm, 4 * GROUP_WIDTH), BF16),
                        pltpu.VMEM((tm // ROW_CHUNK, ROW_CHUNK, d), F32)],
        compiler_params=_cparams(("parallel",)),
        name="out_proj",
    )(*parts, w, g, x, gn)


def _rope_tables(pos_lo, pos_hi):
    half = DIFF_QK_DIM // 2
    inv = ROPE_THETA ** (-np.arange(half, dtype=np.float64) / half)
    lane = np.arange(LANES)
    pos = np.where(lane[None, :] < DIFF_QK_DIM, pos_lo[:, None], pos_hi[:, None]).astype(np.float64)
    ang = pos * inv[lane % half][None, :]
    sign = np.where((lane % DIFF_QK_DIM) < half, -1.0, 1.0)
    return (jnp.asarray(np.cos(ang), dtype=F32),
            jnp.asarray(np.sin(ang) * sign[None, :], dtype=F32))


def _rope_partner(x):
    lane = lax.broadcasted_iota(jnp.int32, x.shape, 1)
    first_half = (lane & (DIFF_QK_DIM // 2)) == 0
    return jnp.where(first_half, pltpu.roll(x, LANES - DIFF_QK_DIM // 2, 1),
                     pltpu.roll(x, DIFF_QK_DIM // 2, 1))


def _rope_partner_matrix():
    src = lax.broadcasted_iota(jnp.int32, (LANES, LANES), 0)
    dst = lax.broadcasted_iota(jnp.int32, (LANES, LANES), 1)
    half = DIFF_QK_DIM // 2
    partner = jnp.where((dst & half) == 0, dst + half, dst - half)
    return jnp.where(src == partner, 1.0, 0.0).astype(BF16)


def _rope_bf16(x, pmat, cos, sin_signed):
    partner = jnp.dot(x, pmat, preferred_element_type=F32)
    return x.astype(F32) * cos + partner * sin_signed


LOG2E = math.log2(math.e)


ATTN_OFFSET_SLACK = 64.0
PREP_UNROLL = 16
SUBLANES = 8
BF16_ROWS = 16


def _scores(krot_ref, qs_ref, j, tk):
    k = krot_ref[j * tk:(j + 1) * tk, :]
    return lax.dot_general(k, qs_ref[...], (((1,), (1,)), ((), ())), preferred_element_type=F32)


def _attend_online(qs_ref, krot_ref, vt_ref, acc_ref, *, tk):
    n_tiles = krot_ref.shape[0] // tk
    m = l = None
    s_next = _scores(krot_ref, qs_ref, 0, tk)
    for j in range(n_tiles):
        s = s_next
        if j + 1 < n_tiles:
            s_next = _scores(krot_ref, qs_ref, j + 1, tk)
        m_cur = jnp.max(s, axis=0, keepdims=True)
        m_next = m_cur if m is None else jnp.maximum(m, m_cur)
        p = jnp.exp2(s - m_next)
        pv = jnp.dot(vt_ref[:, j * tk:(j + 1) * tk], p.astype(BF16), preferred_element_type=F32)
        if m is None:
            l = jnp.sum(p, axis=0, keepdims=True)
            acc_ref[...] = pv
        else:
            alpha = jnp.exp2(m - m_next)
            l = alpha * l + jnp.sum(p, axis=0, keepdims=True)
            acc_ref[...] = alpha * acc_ref[...] + pv
        m = m_next
    return l


def _attend_fixed(qs_ref, krot_ref, vt_ref, acc_ref, offset, *, tk):
    n_tiles = krot_ref.shape[0] // tk
    l = None
    s_next = _scores(krot_ref, qs_ref, 0, tk)
    for j in range(n_tiles):
        s = s_next
        if j + 1 < n_tiles:
            s_next = _scores(krot_ref, qs_ref, j + 1, tk)
        p = jnp.exp2(s - offset)
        pv = jnp.dot(vt_ref[:, j * tk:(j + 1) * tk], p.astype(BF16), preferred_element_type=F32)
        if l is None:
            l = jnp.sum(p, axis=0, keepdims=True)
            acc_ref[...] = pv
        else:
            l = l + jnp.sum(p, axis=0, keepdims=True)
            acc_ref[...] += pv
    return l


def _fixed_offset_is_safe(q_norm2_max, kstat_ref):
    worst_exponent = jnp.sqrt(q_norm2_max) * kstat_ref[1:2, 0:1]
    return (jnp.max(worst_exponent) <= ATTN_OFFSET_SLACK).astype(jnp.int32)


def _attend(qs_ref, fixed_ok_ref, krot_ref, vt_ref, kstat_ref, acc_ref, finish, *, tk):
    use_fixed = fixed_ok_ref[0] == 1

    @pl.when(use_fixed)
    def _():
        qf = qs_ref[...].astype(F32)
        ones = jnp.ones((BF16_ROWS, qf.shape[1]), BF16)
        q_norm2 = lax.dot_general(ones, (qf * qf).astype(BF16), (((1,), (1,)), ((), ())),
                                  preferred_element_type=F32)
        offset = jnp.sqrt(q_norm2[0:1]) * kstat_ref[0:1, 0:1]
        finish(_attend_fixed(qs_ref, krot_ref, vt_ref, acc_ref, offset, tk=tk))

    @pl.when(jnp.logical_not(use_fixed))
    def _():
        finish(_attend_online(qs_ref, krot_ref, vt_ref, acc_ref, tk=tk))


def _rms_gain_norm2_bound(gain):
    return gain.shape[1] * jnp.max(gain * gain, axis=1, keepdims=True)


def _rotated_rows(x, cos, sin_signed, pmat, gain_pair):
    if gain_pair is None:
        return _rope_bf16(x, pmat, cos, sin_signed)
    xf = x.astype(F32)
    inv_rms = lax.rsqrt(jnp.mean(xf * xf, axis=-1, keepdims=True) + EPS)
    return _rope_bf16(x, pmat, cos * gain_pair[0], sin_signed * gain_pair[1]) * inv_rms


def _gain_pair(gain):
    tiled = jnp.broadcast_to(gain, (SUBLANES, gain.shape[1]))
    return gain, _rope_partner(tiled)[0:1]


def _prepare_keys_values(k_ref, v_ref, cos_ref, sin_ref, krot_ref, vt_ref, kstat_ref, pmat, gain):
    seq = k_ref.shape[0]
    gain_pair = None if gain is None else _gain_pair(gain)

    def body(c, carry):
        k_sum, k_max2 = carry
        r0 = pl.multiple_of(c * ROW_CHUNK, ROW_CHUNK)
        rows = pl.ds(r0, ROW_CHUNK)
        kb = _rotated_rows(k_ref[rows, :], cos_ref[rows, :], sin_ref[rows, :], pmat,
                           gain_pair).astype(BF16)
        krot_ref[rows, :] = kb
        vt_ref[:, rows] = v_ref[rows, :].T
        kf = kb.astype(F32)
        if gain is None:
            norm2 = jnp.sum(kf * kf, axis=1, keepdims=True)
            k_max2 = jnp.maximum(k_max2, jnp.max(norm2, axis=0, keepdims=True))
        return k_sum + jnp.sum(kf, axis=0, keepdims=True), k_max2

    k_max2_init = (jnp.zeros((1, 1), F32) if gain is None
                   else _rms_gain_norm2_bound(gain))
    k_sum, k_max2 = lax.fori_loop(0, seq // ROW_CHUNK, body,
                                  (jnp.zeros((1, k_ref.shape[1]), F32), k_max2_init),
                                  unroll=PREP_UNROLL)
    k_bar = k_sum * (1.0 / seq)
    k_bar_norm = jnp.sqrt(jnp.sum(k_bar * k_bar, axis=1, keepdims=True))
    k_max = jnp.sqrt(k_max2) * 1.01
    kstat_ref[0:1, :] = jnp.broadcast_to(k_max, (1, kstat_ref.shape[1]))
    kstat_ref[1:2, :] = jnp.broadcast_to(k_max + k_bar_norm, (1, kstat_ref.shape[1]))


def _stacked_row(c, tq):
    per_block = tq // ROW_CHUNK
    return pl.multiple_of((c // per_block) * (2 * tq) + (c % per_block) * ROW_CHUNK, ROW_CHUNK)


def _diff_attn_kernel(q_ref, k_ref, v_ref, cos_ref, sin_ref, lam_ref, g_ref, o_ref,
                      krot_ref, vt_ref, kstat_ref, qs_ref, fixed_ok_ref, acc_ref, *, tq, tk, lam_init):
    @pl.when(pl.program_id(1) == 0)
    def _():
        pmat = _rope_partner_matrix()
        _prepare_keys_values(k_ref, v_ref, cos_ref, sin_ref, krot_ref, vt_ref, kstat_ref, pmat, None)

        def body(c, q_norm2_max):
            rows = pl.ds(pl.multiple_of(c * ROW_CHUNK, ROW_CHUNK), ROW_CHUNK)
            q = _rotated_rows(q_ref[rows, :], cos_ref[rows, :], sin_ref[rows, :], pmat, None) * (
                DIFF_QK_DIM ** -0.5 * LOG2E)
            lane = lax.broadcasted_iota(jnp.int32, q.shape, 1)
            dst = _stacked_row(c, tq)
            qs_ref[pl.ds(dst, ROW_CHUNK), :] = jnp.where(lane < DIFF_QK_DIM, q, 0.0).astype(BF16)
            qs_ref[pl.ds(dst + tq, ROW_CHUNK), :] = jnp.where(lane >= DIFF_QK_DIM, q, 0.0).astype(BF16)
            return jnp.maximum(q_norm2_max,
                               jnp.max(jnp.sum(q * q, axis=1, keepdims=True), axis=0, keepdims=True))

        q_norm2_max = lax.fori_loop(0, q_ref.shape[0] // ROW_CHUNK, body, jnp.zeros((1, 1), F32),
                                    unroll=PREP_UNROLL)
        fixed_ok_ref[0] = _fixed_offset_is_safe(q_norm2_max, kstat_ref)

    qs_blk = qs_ref.at[pl.ds(pl.multiple_of(pl.program_id(1) * (2 * tq), 2 * tq), 2 * tq), :]

    def finish(l):
        lp = lam_ref[...]
        lam = (jnp.exp(jnp.sum(lp[0:1] * lp[1:2], keepdims=True))
               - jnp.exp(jnp.sum(lp[2:3] * lp[3:4], keepdims=True)) + lam_init)
        ot = (acc_ref[:, 0:tq] / l[:, 0:tq] - lam * (acc_ref[:, tq:2 * tq] / l[:, tq:2 * tq]))
        ms = jnp.mean(ot * ot, axis=0, keepdims=True)
        ot = ot * lax.rsqrt(ms + EPS) * (g_ref[...] * (1.0 - lam_init))
        o_ref[...] = ot.T.astype(o_ref.dtype)

    _attend(qs_blk, fixed_ok_ref, krot_ref, vt_ref, kstat_ref, acc_ref, finish, tk=tk)


def _gqa_attn_kernel(q_ref, k_ref, v_ref, cos_ref, sin_ref, qn_ref, kn_ref, o_ref,
                     krot_ref, vt_ref, kstat_ref, qs_ref, fixed_ok_ref, acc_ref, *, tq, tk):
    @pl.when(pl.program_id(1) == 0)
    def _():
        pmat = _rope_partner_matrix()
        _prepare_keys_values(k_ref, v_ref, cos_ref, sin_ref, krot_ref, vt_ref, kstat_ref, pmat,
                             kn_ref[...])

        q_scale = HEAD_DIM ** -0.5 * LOG2E
        q_gain_pair = _gain_pair(qn_ref[...] * q_scale)

        def body(c, _):
            rows = pl.ds(pl.multiple_of(c * ROW_CHUNK, ROW_CHUNK), ROW_CHUNK)
            dst = _stacked_row(c, tq)
            cos, sin_signed = cos_ref[rows, :], sin_ref[rows, :]
            for r in range(2):
                q = _rotated_rows(q_ref[rows, r * HEAD_DIM:(r + 1) * HEAD_DIM], cos, sin_signed,
                                  pmat, q_gain_pair)
                qs_ref[pl.ds(dst + r * tq, ROW_CHUNK), :] = q.astype(BF16)
            return 0

        lax.fori_loop(0, q_ref.shape[0] // ROW_CHUNK, body, 0, unroll=PREP_UNROLL)
        q_norm2_max = _rms_gain_norm2_bound(qn_ref[...]) * (q_scale * q_scale)
        fixed_ok_ref[0] = _fixed_offset_is_safe(q_norm2_max, kstat_ref)

    qs_blk = qs_ref.at[pl.ds(pl.multiple_of(pl.program_id(1) * (2 * tq), 2 * tq), 2 * tq), :]

    def finish(l):
        for r in range(2):
            ot = acc_ref[:, r * tq:(r + 1) * tq] / l[:, r * tq:(r + 1) * tq]
            o_ref[:, r * HEAD_DIM:(r + 1) * HEAD_DIM] = ot.T.astype(o_ref.dtype)

    _attend(qs_blk, fixed_ok_ref, krot_ref, vt_ref, kstat_ref, acc_ref, finish, tk=tk)


def _flash_call(kernel, proj3, tables, params, *, n_groups, q_width, q_col, k_col, v_col,
                out_width, tq, name):
    batch, seq, _ = proj3.shape
    cos_t, sin_t = tables
    qb, kb, vb = q_col // q_width, k_col // HEAD_DIM, v_col // HEAD_DIM

    def grp(i):
        return i // n_groups, i % n_groups

    in_specs = [
        pl.BlockSpec((None, seq, q_width), lambda i, j: (grp(i)[0], 0, qb + grp(i)[1])),
        pl.BlockSpec((None, seq, HEAD_DIM), lambda i, j: (grp(i)[0], 0, kb + grp(i)[1])),
        pl.BlockSpec((None, seq, HEAD_DIM), lambda i, j: (grp(i)[0], 0, vb + grp(i)[1])),
        pl.BlockSpec((seq, LANES), lambda i, j: (0, 0)),
        pl.BlockSpec((seq, LANES), lambda i, j: (0, 0)),
    ] + [pl.BlockSpec(p.shape, lambda i, j: (0, 0)) for p in params]
    return pl.pallas_call(
        kernel,
        grid=(batch * n_groups, seq // tq),
        in_specs=in_specs,
        out_specs=pl.BlockSpec((None, tq, out_width), lambda i, j: (grp(i)[0], j, grp(i)[1])),
        out_shape=jax.ShapeDtypeStruct((batch, seq, GROUP_WIDTH), BF16),
        scratch_shapes=[pltpu.VMEM((seq, HEAD_DIM), BF16),
                        pltpu.VMEM((HEAD_DIM, seq), BF16),
                        pltpu.VMEM((SUBLANES, LANES), F32),
                        pltpu.VMEM((2 * seq, HEAD_DIM), BF16),
                        pltpu.SMEM((1,), jnp.int32),
                        pltpu.VMEM((HEAD_DIM, 2 * tq), F32)],
        compiler_params=_cparams(("parallel", "arbitrary")),
        name=name,
    )(proj3, proj3, proj3, cos_t, sin_t, *params)


def _conv_kernel(a_ref, g_ref, dw_ref, dwb_ref, lng_ref, lnb_ref, pw_ref, pwb_ref, wo_ref,
                 o_ref, wob_ref, u_ref, acc_ref, y_ref, *, tm):
    seq = a_ref.shape[0]
    t = pl.program_id(1)

    wob_ref[...] = wo_ref[...].astype(wob_ref.dtype)

    @pl.when(t == 0)
    def _():
        zeros = jnp.zeros((CONV_HALO, u_ref.shape[1]), F32)
        u_ref[0:CONV_HALO, :] = zeros
        u_ref[CONV_HALO + seq:2 * CONV_HALO + seq, :] = zeros

        def body(c, _):
            r0 = pl.multiple_of(c * ROW_CHUNK, ROW_CHUNK)
            a = a_ref[pl.ds(r0, ROW_CHUNK), :].astype(F32)
            g = g_ref[pl.ds(r0, ROW_CHUNK), :].astype(F32)
            u_ref[pl.ds(CONV_HALO + r0, ROW_CHUNK), :] = a * jax.nn.sigmoid(g)
            return 0

        lax.fori_loop(0, seq // ROW_CHUNK, body, 0)

    pad = CONV_WIDTH // 2
    sub = 8
    n_ch = u_ref.shape[1]

    def row_block(rb, _):
        base = pl.multiple_of(t * tm + rb * ROW_CHUNK, ROW_CHUNK)
        for cg in range(n_ch // LANES):
            cols = slice(cg * LANES, (cg + 1) * LANES)
            win = u_ref[pl.ds(base, ROW_CHUNK + 2 * CONV_HALO), cols]
            acc = jnp.zeros((ROW_CHUNK, LANES), F32) + dwb_ref[:, cols]
            for r in range(sub):
                shifted = win if r == 0 else pltpu.roll(win, win.shape[0] - r, 0)
                for a in range(2 * CONV_HALO // sub):
                    k = sub * a + r - (CONV_HALO - pad)
                    if 0 <= k < CONV_WIDTH:
                        acc = acc + shifted[sub * a:sub * a + ROW_CHUNK, :] * dw_ref[k:k + 1, cols]
            acc_ref[:, cols] = acc
        acc = acc_ref[...]
        mu = jnp.mean(acc, axis=-1, keepdims=True)
        cen = acc - mu
        var = jnp.mean(cen * cen, axis=-1, keepdims=True)
        y = cen * lax.rsqrt(var + EPS) * lng_ref[...] + lnb_ref[...]
        y_ref[pl.ds(pl.multiple_of(rb * ROW_CHUNK, ROW_CHUNK), ROW_CHUNK), :] = (
            y * jax.nn.sigmoid(y)).astype(BF16)
        return 0

    lax.fori_loop(0, tm // ROW_CHUNK, row_block, 0)
    o_ref[...] = (jnp.dot(y_ref[...], pw_ref[...].astype(BF16), preferred_element_type=F32)
                  + pwb_ref[...]).astype(o_ref.dtype)


def _conv_mixer(proj3, dw, dwb, lng, lnb, pw, pwb, w_out, layer, *, tm):
    batch, seq, _ = proj3.shape
    ch = GROUP_WIDTH
    n_t = seq // tm
    slab = w_out.shape[1] // (batch * n_t)
    assert slab * batch * n_t == w_out.shape[1] and slab % BF16_ROWS == 0
    vec = pl.BlockSpec((1, ch), lambda b, t: (0, 0))
    return pl.pallas_call(
        functools.partial(_conv_kernel, tm=tm),
        grid=(batch, n_t),
        in_specs=[pl.BlockSpec((None, seq, ch), lambda b, t: (b, 0, COL_B_A // ch)),
                  pl.BlockSpec((None, seq, ch), lambda b, t: (b, 0, COL_B_G // ch)),
                  pl.BlockSpec((CONV_WIDTH, ch), lambda b, t: (0, 0)),
                  vec, vec, vec,
                  pl.BlockSpec((None, ch, ch), lambda b, t: (layer, 0, 0)),
                  vec,
                  pl.BlockSpec((None, slab, w_out.shape[2]), lambda b, t: (layer, b * n_t + t, 0))],
        out_specs=[pl.BlockSpec((None, tm, ch), lambda b, t: (b, t, 0)),
                   pl.BlockSpec((slab, w_out.shape[2]), lambda b, t: (b * n_t + t, 0))],
        out_shape=[jax.ShapeDtypeStruct((batch, seq, ch), BF16),
                   jax.ShapeDtypeStruct(w_out.shape[1:], BF16)],
        scratch_shapes=[pltpu.VMEM((seq + 2 * CONV_HALO, ch), F32),
                        pltpu.VMEM((ROW_CHUNK, ch), F32),
                        pltpu.VMEM((tm, ch), BF16)],
        compiler_params=_cparams(("parallel", "arbitrary")),
        name="conv_mixer",
    )(proj3, proj3, dw, dwb, lng, lnb, pw, pwb, w_out)


def _na_geometry(seq):
    rows = seq // GRID_W
    qr_blk = Q_BLOCK // GRID_W
    mask_add = np.zeros((NA_CLASSES, Q_BLOCK, NA_BAND), np.float32)
    q_local = np.arange(Q_BLOCK)
    p = np.arange(NA_BAND)
    for c, blk in enumerate(_na_class_blocks(seq)):
        bs = int(np.clip(blk * qr_blk - NA_KH // 2, 0, rows - NA_BAND_ROWS))
        q_row = (blk * qr_blk + q_local // GRID_W)[:, None]
        q_col = (q_local % GRID_W)[:, None]
        k_row = (bs + p // GRID_W)[None, :]
        k_col = (p % GRID_W)[None, :]
        win_r = np.clip(q_row - NA_KH // 2, 0, rows - NA_KH)
        win_c = np.clip(q_col - NA_KW // 2, 0, GRID_W - NA_KW)
        ok = (k_row >= win_r) & (k_row < win_r + NA_KH) & (k_col >= win_c) & (k_col < win_c + NA_KW)
        mask_add[c] = np.where(ok, 0.0, -1e30)
    return jnp.asarray(mask_add)


def _na_class_blocks(seq):
    n_blk = seq // Q_BLOCK
    return (0, 1, 2, n_blk - 2, n_blk - 1)


def _rpb_tiles_kernel(rpb_ref, mask_ref, o_ref, t2_ref, *, seq):
    h = pl.program_id(0)
    n_c = 2 * NA_KW - 1
    n_r = 2 * NA_KH - 1
    shape = (GRID_W, LANES)
    lane = lax.broadcasted_iota(jnp.int32, shape, 1)
    q_col = lax.broadcasted_iota(jnp.int32, shape, 0)
    ic = jnp.clip((lane & (GRID_W - 1)) - q_col + (NA_KW - 1), 0, n_c - 1)
    left = lane < GRID_W
    base = h * (n_r * n_c)
    for i in range(NA_T2):
        i_l = min(max(i - 1, 0), n_r - 1)
        i_r = min(max(i, 0), n_r - 1)

        acc = jnp.zeros(shape, F32)
        for j in range(n_c):
            coef = jnp.where(left, rpb_ref[base + i_l * n_c + j], rpb_ref[base + i_r * n_c + j])
            acc = acc + jnp.where(ic == j, coef, 0.0)
        t2_ref[i] = acc

    rows = seq // GRID_W
    qr_blk = Q_BLOCK // GRID_W
    for c, blk in enumerate(_na_class_blocks(seq)):
        bs = min(max(blk * qr_blk - NA_KH // 2, 0), rows - NA_BAND_ROWS)
        for a in range(qr_blk):
            for r2 in range(NA_BAND_ROWS // 2):
                i1 = bs + 2 * r2 - (blk * qr_blk + a) + (NA_KH - 1)
                idx = min(max(i1, -1), NA_T2 - 2) + 1
                rs = slice(a * GRID_W, (a + 1) * GRID_W)
                cs = slice(r2 * LANES, (r2 + 1) * LANES)
                o_ref[c, rs, cs] = (t2_ref[idx] + mask_ref[c, rs, cs]) * LOG2E


def _rpb_tiles(rpb, mask_add, seq):
    heads = rpb.shape[0]
    return pl.pallas_call(
        functools.partial(_rpb_tiles_kernel, seq=seq),
        grid=(heads,),
        in_specs=[pl.BlockSpec(memory_space=pltpu.SMEM),
                  pl.BlockSpec(mask_add.shape, lambda h: (0, 0, 0))],
        out_specs=pl.BlockSpec((None,) + mask_add.shape, lambda h: (h, 0, 0, 0)),
        out_shape=jax.ShapeDtypeStruct((heads,) + mask_add.shape, F32),
        scratch_shapes=[pltpu.VMEM((NA_T2, GRID_W, LANES), F32)],
        compiler_params=_cparams(("parallel",)),
        name="na_rpb_tiles",
    )(rpb.reshape(-1), mask_add)


def _na_kernel(q_ref, k_ref, v_ref, bias_ref, o_ref):
    seq = q_ref.shape[0]
    rows = seq // GRID_W
    qr_blk = Q_BLOCK // GRID_W
    n_blk = seq // Q_BLOCK

    classes = _na_class_blocks(seq)

    def band(blk):
        k0 = min(max(blk * qr_blk - NA_KH // 2, 0), rows - NA_BAND_ROWS) * GRID_W
        return slice(k0, k0 + NA_BAND)

    def scores(blk):
        q = q_ref[blk * Q_BLOCK:(blk + 1) * Q_BLOCK, :]
        return lax.dot_general(q, k_ref[band(blk), :], (((1,), (1,)), ((), ())),
                               preferred_element_type=F32)

    pending = [scores(b) for b in range(NA_LOOKAHEAD)]
    for blk in range(n_blk):
        s = pending.pop(0)
        if blk + NA_LOOKAHEAD < n_blk:
            pending.append(scores(blk + NA_LOOKAHEAD))
        cls = classes.index(blk) if blk in classes else 2
        s = s * (HEAD_DIM ** -0.5 * LOG2E) + bias_ref[cls]
        m = jnp.max(s, axis=1, keepdims=True)
        p = jnp.exp2(s - m)
        l = jnp.sum(p, axis=1, keepdims=True)
        o = jnp.dot(p.astype(BF16), v_ref[band(blk), :], preferred_element_type=F32) / l
        o_ref[blk * Q_BLOCK:(blk + 1) * Q_BLOCK, :] = o.astype(o_ref.dtype)


def _na_mixer(proj3, bias):
    batch, seq, _ = proj3.shape
    heads = GROUP_HEADS

    def head_spec(col):
        return pl.BlockSpec((None, seq, HEAD_DIM), lambda b, h: (b, 0, col // HEAD_DIM + h))

    return pl.pallas_call(
        _na_kernel,
        grid=(batch, heads),
        in_specs=[head_spec(COL_D_Q), head_spec(COL_D_K), head_spec(COL_D_V),
                  pl.BlockSpec((None,) + bias.shape[1:], lambda b, h: (h, 0, 0, 0))],
        out_specs=pl.BlockSpec((None, seq, HEAD_DIM), lambda b, h: (b, 0, h)),
        out_shape=jax.ShapeDtypeStruct((batch, seq, GROUP_WIDTH), BF16),
        compiler_params=_cparams(("parallel", "parallel")),
        name="na_mixer",
    )(proj3, proj3, proj3, bias)


def kernel(x, norm_mix_pre, norm_mix_post, norm_ffn_pre, norm_ffn_post, w_in, w_out, diff_lambda, diff_subln, conv_dw, conv_dw_b, conv_ln_g, conv_ln_b, conv_pw, conv_pw_b, gqa_q_norm, gqa_k_norm, na_rpb, ffn_gate, ffn_up, ffn_down):
    batch, seq, d = x.shape
    depth = w_in.shape[0]
    tokens = batch * seq

    t = np.arange(seq)
    diff_tables = _rope_tables(t, t)
    axial_tables = _rope_tables(t // GRID_W, t % GRID_W)
    mask_add = _na_geometry(seq)

    def row(v):
        return v.reshape(1, -1)

    xf = x.reshape(tokens, d)
    h = _rmsnorm(xf, row(norm_mix_pre[0]), tm=RMSNORM_TM)
    for l in range(depth):
        lam_init = 0.8 - 0.6 * math.exp(-0.3 * l)
        proj = _in_proj(h, w_in, l, tm=IN_PROJ_TM, tn=IN_PROJ_TN)
        proj3 = proj.reshape(batch, seq, IN_COLS)

        out_a = _flash_call(
            functools.partial(_diff_attn_kernel, tq=ATTN_TQ, tk=ATTN_TK, lam_init=lam_init),
            proj3, diff_tables, (diff_lambda[l], diff_subln[l].reshape(-1, 1)),
            n_groups=GROUP_HEADS, q_width=HEAD_DIM, q_col=COL_A_Q, k_col=COL_A_K, v_col=COL_A_V,
            out_width=HEAD_DIM, tq=ATTN_TQ, name="diff_attn")
        out_b, w_out_bf16 = _conv_mixer(
            proj3, conv_dw[l], row(conv_dw_b[l]), row(conv_ln_g[l]), row(conv_ln_b[l]),
            conv_pw, row(conv_pw_b[l]), w_out, l, tm=CONV_TM)
        out_c = _flash_call(
            functools.partial(_gqa_attn_kernel, tq=ATTN_TQ, tk=ATTN_TK),
            proj3, axial_tables, (row(gqa_q_norm[l]), row(gqa_k_norm[l])),
            n_groups=GQA_KV_HEADS, q_width=2 * HEAD_DIM, q_col=COL_C_Q, k_col=COL_C_K, v_col=COL_C_V,
            out_width=2 * HEAD_DIM, tq=ATTN_TQ, name="gqa_attn")
        out_d = _na_mixer(proj3, _rpb_tiles(na_rpb[l], mask_add, seq))

        parts = [o.reshape(tokens, GROUP_WIDTH) for o in (out_a, out_b, out_c, out_d)]
        xf, h = _out_proj(parts, w_out_bf16, row(norm_mix_post[l]), xf,
                          row(norm_ffn_pre[l]), tm=OUT_PROJ_TM)

        act, w_down = _ffn_up(h, ffn_gate, ffn_up, ffn_down, l, tm=FFN_UP_TM, tn=FFN_UP_TN)
        next_gain = row(norm_mix_pre[l + 1]) if l + 1 < depth else None
        xf, h = _ffn_down(act, w_down, row(norm_ffn_post[l]), xf, next_gain,
                          tm=FFN_DOWN_TM)
    return xf.reshape(batch, seq, d)
```

```python
import functools
import math

import numpy as np
import jax
import jax.numpy as jnp
from jax import lax
from jax.experimental import pallas as pl
from jax.experimental.pallas import tpu as pltpu

D_MODEL = 2048
HEAD_DIM = 128
GROUP_HEADS = 4
GROUP_WIDTH = GROUP_HEADS * HEAD_DIM
DIFF_QK_DIM = HEAD_DIM // 2
CONV_WIDTH = 31
GQA_KV_HEADS = 2
NA_KH = 8
NA_KW = 16
GRID_W = 64
Q_BLOCK = 128
ROPE_THETA = 10000.0
EPS = 1e-6

COL_A_Q, COL_A_K, COL_A_V = 0, 512, 1024
COL_B_A, COL_B_G = 1536, 2048
COL_C_Q, COL_C_K, COL_C_V = 2560, 3072, 3328
COL_D_Q, COL_D_K, COL_D_V = 3584, 4096, 4608
IN_COLS = 5120

LANES = 128
VMEM_LIMIT = 56 * 1024 * 1024
VMEM_LIMIT_FFN_DOWN = 60 * 1024 * 1024
ROW_CHUNK = 128

RMSNORM_TM = 1024
IN_PROJ_TM, IN_PROJ_TN = 2048, 1024
CONV_TM = 512
OUT_PROJ_TM = 512
FFN_UP_TM, FFN_UP_TN = 2048, 512
FFN_DOWN_TM = 512
FFN_UP_SUBTILE = 1024
ATTN_TQ = 512
ATTN_TK = 512

NA_BAND_ROWS = 10
NA_BAND = NA_BAND_ROWS * GRID_W
NA_CLASSES = 5
NA_T2 = 16
NA_LOOKAHEAD = 2
CONV_HALO = 16

F32 = jnp.float32
BF16 = jnp.bfloat16


def _cparams(semantics, vmem_limit=VMEM_LIMIT):
    return pltpu.CompilerParams(dimension_semantics=semantics,
                                vmem_limit_bytes=vmem_limit)


def _rms(x):
    return x * lax.rsqrt(jnp.mean(x * x, axis=-1, keepdims=True) + EPS)


def _rmsnorm_kernel(x_ref, g_ref, h_ref):
    def body(c, _):
        r0 = pl.multiple_of(c * ROW_CHUNK, ROW_CHUNK)
        h_ref[pl.ds(r0, ROW_CHUNK), :] = (_rms(x_ref[pl.ds(r0, ROW_CHUNK), :]) * g_ref[...]).astype(h_ref.dtype)
        return 0

    lax.fori_loop(0, x_ref.shape[0] // ROW_CHUNK, body, 0)


def _rmsnorm(x, g, *, tm):
    m, d = x.shape
    return pl.pallas_call(
        _rmsnorm_kernel,
        grid=(m // tm,),
        in_specs=[pl.BlockSpec((tm, d), lambda i: (i, 0)),
                  pl.BlockSpec((1, d), lambda i: (0, 0))],
        out_specs=pl.BlockSpec((tm, d), lambda i: (i, 0)),
        out_shape=jax.ShapeDtypeStruct((m, d), BF16),
        compiler_params=_cparams(("parallel",)),
        name="rmsnorm",
    )(x, g)


def _cast_weight_tile(w_ref, wb_ref):
    def body(c, _):
        r0 = pl.multiple_of(c * ROW_CHUNK, ROW_CHUNK)
        wb_ref[pl.ds(r0, ROW_CHUNK), :] = w_ref[pl.ds(r0, ROW_CHUNK), :].astype(wb_ref.dtype)
        return 0

    lax.fori_loop(0, w_ref.shape[0] // ROW_CHUNK, body, 0)


def _in_proj_kernel(h_ref, w_ref, o_ref, wb_ref):
    @pl.when(pl.program_id(1) == 0)
    def _():
        _cast_weight_tile(w_ref, wb_ref)

    o_ref[...] = jnp.dot(h_ref[...], wb_ref[...], preferred_element_type=F32).astype(o_ref.dtype)


def _in_proj(h, w, layer, *, tm, tn):
    m, d = h.shape
    n = w.shape[2]
    return pl.pallas_call(
        _in_proj_kernel,
        grid=(n // tn, m // tm),
        in_specs=[pl.BlockSpec((tm, d), lambda j, i: (i, 0)),
                  pl.BlockSpec((None, d, tn), lambda j, i: (layer, 0, j))],
        out_specs=pl.BlockSpec((tm, tn), lambda j, i: (i, j)),
        out_shape=jax.ShapeDtypeStruct((m, n), BF16),
        scratch_shapes=[pltpu.VMEM((d, tn), BF16)],
        compiler_params=_cparams(("parallel", "arbitrary")),
        name="in_proj",
    )(h, w)


def _ffn_up_kernel(h_ref, wg_ref, wu_ref, wd_ref, o_ref, wdb_ref, wgb_ref, wub_ref):
    @pl.when(pl.program_id(1) == 0)
    def _():
        _cast_weight_tile(wg_ref, wgb_ref)
        _cast_weight_tile(wu_ref, wub_ref)

    wdb_ref[...] = wd_ref[...].astype(wdb_ref.dtype)

    for r in range(h_ref.shape[0] // FFN_UP_SUBTILE):
        rows = slice(r * FFN_UP_SUBTILE, (r + 1) * FFN_UP_SUBTILE)
        h = h_ref[rows, :]
        gate = jnp.dot(h, wgb_ref[...], preferred_element_type=F32)
        up = jnp.dot(h, wub_ref[...], preferred_element_type=F32)
        o_ref[rows, :] = (gate * jax.nn.sigmoid(gate) * up).astype(o_ref.dtype)


def _ffn_up(h, wg, wu, wd, layer, *, tm, tn):
    m, d = h.shape
    n = wg.shape[2]
    n_i = m // tm
    slab = wd.shape[1] // ((n // tn) * n_i)
    assert slab * (n // tn) * n_i == wd.shape[1] and slab % BF16_ROWS == 0
    w_spec = pl.BlockSpec((None, d, tn), lambda j, i: (layer, 0, j))
    return pl.pallas_call(
        _ffn_up_kernel,
        grid=(n // tn, n_i),
        in_specs=[pl.BlockSpec((tm, d), lambda j, i: (i, 0)), w_spec, w_spec,
                  pl.BlockSpec((None, slab, wd.shape[2]), lambda j, i: (layer, j * n_i + i, 0))],
        out_specs=[pl.BlockSpec((tm, tn), lambda j, i: (i, j)),
                   pl.BlockSpec((slab, wd.shape[2]), lambda j, i: (j * n_i + i, 0))],
        out_shape=[jax.ShapeDtypeStruct((m, n), BF16),
                   jax.ShapeDtypeStruct(wd.shape[1:], BF16)],
        scratch_shapes=[pltpu.VMEM((d, tn), BF16), pltpu.VMEM((d, tn), BF16)],
        compiler_params=_cparams(("parallel", "arbitrary")),
        name="ffn_up",
    )(h, wg, wu, wd)


def _project_norm_residual(lhs_ref, w_ref, g_ref, x_ref, gn_ref, o_ref, hn_ref, acc_ref):
    n_part, part, _ = acc_ref.shape
    for b in range(n_part):
        acc_ref[b] = jnp.dot(lhs_ref[b * part:(b + 1) * part, :], w_ref[...],
                             preferred_element_type=F32)
    for b in range(n_part):
        for c in range(part // ROW_CHUNK):
            rows = slice(b * part + c * ROW_CHUNK, b * part + (c + 1) * ROW_CHUNK)
            o = x_ref[rows, :] + _rms(acc_ref[b, c * ROW_CHUNK:(c + 1) * ROW_CHUNK, :]) * g_ref[...]
            o_ref[rows, :] = o
            if hn_ref is not None:
                hn_ref[rows, :] = (_rms(o) * gn_ref[...]).astype(hn_ref.dtype)


def _ffn_down_kernel(*refs, emit_next):
    if emit_next:
        a_ref, w_ref, g_ref, x_ref, gn_ref, o_ref, hn_ref, acc_ref = refs
    else:
        a_ref, w_ref, g_ref, x_ref, o_ref, acc_ref = refs
        gn_ref = hn_ref = None
    _project_norm_residual(a_ref, w_ref, g_ref, x_ref, gn_ref, o_ref, hn_ref, acc_ref)


def _ffn_down(a, w, g, x, gn, *, tm):
    m, kdim = a.shape
    d = w.shape[1]
    emit_next = gn is not None
    vec = pl.BlockSpec((1, d), lambda i: (0, 0))
    row_tile = pl.BlockSpec((tm, d), lambda i: (i, 0))
    outs = pl.pallas_call(
        functools.partial(_ffn_down_kernel, emit_next=emit_next),
        grid=(m // tm,),
        in_specs=[pl.BlockSpec((tm, kdim), lambda i: (i, 0)),
                  pl.BlockSpec((kdim, d), lambda i: (0, 0), pipeline_mode=pl.Buffered(1)),
                  vec, row_tile] + [vec] * emit_next,
        out_specs=[row_tile] + [row_tile] * emit_next,
        out_shape=[jax.ShapeDtypeStruct((m, d), F32)] + [jax.ShapeDtypeStruct((m, d), BF16)] * emit_next,
        scratch_shapes=[pltpu.VMEM((tm // ROW_CHUNK, ROW_CHUNK, d), F32)],
        compiler_params=_cparams(("parallel",), VMEM_LIMIT_FFN_DOWN),
        name="ffn_down",
    )(a, w, g, x, *([gn] * emit_next))
    return (outs[0], outs[1]) if emit_next else (outs[0], None)


def _out_proj_kernel(a_ref, b_ref, c_ref, d_ref, w_ref, g_ref, x_ref, gn_ref, o_ref, hn_ref,
                     lhs_ref, acc_ref):
    for p, part in enumerate((a_ref, b_ref, c_ref, d_ref)):
        lhs_ref[:, p * GROUP_WIDTH:(p + 1) * GROUP_WIDTH] = part[...]
    _project_norm_residual(lhs_ref, w_ref, g_ref, x_ref, gn_ref, o_ref, hn_ref, acc_ref)


def _out_proj(parts, w, g, x, gn, *, tm):
    m, d = x.shape
    part_spec = pl.BlockSpec((tm, GROUP_WIDTH), lambda i: (i, 0))
    vec = pl.BlockSpec((1, d), lambda i: (0, 0))
    row_tile = pl.BlockSpec((tm, d), lambda i: (i, 0))
    return pl.pallas_call(
        _out_proj_kernel,
        grid=(m // tm,),
        in_specs=[part_spec, part_spec, part_spec, part_spec,
                  pl.BlockSpec((4 * GROUP_WIDTH, d), lambda i: (0, 0)),
                  vec, row_tile, vec],
        out_specs=[row_tile, row_tile],
        out_shape=[jax.ShapeDtypeStruct((m, d), F32), jax.ShapeDtypeStruct((m, d), BF16)],
        scratch_shapes=[pltpu.VMEM((tm, 4 * GROUP_WIDTH), BF16),
                        pltpu.VMEM((tm // ROW_CHUNK, ROW_CHUNK, d), F32)],
        compiler_params=_cparams(("parallel",)),
        name="out_proj",
    )(*parts, w, g, x, gn)


def _rope_tables(pos_lo, pos_hi):
    half = DIFF_QK_DIM // 2
    inv = ROPE_THETA ** (-np.arange(half, dtype=np.float64) / half)
    lane = np.arange(LANES)
    pos = np.where(lane[None, :] < DIFF_QK_DIM, pos_lo[:, None], pos_hi[:, None]).astype(np.float64)
    ang = pos * inv[lane % half][None, :]
    sign = np.where((lane % DIFF_QK_DIM) < half, -1.0, 1.0)
    return (jnp.asarray(np.cos(ang), dtype=F32),
            jnp.asarray(np.sin(ang) * sign[None, :], dtype=F32))


def _rope_partner(x):
    lane = lax.broadcasted_iota(jnp.int32, x.shape, 1)
    first_half = (lane & (DIFF_QK_DIM // 2)) == 0
    return jnp.where(first_half, pltpu.roll(x, LANES - DIFF_QK_DIM // 2, 1),
                     pltpu.roll(x, DIFF_QK_DIM // 2, 1))


def _rope_partner_matrix():
    src = lax.broadcasted_iota(jnp.int32, (LANES, LANES), 0)
    dst = lax.broadcasted_iota(jnp.int32, (LANES, LANES), 1)
    half = DIFF_QK_DIM // 2
    partner = jnp.where((dst & half) == 0, dst + half, dst - half)
    return jnp.where(src == partner, 1.0, 0.0).astype(BF16)


def _rope_bf16(x, pmat, cos, sin_signed):
    partner = jnp.dot(x, pmat, preferred_element_type=F32)
    return x.astype(F32) * cos + partner * sin_signed


LOG2E = math.log2(math.e)


ATTN_OFFSET_SLACK = 64.0
PREP_UNROLL = 32
SUBLANES = 8
BF16_ROWS = 16


def _scores(krot_ref, qs_ref, j, tk):
    k = krot_ref[j * tk:(j + 1) * tk, :]
    return lax.dot_general(k, qs_ref[...], (((1,), (1,)), ((), ())), preferred_element_type=F32)


def _attend_online(qs_ref, krot_ref, vt_ref, acc_ref, *, tk):
    n_tiles = krot_ref.shape[0] // tk
    m = l = None
    s_next = _scores(krot_ref, qs_ref, 0, tk)
    for j in range(n_tiles):
        s = s_next
        if j + 1 < n_tiles:
            s_next = _scores(krot_ref, qs_ref, j + 1, tk)
        m_cur = jnp.max(s, axis=0, keepdims=True)
        m_next = m_cur if m is None else jnp.maximum(m, m_cur)
        p = jnp.exp2(s - m_next)
        pv = jnp.dot(vt_ref[:, j * tk:(j + 1) * tk], p.astype(BF16), preferred_element_type=F32)
        if m is None:
            l = jnp.sum(p, axis=0, keepdims=True)
            acc_ref[...] = pv
        else:
            alpha = jnp.exp2(m - m_next)
            l = alpha * l + jnp.sum(p, axis=0, keepdims=True)
            acc_ref[...] = alpha * acc_ref[...] + pv
        m = m_next
    return l


def _attend_fixed(qs_ref, krot_ref, vt_ref, acc_ref, offset, *, tk):
    n_tiles = krot_ref.shape[0] // tk
    l = None
    s_next = _scores(krot_ref, qs_ref, 0, tk)
    for j in range(n_tiles):
        s = s_next
        if j + 1 < n_tiles:
            s_next = _scores(krot_ref, qs_ref, j + 1, tk)
        p = jnp.exp2(s - offset)
        pv = jnp.dot(vt_ref[:, j * tk:(j + 1) * tk], p.astype(BF16), preferred_element_type=F32)
        if l is None:
            l = jnp.sum(p, axis=0, keepdims=True)
            acc_ref[...] = pv
        else:
            l = l + jnp.sum(p, axis=0, keepdims=True)
            acc_ref[...] += pv
    return l


def _fixed_offset_is_safe(q_norm2_max, kstat_ref):
    worst_exponent = jnp.sqrt(q_norm2_max) * kstat_ref[1:2, 0:1]
    return (jnp.max(worst_exponent) <= ATTN_OFFSET_SLACK).astype(jnp.int32)


def _attend(qs_ref, fixed_ok_ref, krot_ref, vt_ref, kstat_ref, acc_ref, finish, *, tk):
    use_fixed = fixed_ok_ref[0] == 1

    @pl.when(use_fixed)
    def _():
        qf = qs_ref[...].astype(F32)
        ones = jnp.ones((BF16_ROWS, qf.shape[1]), BF16)
        q_norm2 = lax.dot_general(ones, (qf * qf).astype(BF16), (((1,), (1,)), ((), ())),
                                  preferred_element_type=F32)
        offset = jnp.sqrt(q_norm2[0:1]) * kstat_ref[0:1, 0:1]
        finish(_attend_fixed(qs_ref, krot_ref, vt_ref, acc_ref, offset, tk=tk))

    @pl.when(jnp.logical_not(use_fixed))
    def _():
        finish(_attend_online(qs_ref, krot_ref, vt_ref, acc_ref, tk=tk))


def _rms_gain_norm2_bound(gain):
    return gain.shape[1] * jnp.max(gain * gain, axis=1, keepdims=True)


def _rotated_rows(x, cos, sin_signed, pmat, gain_pair):
    if gain_pair is None:
        return _rope_bf16(x, pmat, cos, sin_signed)
    xf = x.astype(F32)
    inv_rms = lax.rsqrt(jnp.mean(xf * xf, axis=-1, keepdims=True) + EPS)
    return _rope_bf16(x, pmat, cos * gain_pair[0], sin_signed * gain_pair[1]) * inv_rms


def _gain_pair(gain):
    tiled = jnp.broadcast_to(gain, (SUBLANES, gain.shape[1]))
    return gain, _rope_partner(tiled)[0:1]


def _prepare_keys_values(k_ref, v_ref, cos_ref, sin_ref, krot_ref, vt_ref, kstat_ref, pmat, gain):
    seq = k_ref.shape[0]
    gain_pair = None if gain is None else _gain_pair(gain)

    def body(c, carry):
        k_sum, k_max2 = carry
        r0 = pl.multiple_of(c * ROW_CHUNK, ROW_CHUNK)
        rows = pl.ds(r0, ROW_CHUNK)
        kb = _rotated_rows(k_ref[rows, :], cos_ref[rows, :], sin_ref[rows, :], pmat,
                           gain_pair).astype(BF16)
        krot_ref[rows, :] = kb
        vt_ref[:, rows] = v_ref[rows, :].T
        kf = kb.astype(F32)
        if gain is None:
            norm2 = jnp.sum(kf * kf, axis=1, keepdims=True)
            k_max2 = jnp.maximum(k_max2, jnp.max(norm2, axis=0, keepdims=True))
        return k_sum + jnp.sum(kf, axis=0, keepdims=True), k_max2

    k_max2_init = (jnp.zeros((1, 1), F32) if gain is None
                   else _rms_gain_norm2_bound(gain))
    k_sum, k_max2 = lax.fori_loop(0, seq // ROW_CHUNK, body,
                                  (jnp.zeros((1, k_ref.shape[1]), F32), k_max2_init),
                                  unroll=PREP_UNROLL)
    k_bar = k_sum * (1.0 / seq)
    k_bar_norm = jnp.sqrt(jnp.sum(k_bar * k_bar, axis=1, keepdims=True))
    k_max = jnp.sqrt(k_max2) * 1.01
    kstat_ref[0:1, :] = jnp.broadcast_to(k_max, (1, kstat_ref.shape[1]))
    kstat_ref[1:2, :] = jnp.broadcast_to(k_max + k_bar_norm, (1, kstat_ref.shape[1]))


def _stacked_row(c, tq):
    per_block = tq // ROW_CHUNK
    return pl.multiple_of((c // per_block) * (2 * tq) + (c % per_block) * ROW_CHUNK, ROW_CHUNK)


def _diff_attn_kernel(q_ref, k_ref, v_ref, cos_ref, sin_ref, lam_ref, g_ref, o_ref,
                      krot_ref, vt_ref, kstat_ref, qs_ref, fixed_ok_ref, acc_ref, *, tq, tk, lam_init):
    @pl.when(pl.program_id(1) == 0)
    def _():
        pmat = _rope_partner_matrix()
        _prepare_keys_values(k_ref, v_ref, cos_ref, sin_ref, krot_ref, vt_ref, kstat_ref, pmat, None)

        def body(c, q_norm2_max):
            rows = pl.ds(pl.multiple_of(c * ROW_CHUNK, ROW_CHUNK), ROW_CHUNK)
            q = _rotated_rows(q_ref[rows, :], cos_ref[rows, :], sin_ref[rows, :], pmat, None) * (
                DIFF_QK_DIM ** -0.5 * LOG2E)
            lane = lax.broadcasted_iota(jnp.int32, q.shape, 1)
            dst = _stacked_row(c, tq)
            qs_ref[pl.ds(dst, ROW_CHUNK), :] = jnp.where(lane < DIFF_QK_DIM, q, 0.0).astype(BF16)
            qs_ref[pl.ds(dst + tq, ROW_CHUNK), :] = jnp.where(lane >= DIFF_QK_DIM, q, 0.0).astype(BF16)
            return jnp.maximum(q_norm2_max,
                               jnp.max(jnp.sum(q * q, axis=1, keepdims=True), axis=0, keepdims=True))

        q_norm2_max = lax.fori_loop(0, q_ref.shape[0] // ROW_CHUNK, body, jnp.zeros((1, 1), F32),
                                    unroll=PREP_UNROLL)
        fixed_ok_ref[0] = _fixed_offset_is_safe(q_norm2_max, kstat_ref)

    qs_blk = qs_ref.at[pl.ds(pl.multiple_of(pl.program_id(1) * (2 * tq), 2 * tq), 2 * tq), :]

    def finish(l):
        lp = lam_ref[...]
        lam = (jnp.exp(jnp.sum(lp[0:1] * lp[1:2], keepdims=True))
               - jnp.exp(jnp.sum(lp[2:3] * lp[3:4], keepdims=True)) + lam_init)
        ot = (acc_ref[:, 0:tq] / l[:, 0:tq] - lam * (acc_ref[:, tq:2 * tq] / l[:, tq:2 * tq]))
        ms = jnp.mean(ot * ot, axis=0, keepdims=True)
        ot = ot * lax.rsqrt(ms + EPS) * (g_ref[...] * (1.0 - lam_init))
        o_ref[...] = ot.T.astype(o_ref.dtype)

    _attend(qs_blk, fixed_ok_ref, krot_ref, vt_ref, kstat_ref, acc_ref, finish, tk=tk)


def _gqa_attn_kernel(q_ref, k_ref, v_ref, cos_ref, sin_ref, qn_ref, kn_ref, o_ref,
                     krot_ref, vt_ref, kstat_ref, qs_ref, fixed_ok_ref, acc_ref, *, tq, tk):
    @pl.when(pl.program_id(1) == 0)
    def _():
        pmat = _rope_partner_matrix()
        _prepare_keys_values(k_ref, v_ref, cos_ref, sin_ref, krot_ref, vt_ref, kstat_ref, pmat,
                             kn_ref[...])

        q_scale = HEAD_DIM ** -0.5 * LOG2E
        q_gain_pair = _gain_pair(qn_ref[...] * q_scale)

        def body(c, _):
            rows = pl.ds(pl.multiple_of(c * ROW_CHUNK, ROW_CHUNK), ROW_CHUNK)
            dst = _stacked_row(c, tq)
            cos, sin_signed = cos_ref[rows, :], sin_ref[rows, :]
            for r in range(2):
                q = _rotated_rows(q_ref[rows, r * HEAD_DIM:(r + 1) * HEAD_DIM], cos, sin_signed,
                                  pmat, q_gain_pair)
                qs_ref[pl.ds(dst + r * tq, ROW_CHUNK), :] = q.astype(BF16)
            return 0

        lax.fori_loop(0, q_ref.shape[0] // ROW_CHUNK, body, 0, unroll=PREP_UNROLL)
        q_norm2_max = _rms_gain_norm2_bound(qn_ref[...]) * (q_scale * q_scale)
        fixed_ok_ref[0] = _fixed_offset_is_safe(q_norm2_max, kstat_ref)

    qs_blk = qs_ref.at[pl.ds(pl.multiple_of(pl.program_id(1) * (2 * tq), 2 * tq), 2 * tq), :]

    def finish(l):
        for r in range(2):
            ot = acc_ref[:, r * tq:(r + 1) * tq] / l[:, r * tq:(r + 1) * tq]
            o_ref[:, r * HEAD_DIM:(r + 1) * HEAD_DIM] = ot.T.astype(o_ref.dtype)

    _attend(qs_blk, fixed_ok_ref, krot_ref, vt_ref, kstat_ref, acc_ref, finish, tk=tk)


def _flash_call(kernel, proj3, tables, params, *, n_groups, q_width, q_col, k_col, v_col,
                out_width, tq, name):
    batch, seq, _ = proj3.shape
    cos_t, sin_t = tables
    qb, kb, vb = q_col // q_width, k_col // HEAD_DIM, v_col // HEAD_DIM

    def grp(i):
        return i // n_groups, i % n_groups

    in_specs = [
        pl.BlockSpec((None, seq, q_width), lambda i, j: (grp(i)[0], 0, qb + grp(i)[1])),
        pl.BlockSpec((None, seq, HEAD_DIM), lambda i, j: (grp(i)[0], 0, kb + grp(i)[1])),
        pl.BlockSpec((None, seq, HEAD_DIM), lambda i, j: (grp(i)[0], 0, vb + grp(i)[1])),
        pl.BlockSpec((seq, LANES), lambda i, j: (0, 0)),
        pl.BlockSpec((seq, LANES), lambda i, j: (0, 0)),
    ] + [pl.BlockSpec(p.shape, lambda i, j: (0, 0)) for p in params]
    return pl.pallas_call(
        kernel,
        grid=(batch * n_groups, seq // tq),
        in_specs=in_specs,
        out_specs=pl.BlockSpec((None, tq, out_width), lambda i, j: (grp(i)[0], j, grp(i)[1])),
        out_shape=jax.ShapeDtypeStruct((batch, seq, GROUP_WIDTH), BF16),
        scratch_shapes=[pltpu.VMEM((seq, HEAD_DIM), BF16),
                        pltpu.VMEM((HEAD_DIM, seq), BF16),
                        pltpu.VMEM((SUBLANES, LANES), F32),
                        pltpu.VMEM((2 * seq, HEAD_DIM), BF16),
                        pltpu.SMEM((1,), jnp.int32),
                        pltpu.VMEM((HEAD_DIM, 2 * tq), F32)],
        compiler_params=_cparams(("parallel", "arbitrary")),
        name=name,
    )(proj3, proj3, proj3, cos_t, sin_t, *params)


def _conv_kernel(a_ref, g_ref, dw_ref, dwb_ref, lng_ref, lnb_ref, pw_ref, pwb_ref, wo_ref,
                 o_ref, wob_ref, u_ref, acc_ref, y_ref, *, tm):
    seq = a_ref.shape[0]
    t = pl.program_id(1)

    wob_ref[...] = wo_ref[...].astype(wob_ref.dtype)

    @pl.when(t == 0)
    def _():
        zeros = jnp.zeros((CONV_HALO, u_ref.shape[1]), F32)
        u_ref[0:CONV_HALO, :] = zeros
        u_ref[CONV_HALO + seq:2 * CONV_HALO + seq, :] = zeros

        def body(c, _):
            r0 = pl.multiple_of(c * ROW_CHUNK, ROW_CHUNK)
            a = a_ref[pl.ds(r0, ROW_CHUNK), :].astype(F32)
            g = g_ref[pl.ds(r0, ROW_CHUNK), :].astype(F32)
            u_ref[pl.ds(CONV_HALO + r0, ROW_CHUNK), :] = a * jax.nn.sigmoid(g)
            return 0

        lax.fori_loop(0, seq // ROW_CHUNK, body, 0)

    pad = CONV_WIDTH // 2
    sub = 8
    n_ch = u_ref.shape[1]

    def row_block(rb, _):
        base = pl.multiple_of(t * tm + rb * ROW_CHUNK, ROW_CHUNK)
        for cg in range(n_ch // LANES):
            cols = slice(cg * LANES, (cg + 1) * LANES)
            win = u_ref[pl.ds(base, ROW_CHUNK + 2 * CONV_HALO), cols]
            acc = jnp.zeros((ROW_CHUNK, LANES), F32) + dwb_ref[:, cols]
            for r in range(sub):
                shifted = win if r == 0 else pltpu.roll(win, win.shape[0] - r, 0)
                for a in range(2 * CONV_HALO // sub):
                    k = sub * a + r - (CONV_HALO - pad)
                    if 0 <= k < CONV_WIDTH:
                        acc = acc + shifted[sub * a:sub * a + ROW_CHUNK, :] * dw_ref[k:k + 1, cols]
            acc_ref[:, cols] = acc
        acc = acc_ref[...]
        mu = jnp.mean(acc, axis=-1, keepdims=True)
        cen = acc - mu
        var = jnp.mean(cen * cen, axis=-1, keepdims=True)
        y = cen * lax.rsqrt(var + EPS) * lng_ref[...] + lnb_ref[...]
        y_ref[pl.ds(pl.multiple_of(rb * ROW_CHUNK, ROW_CHUNK), ROW_CHUNK), :] = (
            y * jax.nn.sigmoid(y)).astype(BF16)
        return 0

    lax.fori_loop(0, tm // ROW_CHUNK, row_block, 0)
    o_ref[...] = (jnp.dot(y_ref[...], pw_ref[...].astype(BF16), preferred_element_type=F32)
                  + pwb_ref[...]).astype(o_ref.dtype)


def _conv_mixer(proj3, dw, dwb, lng, lnb, pw, pwb, w_out, layer, *, tm):
    batch, seq, _ = proj3.shape
    ch = GROUP_WIDTH
    n_t = seq // tm
    slab = w_out.shape[1] // (batch * n_t)
    assert slab * batch * n_t == w_out.shape[1] and slab % BF16_ROWS == 0
    vec = pl.BlockSpec((1, ch), lambda b, t: (0, 0))
    return pl.pallas_call(
        functools.partial(_conv_kernel, tm=tm),
        grid=(batch, n_t),
        in_specs=[pl.BlockSpec((None, seq, ch), lambda b, t: (b, 0, COL_B_A // ch)),
                  pl.BlockSpec((None, seq, ch), lambda b, t: (b, 0, COL_B_G // ch)),
                  pl.BlockSpec((CONV_WIDTH, ch), lambda b, t: (0, 0)),
                  vec, vec, vec,
                  pl.BlockSpec((None, ch, ch), lambda b, t: (layer, 0, 0)),
                  vec,
                  pl.BlockSpec((None, slab, w_out.shape[2]), lambda b, t: (layer, b * n_t + t, 0))],
        out_specs=[pl.BlockSpec((None, tm, ch), lambda b, t: (b, t, 0)),
                   pl.BlockSpec((slab, w_out.shape[2]), lambda b, t: (b * n_t + t, 0))],
        out_shape=[jax.ShapeDtypeStruct((batch, seq, ch), BF16),
                   jax.ShapeDtypeStruct(w_out.shape[1:], BF16)],
        scratch_shapes=[pltpu.VMEM((seq + 2 * CONV_HALO, ch), F32),
                        pltpu.VMEM((ROW_CHUNK, ch), F32),
                        pltpu.VMEM((tm, ch), BF16)],
        compiler_params=_cparams(("parallel", "arbitrary")),
        name="conv_mixer",
    )(proj3, proj3, dw, dwb, lng, lnb, pw, pwb, w_out)


def _na_geometry(seq):
    rows = seq // GRID_W
    qr_blk = Q_BLOCK // GRID_W
    mask_add = np.zeros((NA_CLASSES, Q_BLOCK, NA_BAND), np.float32)
    q_local = np.arange(Q_BLOCK)
    p = np.arange(NA_BAND)
    for c, blk in enumerate(_na_class_blocks(seq)):
        bs = int(np.clip(blk * qr_blk - NA_KH // 2, 0, rows - NA_BAND_ROWS))
        q_row = (blk * qr_blk + q_local // GRID_W)[:, None]
        q_col = (q_local % GRID_W)[:, None]
        k_row = (bs + p // GRID_W)[None, :]
        k_col = (p % GRID_W)[None, :]
        win_r = np.clip(q_row - NA_KH // 2, 0, rows - NA_KH)
        win_c = np.clip(q_col - NA_KW // 2, 0, GRID_W - NA_KW)
        ok = (k_row >= win_r) & (k_row < win_r + NA_KH) & (k_col >= win_c) & (k_col < win_c + NA_KW)
        mask_add[c] = np.where(ok, 0.0, -1e30)
    return jnp.asarray(mask_add)


def _na_class_blocks(seq):
    n_blk = seq // Q_BLOCK
    return (0, 1, 2, n_blk - 2, n_blk - 1)


def _rpb_tiles_kernel(rpb_ref, mask_ref, o_ref, t2_ref, *, seq):
    h = pl.program_id(0)
    n_c = 2 * NA_KW - 1
    n_r = 2 * NA_KH - 1
    shape = (GRID_W, LANES)
    lane = lax.broadcasted_iota(jnp.int32, shape, 1)
    q_col = lax.broadcasted_iota(jnp.int32, shape, 0)
    ic = jnp.clip((lane & (GRID_W - 1)) - q_col + (NA_KW - 1), 0, n_c - 1)
    left = lane < GRID_W
    base = h * (n_r * n_c)
    for i in range(NA_T2):
        i_l = min(max(i - 1, 0), n_r - 1)
        i_r = min(max(i, 0), n_r - 1)

        acc = jnp.zeros(shape, F32)
        for j in range(n_c):
            coef = jnp.where(left, rpb_ref[base + i_l * n_c + j], rpb_ref[base + i_r * n_c + j])
            acc = acc + jnp.where(ic == j, coef, 0.0)
        t2_ref[i] = acc

    rows = seq // GRID_W
    qr_blk = Q_BLOCK // GRID_W
    for c, blk in enumerate(_na_class_blocks(seq)):
        bs = min(max(blk * qr_blk - NA_KH // 2, 0), rows - NA_BAND_ROWS)
        for a in range(qr_blk):
            for r2 in range(NA_BAND_ROWS // 2):
                i1 = bs + 2 * r2 - (blk * qr_blk + a) + (NA_KH - 1)
                idx = min(max(i1, -1), NA_T2 - 2) + 1
                rs = slice(a * GRID_W, (a + 1) * GRID_W)
                cs = slice(r2 * LANES, (r2 + 1) * LANES)
                o_ref[c, rs, cs] = (t2_ref[idx] + mask_ref[c, rs, cs]) * LOG2E


def _rpb_tiles(rpb, mask_add, seq):
    heads = rpb.shape[0]
    return pl.pallas_call(
        functools.partial(_rpb_tiles_kernel, seq=seq),
        grid=(heads,),
        in_specs=[pl.BlockSpec(memory_space=pltpu.SMEM),
                  pl.BlockSpec(mask_add.shape, lambda h: (0, 0, 0))],
        out_specs=pl.BlockSpec((None,) + mask_add.shape, lambda h: (h, 0, 0, 0)),
        out_shape=jax.ShapeDtypeStruct((heads,) + mask_add.shape, F32),
        scratch_shapes=[pltpu.VMEM((NA_T2, GRID_W, LANES), F32)],
        compiler_params=_cparams(("parallel",)),
        name="na_rpb_tiles",
    )(rpb.reshape(-1), mask_add)


def _na_kernel(q_ref, k_ref, v_ref, bias_ref, o_ref):
    seq = q_ref.shape[0]
    rows = seq // GRID_W
    qr_blk = Q_BLOCK // GRID_W
    n_blk = seq // Q_BLOCK

    classes = _na_class_blocks(seq)

    def band(blk):
        k0 = min(max(blk * qr_blk - NA_KH // 2, 0), rows - NA_BAND_ROWS) * GRID_W
        return slice(k0, k0 + NA_BAND)

    def scores(blk):
        q = q_ref[blk * Q_BLOCK:(blk + 1) * Q_BLOCK, :]
        return lax.dot_general(q, k_ref[band(blk), :], (((1,), (1,)), ((), ())),
                               preferred_element_type=F32)

    pending = [scores(b) for b in range(NA_LOOKAHEAD)]
    for blk in range(n_blk):
        s = pending.pop(0)
        if blk + NA_LOOKAHEAD < n_blk:
            pending.append(scores(blk + NA_LOOKAHEAD))
        cls = classes.index(blk) if blk in classes else 2
        s = s * (HEAD_DIM ** -0.5 * LOG2E) + bias_ref[cls]
        m = jnp.max(s, axis=1, keepdims=True)
        p = jnp.exp2(s - m)
        l = jnp.sum(p, axis=1, keepdims=True)
        o = jnp.dot(p.astype(BF16), v_ref[band(blk), :], preferred_element_type=F32) / l
        o_ref[blk * Q_BLOCK:(blk + 1) * Q_BLOCK, :] = o.astype(o_ref.dtype)


def _na_mixer(proj3, bias):
    batch, seq, _ = proj3.shape
    heads = GROUP_HEADS

    def head_spec(col):
        return pl.BlockSpec((None, seq, HEAD_DIM), lambda b, h: (b, 0, col // HEAD_DIM + h))

    return pl.pallas_call(
        _na_kernel,
        grid=(batch, heads),
        in_specs=[head_spec(COL_D_Q), head_spec(COL_D_K), head_spec(COL_D_V),
                  pl.BlockSpec((None,) + bias.shape[1:], lambda b, h: (h, 0, 0, 0))],
        out_specs=pl.BlockSpec((None, seq, HEAD_DIM), lambda b, h: (b, 0, h)),
        out_shape=jax.ShapeDtypeStruct((batch, seq, GROUP_WIDTH), BF16),
        compiler_params=_cparams(("parallel", "parallel")),
        name="na_mixer",
    )(proj3, proj3, proj3, bias)


def kernel(x, norm_mix_pre, norm_mix_post, norm_ffn_pre, norm_ffn_post, w_in, w_out, diff_lambda, diff_subln, conv_dw, conv_dw_b, conv_ln_g, conv_ln_b, conv_pw, conv_pw_b, gqa_q_norm, gqa_k_norm, na_rpb, ffn_gate, ffn_up, ffn_down):
    batch, seq, d = x.shape
    depth = w_in.shape[0]
    tokens = batch * seq

    t = np.arange(seq)
    diff_tables = _rope_tables(t, t)
    axial_tables = _rope_tables(t // GRID_W, t % GRID_W)
    mask_add = _na_geometry(seq)

    def row(v):
        return v.reshape(1, -1)

    xf = x.reshape(tokens, d)
    h = _rmsnorm(xf, row(norm_mix_pre[0]), tm=RMSNORM_TM)
    for l in range(depth):
        lam_init = 0.8 - 0.6 * math.exp(-0.3 * l)
        proj = _in_proj(h, w_in, l, tm=IN_PROJ_TM, tn=IN_PROJ_TN)
        proj3 = proj.reshape(batch, seq, IN_COLS)

        out_a = _flash_call(
            functools.partial(_diff_attn_kernel, tq=ATTN_TQ, tk=ATTN_TK, lam_init=lam_init),
            proj3, diff_tables, (diff_lambda[l], diff_subln[l].reshape(-1, 1)),
            n_groups=GROUP_HEADS, q_width=HEAD_DIM, q_col=COL_A_Q, k_col=COL_A_K, v_col=COL_A_V,
            out_width=HEAD_DIM, tq=ATTN_TQ, name="diff_attn")
        out_b, w_out_bf16 = _conv_mixer(
            proj3, conv_dw[l], row(conv_dw_b[l]), row(conv_ln_g[l]), row(conv_ln_b[l]),
            conv_pw, row(conv_pw_b[l]), w_out, l, tm=CONV_TM)
        out_c = _flash_call(
            functools.partial(_gqa_attn_kernel, tq=ATTN_TQ, tk=ATTN_TK),
            proj3, axial_tables, (row(gqa_q_norm[l]), row(gqa_k_norm[l])),
            n_groups=GQA_KV_HEADS, q_width=2 * HEAD_DIM, q_col=COL_C_Q, k_col=COL_C_K, v_col=COL_C_V,
            out_width=2 * HEAD_DIM, tq=ATTN_TQ, name="gqa_attn")
        out_d = _na_mixer(proj3, _rpb_tiles(na_rpb[l], mask_add, seq))

        parts = [o.reshape(tokens, GROUP_WIDTH) for o in (out_a, out_b, out_c, out_d)]
        xf, h = _out_proj(parts, w_out_bf16, row(norm_mix_post[l]), xf,
                          row(norm_ffn_pre[l]), tm=OUT_PROJ_TM)

        act, w_down = _ffn_up(h, ffn_gate, ffn_up, ffn_down, l, tm=FFN_UP_TM, tn=FFN_UP_TN)
        next_gain = row(norm_mix_pre[l + 1]) if l + 1 < depth else None
        xf, h = _ffn_down(act, w_down, row(norm_ffn_post[l]), xf, next_gain,
                          tm=FFN_DOWN_TM)
    return xf.reshape(batch, seq, d)
```
